```python
import math
import jax
import jax.numpy as jnp
from jax import lax
import numpy as np

D_MODEL = 2048
BATCH = 4
SEQ = 2048
DEPTH = 4
DEC_BATCH = 8
DEC_SEQ = 1
PAST_LEN = 16384
PAGE_SIZE = 128

N_META = 16
MIX_W = D_MODEL
GROUP_W = MIX_W // 4
S5_W = GROUP_W
S5_CH = 16
S5_GROUPS = S5_W // S5_CH
S5_STATE = 64
FOX_W = GROUP_W
FOX_HD = 128
FOX_HEADS = FOX_W // FOX_HD
FOX_BLOCK = 128
RET_W = GROUP_W
RET_HD = 128
RET_HEADS = RET_W // RET_HD
RET_CHUNK = 128
LRU_W = MIX_W - S5_W - FOX_W - RET_W
LRU_HEADS = 8
LRU_BLK = LRU_W // LRU_HEADS
LRU_C = 8.0
CONV_W = 4
PEER_HEADS = 8
PEER_NKEYS = 128
PEER_EXPERTS = PEER_NKEYS * PEER_NKEYS
PEER_DK = 256
PEER_TOPK = 16
PEER_BLOCK = 64
EPS = 1e-6
IN_SIZES = (S5_W, FOX_W, FOX_W, FOX_W, FOX_HEADS, RET_W, RET_W, RET_W, RET_W, LRU_W, LRU_W)
IN_W = sum(IN_SIZES)
F32 = jnp.float32

kernel_name = 'hybrid_s5_fox_ret_lru_peer_step'


def rmsnorm(x, g):
    xf = x.astype(F32)
    y = xf * lax.rsqrt(jnp.mean(xf * xf, axis=-1, keepdims=True) + EPS)
    return (y * g.astype(F32)).astype(x.dtype)


def linear_scan(a, b, h0):
    def combine(c1, c2):
        a1, b1 = c1
        a2, b2 = c2
        return a1 * a2, a2 * b1 + b2
    a_cum, b_cum = lax.associative_scan(combine, (a, b), axis=1)
    return b_cum + a_cum * h0[:, None]


def s5_mixer(u, lam_re, lam_im, log_dt, b_re, b_im, c_re, c_im, d, glu_w, glu_b, h0_re, h0_im):
    bsz, L, _ = u.shape
    lam = lax.complex(jnp.minimum(lam_re.astype(F32), -1e-4), lam_im.astype(F32))
    dt = jnp.exp(log_dt.astype(F32))[:, None]
    lam_bar = jnp.exp(lam * dt)
    b_bar = ((lam_bar - 1.0) / lam)[:, :, None] * lax.complex(b_re.astype(F32), b_im.astype(F32))
    ug = u.astype(F32).reshape(bsz, L, S5_GROUPS, S5_CH)
    bu = jnp.einsum('gpc,blgc->blgp', b_bar, ug.astype(jnp.complex64))
    h0 = lax.complex(h0_re.astype(F32), h0_im.astype(F32))
    h = linear_scan(jnp.broadcast_to(lam_bar, bu.shape), bu, h0)
    cm = lax.complex(c_re.astype(F32), c_im.astype(F32))
    y = jnp.real(jnp.einsum('gcp,blgp->blgc', cm, h)) + d.astype(F32).reshape(S5_GROUPS, S5_CH) * ug
    y = jax.nn.gelu(y.reshape(bsz, L, S5_W))
    y = y * jax.nn.sigmoid(y @ glu_w.astype(F32) + glu_b.astype(F32))
    h_last = h[:, -1]
    return y.astype(u.dtype), jnp.real(h_last), jnp.imag(h_last)


def fox_block(q, cq, pq, k, v, ck, pk):
    s = jnp.einsum('bqhd,bshd->bhqs', q, k).astype(F32) * (FOX_HD ** -0.5)
    s = s + jnp.transpose(cq, (0, 2, 1))[..., None] - jnp.transpose(ck, (0, 2, 1))[:, :, None, :]
    s = jnp.where(pk[None, :] <= pq[:, None], s, -jnp.inf)
    p = jax.nn.softmax(s, axis=-1)
    return jnp.einsum('bhqs,bshd->bqhd', p.astype(v.dtype), v)


def fox_prompt(q, k, v, logf):
    bsz, L = q.shape[0], q.shape[1]
    c = jnp.cumsum(logf, axis=1)
    pos = jnp.arange(L)
    o_meta = fox_block(q[:, :N_META], c[:, :N_META], pos[:N_META], k, v, c, pos)
    nb = (L - N_META) // FOX_BLOCK
    qb = q[:, N_META:].reshape(bsz, nb, FOX_BLOCK, FOX_HEADS, FOX_HD).transpose(1, 0, 2, 3, 4)
    cb = c[:, N_META:].reshape(bsz, nb, FOX_BLOCK, FOX_HEADS).transpose(1, 0, 2, 3)
    pb = pos[N_META:].reshape(nb, FOX_BLOCK)
    o_real = lax.map(lambda blk: fox_block(blk[0], blk[1], blk[2], k, v, c, pos), (qb, cb, pb))
    o_real = o_real.transpose(1, 0, 2, 3, 4).reshape(bsz, L - N_META, FOX_HEADS, FOX_HD)
    return jnp.concatenate([o_meta, o_real], axis=1)


def fox_sample(q, k, v, logf, k_past, v_past, logf_past):
    lp = logf_past.astype(F32)
    rev = jnp.flip(jnp.cumsum(jnp.flip(lp, axis=1), axis=1), axis=1)
    c_past = lp - rev
    c_new = jnp.cumsum(logf, axis=1)
    P, n_new = k_past.shape[1], q.shape[1]
    k_all = jnp.concatenate([k_past.astype(k.dtype), k], axis=1)
    v_all = jnp.concatenate([v_past.astype(v.dtype), v], axis=1)
    c_all = jnp.concatenate([c_past, c_new], axis=1)
    return fox_block(q, c_new, P + jnp.arange(n_new), k_all, v_all, c_all, jnp.arange(P + n_new))


def rotary(x, pos):
    half = x.shape[-1] // 2
    inv = 1.0 / (10000.0 ** (jnp.arange(half, dtype=F32) / half))
    ang = pos.astype(F32)[:, None] * inv[None, :]
    cos, sin = jnp.cos(ang)[None, :, None, :], jnp.sin(ang)[None, :, None, :]
    xf = x.astype(F32)
    x1, x2 = xf[..., :half], xf[..., half:]
    return jnp.concatenate([x1 * cos - x2 * sin, x1 * sin + x2 * cos], axis=-1)


def ret_log_gamma():
    return jnp.log(1.0 - 2.0 ** (-5.0 - jnp.arange(RET_HEADS, dtype=F32)))


def retention_chunks(q, k, v, s0, chunk):
    bsz, L, H, _ = q.shape
    n = L // chunk
    lg = ret_log_gamma()
    i = jnp.arange(chunk, dtype=F32)
    diff = i[:, None] - i[None, :]
    decay = jnp.exp(jnp.where(diff[None] >= 0, diff[None] * lg[:, None, None], -jnp.inf))
    q_dec = jnp.exp((i[:, None] + 1.0) * lg[None, :])
    k_dec = jnp.exp((chunk - 1.0 - i[:, None]) * lg[None, :])
    c_dec = jnp.exp(chunk * lg)

    def to_chunks(t):
        return t.astype(F32).reshape(bsz, n, chunk, H, t.shape[-1]).transpose(1, 0, 2, 3, 4)

    def step(s, blk):
        qc, kc, vc = blk
        att = jnp.einsum('bihd,bjhd->bhij', qc, kc) * decay
        o = jnp.einsum('bhij,bjhe->bihe', att, vc) + jnp.einsum('bihd,bhde->bihe', qc * q_dec[None, :, :, None], s)
        s = s * c_dec[None, :, None, None] + jnp.einsum('bjhd,bjhe->bhde', kc * k_dec[None, :, :, None], vc)
        return s, o

    s, o = lax.scan(step, s0.astype(F32), (to_chunks(q), to_chunks(k), to_chunks(v)))
    return o.transpose(1, 0, 2, 3, 4).reshape(bsz, L, H, -1), s


def ret_mixer(q, k, v, g, norm_g, s0, pos, prompt):
    q = rotary(q, pos)
    k = rotary(k, pos) * (RET_HD ** -0.5)
    v = v.astype(F32)
    if prompt:
        o_m, s = retention_chunks(q[:, :N_META], k[:, :N_META], v[:, :N_META], s0, N_META)
        o_r, s = retention_chunks(q[:, N_META:], k[:, N_META:], v[:, N_META:], s, RET_CHUNK)
        o = jnp.concatenate([o_m, o_r], axis=1)
    else:
        o, s = retention_chunks(q, k, v, s0, q.shape[1])
    bsz, L = o.shape[0], o.shape[1]
    o = rmsnorm(o, norm_g.reshape(RET_HEADS, RET_HD)).reshape(bsz, L, RET_W)
    y = o * jax.nn.silu(g.astype(F32))
    return y.astype(g.dtype), s


def lru_mixer(xb, gb, conv_w, conv_b, wa, ba, wx, bx, lam, h0, buf0):
    bsz, L, _ = xb.shape
    xc = jnp.concatenate([buf0.astype(xb.dtype), xb], axis=1)
    conv = conv_b.astype(F32) + sum(xc[:, j:j + L].astype(F32) * conv_w[j].astype(F32) for j in range(CONV_W))
    new_buf = xc[:, xc.shape[1] - (CONV_W - 1):]
    xh = conv.reshape(bsz, L, LRU_HEADS, LRU_BLK)
    r = jax.nn.sigmoid(jnp.einsum('blhi,hij->blhj', xh, wa.astype(F32)).reshape(bsz, L, LRU_W) + ba.astype(F32))
    i = jax.nn.sigmoid(jnp.einsum('blhi,hij->blhj', xh, wx.astype(F32)).reshape(bsz, L, LRU_W) + bx.astype(F32))
    log_a = -LRU_C * r * jax.nn.softplus(-lam.astype(F32))
    a = jnp.exp(log_a)
    x_in = jnp.sqrt(-jnp.expm1(2.0 * log_a)) * (i * conv)
    h = linear_scan(a, x_in, h0.astype(F32))
    y = h * jax.nn.gelu(gb.astype(F32))
    return y.astype(xb.dtype), h[:, -1], new_buf


def peer(x, wq, subkeys, u_tab, v_tab):
    T = x.shape[0]
    q = (x @ wq).reshape(T, PEER_HEADS, 2, PEER_DK // 2)
    s = jnp.einsum('thcd,hckd->thck', q, subkeys).astype(F32)
    s1, i1 = lax.top_k(s[:, :, 0], PEER_TOPK)
    s2, i2 = lax.top_k(s[:, :, 1], PEER_TOPK)
    cand = (s1[..., :, None] + s2[..., None, :]).reshape(T, PEER_HEADS, PEER_TOPK * PEER_TOPK)
    cidx = (i1[..., :, None] * PEER_NKEYS + i2[..., None, :]).reshape(T, PEER_HEADS, PEER_TOPK * PEER_TOPK)
    top_s, j = lax.top_k(cand, PEER_TOPK)
    idx = jnp.take_along_axis(cidx, j, axis=-1)
    gate = jax.nn.softmax(top_s, axis=-1)
    n_blk = -(-T // PEER_BLOCK)
    pad = n_blk * PEER_BLOCK - T
    xp = jnp.pad(x, ((0, pad), (0, 0))).reshape(n_blk, PEER_BLOCK, D_MODEL)
    ip = jnp.pad(idx, ((0, pad), (0, 0), (0, 0))).reshape(n_blk, PEER_BLOCK, PEER_HEADS, PEER_TOPK)
    gp = jnp.pad(gate, ((0, pad), (0, 0), (0, 0))).reshape(n_blk, PEER_BLOCK, PEER_HEADS, PEER_TOPK)

    def expert_block(blk):
        xb, ib, gb = blk
        act = jax.nn.gelu(jnp.einsum('td,thkd->thk', xb, u_tab[ib]).astype(F32))
        return jnp.einsum('thk,thkd->td', (act * gb).astype(xb.dtype), v_tab[ib])

    out = lax.map(expert_block, (xp, ip, gp)).reshape(n_blk * PEER_BLOCK, D_MODEL)
    return out[:T]


def trunk_layer(h, lp, st, pos):
    (norm_mix_g, w_in, w_out, s5p, fox_bf, ret_norm_g, lrup, norm_ffn_g, peerp) = lp
    fox_past, s5_re0, s5_im0, ret_s0, lru_h0, conv_buf0 = st
    prompt = fox_past is None
    bsz, L, _ = h.shape
    hn = rmsnorm(h, norm_mix_g)
    splits = [sum(IN_SIZES[:j + 1]) for j in range(len(IN_SIZES) - 1)]
    (s5_u, fq, fk, fv, ff, rq, rk, rv, rg, lx, lg) = jnp.split(hn @ w_in, splits, axis=-1)
    y_s5, s5_re, s5_im = s5_mixer(s5_u, *s5p, s5_re0, s5_im0)
    fq = fq.reshape(bsz, L, FOX_HEADS, FOX_HD)
    fk = fk.reshape(bsz, L, FOX_HEADS, FOX_HD)
    fv = fv.reshape(bsz, L, FOX_HEADS, FOX_HD)
    logf = jax.nn.log_sigmoid(ff.astype(F32) + fox_bf.astype(F32))
    if prompt:
        y_fox = fox_prompt(fq, fk, fv, logf)
    else:
        y_fox = fox_sample(fq, fk, fv, logf, *fox_past)
    rq = rq.reshape(bsz, L, RET_HEADS, RET_HD)
    rk = rk.reshape(bsz, L, RET_HEADS, RET_HD)
    rv = rv.reshape(bsz, L, RET_HEADS, RET_HD)
    y_ret, ret_s = ret_mixer(rq, rk, rv, rg, ret_norm_g, ret_s0, pos, prompt)
    y_lru, lru_h, conv_buf = lru_mixer(lx, lg, *lrup, lru_h0, conv_buf0)
    mix = jnp.concatenate([y_s5.astype(h.dtype), y_fox.reshape(bsz, L, FOX_W).astype(h.dtype),
                           y_ret.astype(h.dtype), y_lru.astype(h.dtype)], axis=-1)
    h = h + mix @ w_out
    ffn = peer(rmsnorm(h, norm_ffn_g).reshape(bsz * L, D_MODEL), *peerp)
    h = h + ffn.reshape(bsz, L, D_MODEL).astype(h.dtype)
    return h, (fk, fv, logf, s5_re, s5_im, ret_s, lru_h, conv_buf)


def setup_inputs(seed: int = 0) -> dict:
    key = jax.random.key(seed)
    ks = iter(jax.random.split(key, 64))

    def normal(shape, scale=1.0):
        return jax.random.normal(next(ks), shape, F32) * scale

    def uniform(shape, lo, hi):
        return jax.random.uniform(next(ks), shape, F32, lo, hi)

    n_pages = PAST_LEN // PAGE_SIZE
    n_used = DEC_BATCH * n_pages
    n_pool = n_used + max(1, n_used // 4)
    perm = jax.random.permutation(next(ks), n_pool)
    page_table = perm[:n_used].reshape(DEC_BATCH, n_pages).astype(jnp.int32)
    lam_im_base = math.pi * jnp.arange(S5_STATE, dtype=F32)
    return {
        'x_prompt': normal((BATCH, SEQ, D_MODEL)),
        'x_sample': normal((DEC_BATCH, DEC_SEQ, D_MODEL)),
        'cache_fox_k': normal((DEPTH, n_pool, PAGE_SIZE, FOX_HEADS, FOX_HD)),
        'cache_fox_v': normal((DEPTH, n_pool, PAGE_SIZE, FOX_HEADS, FOX_HD)),
        'cache_fox_logf': jax.nn.log_sigmoid(8.0 + normal((DEPTH, n_pool, PAGE_SIZE, FOX_HEADS), 0.5)),
        'page_table': page_table,
        'state_s5_re': normal((DEPTH, DEC_BATCH, S5_GROUPS, S5_STATE), 0.5),
        'state_s5_im': normal((DEPTH, DEC_BATCH, S5_GROUPS, S5_STATE), 0.5),
        'state_ret': normal((DEPTH, DEC_BATCH, RET_HEADS, RET_HD, RET_HD), 0.1),
        'state_lru': normal((DEPTH, DEC_BATCH, LRU_W), 0.5),
        'state_conv': normal((DEPTH, DEC_BATCH, CONV_W - 1, LRU_W)),
        'meta_tokens': normal((N_META, D_MODEL)),
        'norm_mix_g': 1.0 + normal((DEPTH, D_MODEL), 0.01),
        'w_in': normal((DEPTH, D_MODEL, IN_W), D_MODEL ** -0.5),
        'w_out': normal((DEPTH, MIX_W, D_MODEL), MIX_W ** -0.5),
        's5_lam_re': -0.5 + normal((DEPTH, S5_GROUPS, S5_STATE), 0.01),
        's5_lam_im': lam_im_base + normal((DEPTH, S5_GROUPS, S5_STATE), 0.01),
        's5_log_dt': uniform((DEPTH, S5_GROUPS), math.log(0.001), math.log(0.1)),
        's5_b_re': normal((DEPTH, S5_GROUPS, S5_STATE, S5_CH), (2 * S5_CH) ** -0.5),
        's5_b_im': normal((DEPTH, S5_GROUPS, S5_STATE, S5_CH), (2 * S5_CH) ** -0.5),
        's5_c_re': normal((DEPTH, S5_GROUPS, S5_CH, S5_STATE), (2 * S5_STATE) ** -0.5),
        's5_c_im': normal((DEPTH, S5_GROUPS, S5_CH, S5_STATE), (2 * S5_STATE) ** -0.5),
        's5_d': normal((DEPTH, S5_W)),
        's5_glu_w': normal((DEPTH, S5_W, S5_W), S5_W ** -0.5),
        's5_glu_b': normal((DEPTH, S5_W), 0.01),
        'fox_bf': 4.0 + normal((DEPTH, FOX_HEADS), 0.1),
        'ret_norm_g': 1.0 + normal((DEPTH, RET_W), 0.01),
        'lru_conv_w': normal((DEPTH, CONV_W, LRU_W), CONV_W ** -0.5),
        'lru_conv_b': normal((DEPTH, LRU_W), 0.01),
        'lru_wa': normal((DEPTH, LRU_HEADS, LRU_BLK, LRU_BLK), LRU_BLK ** -0.5),
        'lru_ba': normal((DEPTH, LRU_W), 0.01),
        'lru_wx': normal((DEPTH, LRU_HEADS, LRU_BLK, LRU_BLK), LRU_BLK ** -0.5),
        'lru_bx': normal((DEPTH, LRU_W), 0.01),
        'lru_lam': uniform((DEPTH, LRU_W), 4.3, 9.0),
        'norm_ffn_g': 1.0 + normal((DEPTH, D_MODEL), 0.01),
        'peer_wq': normal((DEPTH, D_MODEL, PEER_HEADS * PEER_DK), D_MODEL ** -0.5),
        'peer_subkeys': normal((DEPTH, PEER_HEADS, 2, PEER_NKEYS, PEER_DK // 2), (PEER_DK // 2) ** -0.5),
        'peer_u': normal((DEPTH, PEER_EXPERTS, D_MODEL), D_MODEL ** -0.5),
        'peer_v': normal((DEPTH, PEER_EXPERTS, D_MODEL), (PEER_HEADS * PEER_TOPK) ** -0.5),
        'norm_final_g': 1.0 + normal((D_MODEL,), 0.01),
    }


def reference(x_prompt, x_sample, cache_fox_k, cache_fox_v, cache_fox_logf, page_table,
              state_s5_re, state_s5_im, state_ret, state_lru, state_conv,
              meta_tokens, norm_mix_g, w_in, w_out,
              s5_lam_re, s5_lam_im, s5_log_dt, s5_b_re, s5_b_im, s5_c_re, s5_c_im, s5_d, s5_glu_w, s5_glu_b,
              fox_bf, ret_norm_g,
              lru_conv_w, lru_conv_b, lru_wa, lru_ba, lru_wx, lru_bx, lru_lam,
              norm_ffn_g, peer_wq, peer_subkeys, peer_u, peer_v, norm_final_g):
    bp, bs = x_prompt.shape[0], x_sample.shape[0]
    meta = jnp.broadcast_to(meta_tokens[None].astype(x_prompt.dtype), (bp, N_META, D_MODEL))
    h_p = jnp.concatenate([meta, x_prompt], axis=1)
    h_s = x_sample
    pos_p = jnp.arange(h_p.shape[1])
    pos_s = PAST_LEN + jnp.arange(x_sample.shape[1])
    new_p, new_s = [], []
    for l in range(DEPTH):
        lp = (norm_mix_g[l], w_in[l], w_out[l],
              (s5_lam_re[l], s5_lam_im[l], s5_log_dt[l], s5_b_re[l], s5_b_im[l],
               s5_c_re[l], s5_c_im[l], s5_d[l], s5_glu_w[l], s5_glu_b[l]),
              fox_bf[l], ret_norm_g[l],
              (lru_conv_w[l], lru_conv_b[l], lru_wa[l], lru_ba[l], lru_wx[l], lru_bx[l], lru_lam[l]),
              norm_ffn_g[l], (peer_wq[l], peer_subkeys[l], peer_u[l], peer_v[l]))
        st_p = (None,
                jnp.zeros((bp, S5_GROUPS, S5_STATE), F32), jnp.zeros((bp, S5_GROUPS, S5_STATE), F32),
                jnp.zeros((bp, RET_HEADS, RET_HD, RET_HD), F32), jnp.zeros((bp, LRU_W), F32),
                jnp.zeros((bp, CONV_W - 1, LRU_W), x_prompt.dtype))
        h_p, out_p = trunk_layer(h_p, lp, st_p, pos_p)
        past = tuple(c[l][page_table].reshape((bs, -1) + c.shape[3:])
                     for c in (cache_fox_k, cache_fox_v, cache_fox_logf))
        st_s = (past, state_s5_re[l], state_s5_im[l], state_ret[l], state_lru[l], state_conv[l])
        h_s, out_s = trunk_layer(h_s, lp, st_s, pos_s)
        new_p.append(out_p)
        new_s.append(out_s)

    def stk(outs, j):
        return jnp.stack([o[j] for o in outs], axis=0)

    y_prompt = rmsnorm(h_p, norm_final_g)[:, N_META:]
    y_sample = rmsnorm(h_s, norm_final_g)
    return (y_prompt, y_sample,
            stk(new_p, 0), stk(new_p, 1), stk(new_p, 2), stk(new_p, 3), stk(new_p, 4), stk(new_p, 5), stk(new_p, 6), stk(new_p, 7),
            stk(new_s, 0), stk(new_s, 1), stk(new_s, 2), stk(new_s, 3), stk(new_s, 4), stk(new_s, 5), stk(new_s, 6), stk(new_s, 7))
```

```python
import functools
import math

import jax
import jax.numpy as jnp
from jax import lax
from jax.experimental import pallas as pl
from jax.experimental.pallas import tpu as pltpu

F32 = jnp.float32
BF16 = jnp.bfloat16
NEG_INF = float("-inf")

EPS = 1e-6
N_META = 16
PAST_LEN = 16384
PAGE_SIZE = 128
GROUP_W = 512
HEAD_D = 128
N_HEADS = 4
S5_STATE_W = 2048
LRU_C = 8.0
CONV_W = 4
PEER_HEADS = 8
PEER_NKEYS = 128
PEER_TOPK = 16
LANES = 128
SUBLANES = 8
LOG2E = 1.4426950408889634
MIB = 1024 * 1024

COL_S5, COL_FQ, COL_FK, COL_FV, COL_RQ, COL_RK, COL_RV, COL_RG, COL_LX, COL_LG = range(10)


def _cparams(sem, vmem_mib):
    return pltpu.CompilerParams(dimension_semantics=sem, vmem_limit_bytes=vmem_mib * MIB)


def _dot(a, b):
    return jnp.dot(a, b, preferred_element_type=F32)


def _dot_nt(a, b):
    return lax.dot_general(a, b, (((1,), (1,)), ((), ())), preferred_element_type=F32)


def _dot_tn(a, b):
    return lax.dot_general(a, b, (((0,), (0,)), ((), ())), preferred_element_type=F32)


def _dot_f32(a, b):
    return jnp.dot(a, b, preferred_element_type=F32, precision=lax.Precision.HIGHEST)


def _gelu(x):
    return 0.5 * x * (1.0 + jnp.tanh(0.7978845608028654 * (x + 0.044715 * (x * x * x))))


def _row_tile(n, target):
    if n <= target:
        return n
    best = None
    for t in range(SUBLANES, target + 1, SUBLANES):
        if n % t == 0:
            best = t
    assert best is not None, n
    return best


def _in_proj_kernel(x_ref, g_ref, w_ref, wff_ref, o_ref, off_ref, xn_ref):
    @pl.when(pl.program_id(1) == 0)
    def _():
        x = x_ref[...]
        ms = jnp.mean(x * x, axis=-1, keepdims=True)
        xn = (x * lax.rsqrt(ms + EPS) * g_ref[...]).astype(BF16)
        xn_ref[...] = xn
        off_ref[...] = _dot(xn, wff_ref[...])
    o_ref[...] = _dot(xn_ref[...], w_ref[...])


def _in_proj(x, g, w_main, w_ff):
    t, d = x.shape
    n = w_main.shape[1]
    tm, tn = _row_tile(t, 1032), 512
    return pl.pallas_call(
        _in_proj_kernel,
        grid=(t // tm, n // tn),
        in_specs=[pl.BlockSpec((tm, d), lambda i, j: (i, 0)),
                  pl.BlockSpec((1, d), lambda i, j: (0, 0)),
                  pl.BlockSpec((d, tn), lambda i, j: (0, j)),
                  pl.BlockSpec((d, LANES), lambda i, j: (0, 0))],
        out_specs=[pl.BlockSpec((tm, tn), lambda i, j: (i, j)),
                   pl.BlockSpec((tm, LANES), lambda i, j: (i, 0))],
        out_shape=[jax.ShapeDtypeStruct((t, n), F32), jax.ShapeDtypeStruct((t, LANES), F32)],
        scratch_shapes=[pltpu.VMEM((tm, d), BF16)],
        compiler_params=_cparams(("arbitrary", "arbitrary"), 48),
        name="in_proj",
    )(x, g.reshape(1, d), w_main, w_ff)


def _q_proj_kernel(x_ref, g_ref, w_ref, o_ref, xn_ref):
    @pl.when(pl.program_id(1) == 0)
    def _():
        x = x_ref[...]
        ms = jnp.mean(x * x, axis=-1, keepdims=True)
        xn_ref[...] = (x * lax.rsqrt(ms + EPS) * g_ref[...]).astype(BF16)
    o_ref[...] = _dot(xn_ref[...], w_ref[...])


def _q_proj(x, g, w):
    t, d = x.shape
    n = w.shape[1]
    tm, tn = _row_tile(t, 1032), 512
    return pl.pallas_call(
        _q_proj_kernel,
        grid=(t // tm, n // tn),
        in_specs=[pl.BlockSpec((tm, d), lambda i, j: (i, 0)),
                  pl.BlockSpec((1, d), lambda i, j: (0, 0)),
                  pl.BlockSpec((d, tn), lambda i, j: (0, j))],
        out_specs=[pl.BlockSpec((tm, tn), lambda i, j: (i, j)),
                   pl.BlockSpec((tm, d), lambda i, j: (i, 0))],
        out_shape=[jax.ShapeDtypeStruct((t, n), F32), jax.ShapeDtypeStruct((t, d), BF16)],
        compiler_params=_cparams(("arbitrary", "arbitrary"), 48),
        name="q_proj",
    )(x, g.reshape(1, d), w)


def _out_proj_kernel(h_ref, y0_ref, y1_ref, y2_ref, y3_ref, w_ref, o_ref):
    acc = h_ref[...]
    for gi, y_ref in enumerate((y0_ref, y1_ref, y2_ref, y3_ref)):
        acc = acc + _dot(y_ref[...].astype(BF16), w_ref[gi])
    o_ref[...] = acc


def _out_proj(h, ys, w):
    t, d = h.shape
    tm, tn = _row_tile(t, 1032), 512
    yspec = pl.BlockSpec((tm, GROUP_W), lambda i, j: (i, 0))
    return pl.pallas_call(
        _out_proj_kernel,
        grid=(t // tm, d // tn),
        in_specs=[pl.BlockSpec((tm, tn), lambda i, j: (i, j)), yspec, yspec, yspec, yspec,
                  pl.BlockSpec((4, GROUP_W, tn), lambda i, j: (0, 0, j))],
        out_specs=pl.BlockSpec((tm, tn), lambda i, j: (i, j)),
        out_shape=jax.ShapeDtypeStruct((t, d), F32),
        compiler_params=_cparams(("arbitrary", "arbitrary"), 48),
        name="out_proj",
    )(h, *ys, w)


def _rmsnorm_kernel(x_ref, g_ref, o_ref):
    x = x_ref[...]
    ms = jnp.mean(x * x, axis=-1, keepdims=True)
    o_ref[...] = x * lax.rsqrt(ms + EPS) * g_ref[...]


def _rmsnorm(x, g):
    t, d = x.shape
    tm = _row_tile(t, 1032)
    return pl.pallas_call(
        _rmsnorm_kernel,
        grid=(t // tm,),
        in_specs=[pl.BlockSpec((tm, d), lambda i: (i, 0)), pl.BlockSpec((1, d), lambda i: (0, 0))],
        out_specs=pl.BlockSpec((tm, d), lambda i: (i, 0)),
        out_shape=jax.ShapeDtypeStruct((t, d), F32),
        compiler_params=_cparams(("arbitrary",), 48),
        name="final_norm",
    )(x, g.reshape(1, d))


def _s5_kernel(u_ref, h0re_ref, h0im_ref, lre_ref, lim_ref, bw_ref, cw_ref, d_ref, gw_ref, gb_ref,
               y_ref, hre_ref, him_ref, sre_ref, sim_ref, *, nb, tc):
    single = tc == 1

    @pl.when(pl.program_id(0) == 0)
    def _():
        hre_ref[...] = h0re_ref[...]
        him_ref[...] = h0im_ref[...]

    def get_u(b):
        return u_ref[...] if single else u_ref[b]

    seqs = (0,) if single else tuple(range(nb))
    nrow = nb if single else tc
    for b in seqs:
        ub = get_u(b).astype(BF16)
        for r in range(4):
            bu = _dot(ub[:, r * LANES:(r + 1) * LANES], bw_ref[r])
            for q in range(4):
                sre_ref[4 * r + q, b * nrow:(b + 1) * nrow, :] = bu[:, q * LANES:(q + 1) * LANES]
                sim_ref[4 * r + q, b * nrow:(b + 1) * nrow, :] = bu[:, GROUP_W + q * LANES:GROUP_W + (q + 1) * LANES]

    for r in range(4):
        tiles = tuple(range(4 * r, 4 * r + 4))
        lanes = [slice(lt * LANES, (lt + 1) * LANES) for lt in tiles]
        lr = [jnp.broadcast_to(lre_ref[:, ln], (nb, LANES)) for ln in lanes]
        li = [jnp.broadcast_to(lim_ref[:, ln], (nb, LANES)) for ln in lanes]

        def body(t, carry, tiles=tiles, lr=lr, li=li):
            idx = pl.ds(t, nb, stride=tc)
            new = []
            for q, lt in enumerate(tiles):
                hr, hi = carry[q]
                nr = lr[q] * hr - li[q] * hi + sre_ref[lt, idx, :]
                ni = lr[q] * hi + li[q] * hr + sim_ref[lt, idx, :]
                sre_ref[lt, idx, :] = nr
                sim_ref[lt, idx, :] = ni
                new.append((nr, ni))
            return tuple(new)

        fin = lax.fori_loop(0, tc, body, tuple((hre_ref[:, ln], him_ref[:, ln]) for ln in lanes))
        for q, ln in enumerate(lanes):
            hre_ref[:, ln] = fin[q][0]
            him_ref[:, ln] = fin[q][1]

    for b in seqs:
        rows = slice(b * nrow, (b + 1) * nrow)
        ys = []
        for r in range(4):
            hre = jnp.concatenate([sre_ref[4 * r + q, rows, :] for q in range(4)], axis=1).astype(BF16)
            him = jnp.concatenate([sim_ref[4 * r + q, rows, :] for q in range(4)], axis=1).astype(BF16)
            ys.append(_dot(hre, cw_ref[r, :GROUP_W, :]) + _dot(him, cw_ref[r, GROUP_W:, :]))
        y = jnp.concatenate(ys, axis=1) + d_ref[...] * get_u(b)
        y = _gelu(y)
        z = _dot(y.astype(BF16), gw_ref[...]) + gb_ref[...]
        out = y * jax.nn.sigmoid(z)
        if single:
            y_ref[...] = out
        else:
            y_ref[b] = out


def _s5_weights(lam_re, lam_im, log_dt, b_re, b_im, c_re, c_im):
    lam = lax.complex(jnp.minimum(lam_re, -1e-4), lam_im)
    dt = jnp.exp(log_dt)[:, None]
    lam_bar = jnp.exp(lam * dt)
    b_bar = ((lam_bar - 1.0) / lam)[:, :, None] * lax.complex(b_re, b_im)
    eye = jnp.eye(8, dtype=F32)

    def pack_b(m):
        m = m.reshape(4, 8, 64, 16)
        return jnp.einsum("rgpc,gh->rgchp", m, eye).reshape(4, LANES, GROUP_W)

    def pack_c(m):
        m = m.reshape(4, 8, 16, 64)
        return jnp.einsum("rgcp,hg->rhpgc", m, eye).reshape(4, GROUP_W, LANES)

    bw = jnp.concatenate([pack_b(jnp.real(b_bar)), pack_b(jnp.imag(b_bar))], axis=2).astype(BF16)
    cw = jnp.concatenate([pack_c(c_re), -pack_c(c_im)], axis=1).astype(BF16)
    return (jnp.real(lam_bar).reshape(1, S5_STATE_W), jnp.imag(lam_bar).reshape(1, S5_STATE_W), bw, cw)


def _s5(proj, h0re, h0im, wts, d, glu_w, glu_b, *, nb, seq):
    lre, lim, bw, cw = wts
    single = seq == 1
    tc = 1 if single else _row_tile(seq, 344)
    nchunk = seq // tc
    full = lambda shape: pl.BlockSpec(shape, lambda c: tuple(0 for _ in shape))
    if single:
        u_in = proj
        u_spec = pl.BlockSpec((nb, GROUP_W), lambda c: (0, COL_S5))
        y_spec = pl.BlockSpec((nb, GROUP_W), lambda c: (0, 0))
        y_shape = jax.ShapeDtypeStruct((nb, GROUP_W), F32)
    else:
        u_in = proj.reshape(nb, seq, proj.shape[1])
        u_spec = pl.BlockSpec((nb, tc, GROUP_W), lambda c: (0, c, COL_S5))
        y_spec = pl.BlockSpec((nb, tc, GROUP_W), lambda c: (0, c, 0))
        y_shape = jax.ShapeDtypeStruct((nb, seq, GROUP_W), F32)
    y, hre, him = pl.pallas_call(
        functools.partial(_s5_kernel, nb=nb, tc=tc),
        grid=(nchunk,),
        in_specs=[u_spec, full((nb, S5_STATE_W)), full((nb, S5_STATE_W)), full((1, S5_STATE_W)),
                  full((1, S5_STATE_W)), full((4, LANES, 2 * GROUP_W)), full((4, 2 * GROUP_W, LANES)),
                  full((1, GROUP_W)), full((GROUP_W, GROUP_W)), full((1, GROUP_W))],
        out_specs=[y_spec, full((nb, S5_STATE_W)), full((nb, S5_STATE_W))],
        out_shape=[y_shape, jax.ShapeDtypeStruct((nb, S5_STATE_W), F32),
                   jax.ShapeDtypeStruct((nb, S5_STATE_W), F32)],
        scratch_shapes=[pltpu.VMEM((S5_STATE_W // LANES, nb * tc, LANES), F32)] * 2,
        compiler_params=_cparams(("arbitrary",), 56),
        name="s5",
    )(u_in, h0re, h0im, lre, lim, bw, cw, d.reshape(1, GROUP_W), glu_w, glu_b.reshape(1, GROUP_W))
    return y.reshape(nb * seq, GROUP_W), hre, him


def _softplus(x):
    return jnp.maximum(x, 0.0) + jnp.log1p(jnp.exp(-jnp.abs(x)))


def _lru_gates(conv, wa_ref, wx_ref, ba_ref, bx_ref, lam_ref):
    cb = conv.astype(BF16)
    r = jax.nn.sigmoid(_dot(cb, wa_ref[...]) + ba_ref[...])
    i = jax.nn.sigmoid(_dot(cb, wx_ref[...]) + bx_ref[...])
    log_a = -LRU_C * r * _softplus(-lam_ref[...])
    a = jnp.exp(log_a)
    return a, jnp.sqrt(-jnp.tanh(log_a) * (a * a + 1.0)) * (i * conv)


def _lru_kernel(x_ref, g_ref, h0_ref, buf0_ref, cw_ref, cb_ref, wa_ref, ba_ref, wx_ref, bx_ref, lam_ref,
                y_ref, h_ref, buf_ref, xe_ref, sa_ref, sx_ref, *, nb, tc):
    nq = GROUP_W // LANES

    @pl.when(pl.program_id(0) == 0)
    def _():
        h_ref[...] = h0_ref[...]
        for b in range(nb):
            xe_ref[b, 5:8, :] = buf0_ref[b]

    for b in range(nb):
        xe_ref[b, 8:8 + tc, :] = x_ref[b]
    for b in range(nb):
        conv = cb_ref[...] + sum(xe_ref[b, 5 + j:5 + j + tc, :] * cw_ref[j:j + 1, :] for j in range(CONV_W))
        a, xin = _lru_gates(conv, wa_ref, wx_ref, ba_ref, bx_ref, lam_ref)
        for q in range(nq):
            sa_ref[q, b * tc:(b + 1) * tc, :] = a[:, q * LANES:(q + 1) * LANES]
            sx_ref[q, b * tc:(b + 1) * tc, :] = xin[:, q * LANES:(q + 1) * LANES]

    def body(t, hs):
        idx = pl.ds(t, nb, stride=tc)
        new = []
        for q in range(nq):
            h = sa_ref[q, idx, :] * hs[q] + sx_ref[q, idx, :]
            sx_ref[q, idx, :] = h
            new.append(h)
        return tuple(new)

    fin = lax.fori_loop(0, tc, body, tuple(h_ref[:, q * LANES:(q + 1) * LANES] for q in range(nq)))
    for q in range(nq):
        h_ref[:, q * LANES:(q + 1) * LANES] = fin[q]
    for b in range(nb):
        hseq = jnp.concatenate([sx_ref[q, b * tc:(b + 1) * tc, :] for q in range(nq)], axis=1)
        y_ref[b] = hseq * _gelu(g_ref[b])
        tail = xe_ref[b, tc + 5:tc + 8, :]
        xe_ref[b, 5:8, :] = tail
        buf_ref[b] = tail


def _lru_step_kernel(x_ref, g_ref, h0_ref, buf0_ref, cw_ref, cb_ref, wa_ref, ba_ref, wx_ref, bx_ref, lam_ref,
                     y_ref, h_ref, buf_ref):
    x = x_ref[...]
    conv = cb_ref[...] + x * cw_ref[3:4, :] + sum(buf0_ref[j] * cw_ref[j:j + 1, :] for j in range(CONV_W - 1))
    a, xin = _lru_gates(conv, wa_ref, wx_ref, ba_ref, bx_ref, lam_ref)
    h = a * h0_ref[...] + xin
    h_ref[...] = h
    y_ref[...] = h * _gelu(g_ref[...])
    buf_ref[0] = buf0_ref[1]
    buf_ref[1] = buf0_ref[2]
    buf_ref[2] = x


def _block_diag(w):
    return jnp.einsum("hij,hg->higj", w, jnp.eye(8, dtype=w.dtype)).reshape(GROUP_W, GROUP_W)


def _lru(proj, h0, buf0, conv_w, conv_b, wa, ba, wx, bx, lam, *, nb, seq):
    single = seq == 1
    row = lambda v: v.reshape(1, GROUP_W)
    wts = (conv_w, row(conv_b), _block_diag(wa).astype(BF16), row(ba), _block_diag(wx).astype(BF16), row(bx), row(lam))
    full = lambda shape: pl.BlockSpec(shape, lambda c: tuple(0 for _ in shape))
    wspecs = [full((CONV_W, GROUP_W)), full((1, GROUP_W)), full((GROUP_W, GROUP_W)), full((1, GROUP_W)),
              full((GROUP_W, GROUP_W)), full((1, GROUP_W)), full((1, GROUP_W))]
    if single:
        y, h, buf = pl.pallas_call(
            _lru_step_kernel,
            grid=(1,),
            in_specs=[pl.BlockSpec((nb, GROUP_W), lambda c: (0, COL_LX)), pl.BlockSpec((nb, GROUP_W), lambda c: (0, COL_LG)),
                      full((nb, GROUP_W)), full((CONV_W - 1, nb, GROUP_W))] + wspecs,
            out_specs=[full((nb, GROUP_W)), full((nb, GROUP_W)), full((CONV_W - 1, nb, GROUP_W))],
            out_shape=[jax.ShapeDtypeStruct((nb, GROUP_W), F32), jax.ShapeDtypeStruct((nb, GROUP_W), F32),
                       jax.ShapeDtypeStruct((CONV_W - 1, nb, GROUP_W), F32)],
            compiler_params=_cparams(("arbitrary",), 32),
            name="lru_step",
        )(proj, proj, h0, jnp.transpose(buf0, (1, 0, 2)), *wts)
        return y, h, jnp.transpose(buf, (1, 0, 2))
    tc = _row_tile(seq, 344)
    p3 = proj.reshape(nb, seq, proj.shape[1])
    y, h, buf = pl.pallas_call(
        functools.partial(_lru_kernel, nb=nb, tc=tc),
        grid=(seq // tc,),
        in_specs=[pl.BlockSpec((nb, tc, GROUP_W), lambda c: (0, c, COL_LX)),
                  pl.BlockSpec((nb, tc, GROUP_W), lambda c: (0, c, COL_LG)),
                  full((nb, GROUP_W)), full((nb, CONV_W - 1, GROUP_W))] + wspecs,
        out_specs=[pl.BlockSpec((nb, tc, GROUP_W), lambda c: (0, c, 0)), full((nb, GROUP_W)),
                   full((nb, CONV_W - 1, GROUP_W))],
        out_shape=[jax.ShapeDtypeStruct((nb, seq, GROUP_W), F32), jax.ShapeDtypeStruct((nb, GROUP_W), F32),
                   jax.ShapeDtypeStruct((nb, CONV_W - 1, GROUP_W), F32)],
        scratch_shapes=[pltpu.VMEM((nb, tc + 8, GROUP_W), F32), pltpu.VMEM((GROUP_W // LANES, nb * tc, LANES), F32),
                        pltpu.VMEM((GROUP_W // LANES, nb * tc, LANES), F32)],
        compiler_params=_cparams(("arbitrary",), 48),
        name="lru",
    )(p3, p3, h0, buf0, *wts)
    return y.reshape(nb * seq, GROUP_W), h, buf


def _ret_log_gamma(h):
    return math.log(1.0 - 2.0 ** (-5.0 - h))


def _rotary_tables(pos):
    half = HEAD_D // 2
    inv = 1.0 / (10000.0 ** (jnp.arange(half, dtype=F32) / half))
    ang = pos.astype(F32)[:, None] * inv[None, :]
    cos, sin = jnp.cos(ang), jnp.sin(ang)
    return jnp.concatenate([cos, cos], axis=1), jnp.concatenate([-sin, sin], axis=1)


def _rotate(x, cos2, sin2):
    return x * cos2 + pltpu.roll(x, HEAD_D // 2, 1) * sin2


def _ret_kernel(q_ref, k_ref, v_ref, g_ref, cos_ref, sin_ref, ng_ref, y_ref, s_ref, *, seq):
    c = pl.program_id(1)

    @pl.when(c == 0)
    def _():
        s_ref[...] = jnp.zeros_like(s_ref)

    nv = jnp.minimum(LANES, seq - c * LANES)
    nvf = nv.astype(F32)
    row = lax.broadcasted_iota(jnp.int32, (LANES, 1), 0)
    rowf = row.astype(F32)
    valid = row < nv
    diff = (lax.broadcasted_iota(jnp.int32, (LANES, LANES), 0)
            - lax.broadcasted_iota(jnp.int32, (LANES, LANES), 1)).astype(F32)
    cos2, sin2 = cos_ref[...], sin_ref[...]
    for h in range(N_HEADS):
        lg = _ret_log_gamma(h)
        cols = slice(h * HEAD_D, (h + 1) * HEAD_D)
        decay = jnp.exp(jnp.where(diff >= 0, diff * lg, NEG_INF))
        q_dec = jnp.exp((rowf + 1.0) * lg)
        k_dec = jnp.exp((nvf - 1.0 - rowf) * lg)
        c_dec = jnp.exp(jnp.full((1, 1), lg, F32) * nvf)
        q = q_ref[:, cols]
        k = jnp.where(valid, k_ref[:, cols], 0.0)
        v = jnp.where(valid, v_ref[:, cols], 0.0)
        qr = _rotate(q, cos2, sin2)
        kr = _rotate(k, cos2, sin2) * (HEAD_D ** -0.5)
        vb = v.astype(BF16)
        att = _dot_nt(qr.astype(BF16), kr.astype(BF16)) * decay
        s_old = s_ref[h]
        o = _dot(att.astype(BF16), vb) + _dot((qr * q_dec).astype(BF16), s_old.astype(BF16))
        s_ref[h] = s_old * c_dec + _dot_tn((kr * k_dec).astype(BF16), vb)
        o = o * lax.rsqrt(jnp.mean(o * o, axis=-1, keepdims=True) + EPS) * ng_ref[:, cols]
        g = g_ref[:, cols]
        y_ref[:, cols] = o * (g * jax.nn.sigmoid(g))


def _ret_prompt(proj, norm_g, *, nb, seq):
    p3 = proj.reshape(nb, seq, proj.shape[1])
    nchunk = pl.cdiv(seq, LANES)
    cos2, sin2 = _rotary_tables(jnp.arange(nchunk * LANES))
    blk = lambda col: pl.BlockSpec((None, LANES, GROUP_W), lambda b, c: (b, c, col))
    tab = pl.BlockSpec((LANES, HEAD_D), lambda b, c: (c, 0))
    y, s = pl.pallas_call(
        functools.partial(_ret_kernel, seq=seq),
        grid=(nb, nchunk),
        in_specs=[blk(COL_RQ), blk(COL_RK), blk(COL_RV), blk(COL_RG), tab, tab,
                  pl.BlockSpec((1, GROUP_W), lambda b, c: (0, 0))],
        out_specs=[pl.BlockSpec((None, LANES, GROUP_W), lambda b, c: (b, c, 0)),
                   pl.BlockSpec((None, N_HEADS, HEAD_D, HEAD_D), lambda b, c: (b, 0, 0, 0))],
        out_shape=[jax.ShapeDtypeStruct((nb, seq, GROUP_W), F32),
                   jax.ShapeDtypeStruct((nb, N_HEADS, HEAD_D, HEAD_D), F32)],
        compiler_params=_cparams(("arbitrary", "arbitrary"), 32),
        name="retention",
    )(p3, p3, p3, p3, cos2, sin2, norm_g.reshape(1, GROUP_W))
    return y.reshape(nb * seq, GROUP_W), s


def _ret_step_kernel(q_ref, k_ref, v_ref, g_ref, cos_ref, sin_ref, ng_ref, s0_ref, y_ref, s_ref):
    cos2, sin2 = cos_ref[...], sin_ref[...]
    eye = (lax.broadcasted_iota(jnp.int32, (HEAD_D, HEAD_D), 0)
           == lax.broadcasted_iota(jnp.int32, (HEAD_D, HEAD_D), 1))
    for h in range(N_HEADS):
        gamma = 1.0 - 2.0 ** (-5.0 - h)
        cols = slice(h * HEAD_D, (h + 1) * HEAD_D)
        qr = _rotate(q_ref[:, cols], cos2, sin2)
        kr = _rotate(k_ref[:, cols], cos2, sin2) * (HEAD_D ** -0.5)
        v = v_ref[:, cols]
        s0 = s0_ref[h]
        qs = _dot_f32(jnp.broadcast_to(qr, (SUBLANES, HEAD_D)), s0)[0:1]
        o = jnp.sum(qr * kr, axis=-1, keepdims=True) * v + gamma * qs
        kcol = jnp.sum(jnp.where(eye, jnp.broadcast_to(kr, (HEAD_D, HEAD_D)), 0.0), axis=1, keepdims=True)
        s_ref[h] = gamma * s0 + kcol * v
        o = o * lax.rsqrt(jnp.mean(o * o, axis=-1, keepdims=True) + EPS) * ng_ref[:, cols]
        g = g_ref[:, cols]
        y_ref[:, cols] = o * (g * jax.nn.sigmoid(g))


def _ret_step(proj, norm_g, s0, *, nb):
    p3 = proj.reshape(nb, 1, proj.shape[1])
    cos2, sin2 = _rotary_tables(jnp.full((1,), PAST_LEN))
    blk = lambda col: pl.BlockSpec((None, 1, GROUP_W), lambda b: (b, 0, col))
    one = lambda n: pl.BlockSpec((1, n), lambda b: (0, 0))
    st = pl.BlockSpec((None, N_HEADS, HEAD_D, HEAD_D), lambda b: (b, 0, 0, 0))
    y, s = pl.pallas_call(
        _ret_step_kernel,
        grid=(nb,),
        in_specs=[blk(COL_RQ), blk(COL_RK), blk(COL_RV), blk(COL_RG), one(HEAD_D), one(HEAD_D), one(GROUP_W), st],
        out_specs=[pl.BlockSpec((None, 1, GROUP_W), lambda b: (b, 0, 0)), st],
        out_shape=[jax.ShapeDtypeStruct((nb, 1, GROUP_W), F32),
                   jax.ShapeDtypeStruct((nb, N_HEADS, HEAD_D, HEAD_D), F32)],
        compiler_params=_cparams(("arbitrary",), 32),
        name="retention_step",
    )(p3, p3, p3, p3, cos2, sin2, norm_g.reshape(1, GROUP_W), s0)
    return y.reshape(nb, GROUP_W), s


def _log_sigmoid(x):
    return jnp.minimum(x, 0.0) - jnp.log1p(jnp.exp(-jnp.abs(x)))


def _fox_prep_kernel(ff_ref, bf_ref, logf_ref, c_ref):
    logf = _log_sigmoid(ff_ref[...] + bf_ref[...])
    logf_ref[...] = logf
    tri = (lax.broadcasted_iota(jnp.int32, (LANES, LANES), 0)
           <= lax.broadcasted_iota(jnp.int32, (LANES, LANES), 1)).astype(F32)
    carry = jnp.zeros((SUBLANES, 1), F32)
    for j in range(logf.shape[1] // LANES):
        blk = logf[:, j * LANES:(j + 1) * LANES]
        c_ref[:, j * LANES:(j + 1) * LANES] = carry + _dot_f32(blk, tri)
        carry = carry + jnp.sum(blk, axis=1, keepdims=True)


def _fox_prep(ff, bf, *, nb, seq):
    nblk = pl.cdiv(seq, LANES)
    lp = nblk * LANES
    fft = jnp.transpose(ff[:, :N_HEADS].reshape(nb, seq, N_HEADS), (0, 2, 1))
    fft = jnp.pad(fft, ((0, 0), (0, SUBLANES - N_HEADS), (0, lp - seq)))
    bfc = jnp.pad(bf, (0, SUBLANES - N_HEADS)).reshape(SUBLANES, 1)
    spec = pl.BlockSpec((None, SUBLANES, lp), lambda b: (b, 0, 0))
    logf_t, c_t = pl.pallas_call(
        _fox_prep_kernel,
        grid=(nb,),
        in_specs=[spec, pl.BlockSpec((SUBLANES, 1), lambda b: (0, 0))],
        out_specs=[spec, spec],
        out_shape=[jax.ShapeDtypeStruct((nb, SUBLANES, lp), F32)] * 2,
        compiler_params=_cparams(("arbitrary",), 56),
        name="fox_prep",
    )(fft, bfc)
    logf = jnp.transpose(logf_t[:, :N_HEADS, :seq], (0, 2, 1))
    c_blk = jnp.transpose(c_t.reshape(nb, SUBLANES, nblk, LANES), (0, 2, 1, 3))
    return logf, c_blk


def _softmax_update(carry, s, vb):
    m, l, acc = carry
    mn = jnp.maximum(m, jnp.max(s, axis=-1, keepdims=True))
    p = jnp.exp(s - mn)
    al = jnp.exp(m - mn)
    return mn, al * l + jnp.sum(p, axis=-1, keepdims=True), al * acc + _dot(p.astype(BF16), vb)


def _fox_attn_kernel(q_ref, k_ref, v_ref, c_ref, o_ref, *, seq):
    i = pl.program_id(1)
    nfull = seq // LANES
    rem = seq - nfull * LANES
    scale = HEAD_D ** -0.5
    qpos = i * LANES + lax.broadcasted_iota(jnp.int32, (LANES, 1), 0)
    for h in range(N_HEADS):
        cols = slice(h * HEAD_D, (h + 1) * HEAD_D)
        qb = q_ref[:, cols].astype(BF16)

        def body(j, carry, cols=cols, qb=qb, h=h):
            r0 = pl.multiple_of(j * LANES, LANES)
            kb = k_ref[pl.ds(r0, LANES), cols].astype(BF16)
            vb = v_ref[pl.ds(r0, LANES), cols].astype(BF16)
            s = _dot_nt(qb, kb) * scale - c_ref[j][h:h + 1, :]
            kpos = j * LANES + lax.broadcasted_iota(jnp.int32, (1, LANES), 1)
            s = jnp.where(kpos <= qpos, s, NEG_INF)
            return _softmax_update(carry, s, vb)

        init = (jnp.full((LANES, 1), NEG_INF, F32), jnp.zeros((LANES, 1), F32), jnp.zeros((LANES, HEAD_D), F32))
        carry = lax.fori_loop(0, jnp.minimum(i + 1, nfull), body, init)
        if rem:
            kb = k_ref[nfull * LANES:seq, cols].astype(BF16)
            vb = v_ref[nfull * LANES:seq, cols].astype(BF16)
            s = _dot_nt(qb, kb) * scale - c_ref[nfull][h:h + 1, 0:rem]
            kpos = nfull * LANES + lax.broadcasted_iota(jnp.int32, (1, rem), 1)
            s = jnp.where(kpos <= qpos, s, NEG_INF)
            carry = _softmax_update(carry, s, vb)
        _, l, acc = carry
        o_ref[:, cols] = acc / l


def _fox_prompt(proj, c_blk, *, nb, seq):
    p3 = proj.reshape(nb, seq, proj.shape[1])
    nblk = pl.cdiv(seq, LANES)
    kv = lambda col: pl.BlockSpec((None, seq, GROUP_W), lambda b, i: (b, 0, col))
    y = pl.pallas_call(
        functools.partial(_fox_attn_kernel, seq=seq),
        grid=(nb, nblk),
        in_specs=[pl.BlockSpec((None, LANES, GROUP_W), lambda b, i: (b, i, COL_FQ)), kv(COL_FK), kv(COL_FV),
                  pl.BlockSpec((None, nblk, SUBLANES, LANES), lambda b, i: (b, 0, 0, 0))],
        out_specs=pl.BlockSpec((None, LANES, GROUP_W), lambda b, i: (b, i, 0)),
        out_shape=jax.ShapeDtypeStruct((nb, seq, GROUP_W), F32),
        compiler_params=_cparams(("arbitrary", "arbitrary"), 48),
        name="fox_attn",
    )(p3, p3, p3, c_blk)
    return y.reshape(nb * seq, GROUP_W)


def _fox_step_kernel(pt_ref, q_ref, kn_ref, vn_ref, ff_ref, bf_ref, *rest, pps):
    k_refs, v_refs, lf_refs = rest[:pps], rest[pps:2 * pps], rest[2 * pps:3 * pps]
    o_ref, logf_ref, qbd_ref, m_ref, l_ref, acc_ref, carry_ref = rest[3 * pps:]
    j = pl.program_id(1)
    scale = HEAD_D ** -0.5
    head_of_col = lax.broadcasted_iota(jnp.int32, (SUBLANES, GROUP_W), 1) // HEAD_D
    bd = head_of_col == lax.broadcasted_iota(jnp.int32, (SUBLANES, GROUP_W), 0)

    @pl.when(j == 0)
    def _():
        qbd_ref[...] = jnp.where(bd, jnp.broadcast_to(q_ref[...], (SUBLANES, GROUP_W)), 0.0).astype(BF16)
        m_ref[...] = jnp.full_like(m_ref, NEG_INF)
        l_ref[...] = jnp.zeros_like(l_ref)
        acc_ref[...] = jnp.zeros_like(acc_ref)
        carry_ref[...] = jnp.zeros_like(carry_ref)

    later = (lax.broadcasted_iota(jnp.int32, (PAGE_SIZE, PAGE_SIZE), 0)
             > lax.broadcasted_iota(jnp.int32, (PAGE_SIZE, PAGE_SIZE), 1)).astype(F32)
    for r in range(pps):
        lp = lf_refs[r][...]
        bias = carry_ref[...] + _dot_f32(lp, later)
        carry_ref[...] = carry_ref[...] + jnp.sum(lp, axis=1, keepdims=True)
        s = _dot_nt(qbd_ref[...], k_refs[r][...].astype(BF16)) * scale + bias
        m, l, acc = _softmax_update((m_ref[...], l_ref[...], acc_ref[...]), s, v_refs[r][...].astype(BF16))
        m_ref[...], l_ref[...], acc_ref[...] = m, l, acc

    @pl.when(j == pl.num_programs(1) - 1)
    def _():
        logf = _log_sigmoid(ff_ref[...] + bf_ref[...])
        logf_ref[...] = jnp.broadcast_to(logf, (SUBLANES, LANES))
        qf = qbd_ref[...].astype(F32)
        kn = kn_ref[...].astype(BF16).astype(F32)
        s_new = jnp.sum(qf * kn, axis=1, keepdims=True) * scale - logf
        m, l, acc = m_ref[...], l_ref[...], acc_ref[...]
        mn = jnp.maximum(m, s_new)
        p = jnp.exp(s_new - mn)
        al = jnp.exp(m - mn)
        out = (al * acc + p * vn_ref[...]) / (al * l + p)
        o_ref[...] = jnp.sum(jnp.where(bd, out, 0.0), axis=0, keepdims=True)


def _fox_step(proj, ff, bf, cache_k, cache_v, cache_lf, page_table, *, nb, pps=4):
    pool = cache_k.shape[0]
    npages = page_table.shape[1]
    ck = cache_k.reshape(pool, PAGE_SIZE, GROUP_W)
    cv = cache_v.reshape(pool, PAGE_SIZE, GROUP_W)
    clf = jnp.pad(jnp.transpose(cache_lf, (0, 2, 1)), ((0, 0), (0, SUBLANES - N_HEADS), (0, 0)))
    p3 = proj.reshape(nb, 1, proj.shape[1])
    ffc = jnp.pad(ff[:, :N_HEADS], ((0, 0), (0, SUBLANES - N_HEADS))).reshape(nb, SUBLANES, 1)
    bfc = jnp.pad(bf, (0, SUBLANES - N_HEADS)).reshape(SUBLANES, 1)
    row = lambda col: pl.BlockSpec((None, 1, GROUP_W), lambda b, j, pt: (b, 0, col))

    def page(shape, r):
        return pl.BlockSpec((None,) + shape, lambda b, j, pt: (pt[b, npages - 1 - (j * pps + r)], 0, 0))

    in_specs = ([row(COL_FQ), row(COL_FK), row(COL_FV),
                 pl.BlockSpec((None, SUBLANES, 1), lambda b, j, pt: (b, 0, 0)),
                 pl.BlockSpec((SUBLANES, 1), lambda b, j, pt: (0, 0))]
                + [page((PAGE_SIZE, GROUP_W), r) for r in range(pps)]
                + [page((PAGE_SIZE, GROUP_W), r) for r in range(pps)]
                + [page((SUBLANES, PAGE_SIZE), r) for r in range(pps)])
    y, logf = pl.pallas_call(
        functools.partial(_fox_step_kernel, pps=pps),
        grid_spec=pltpu.PrefetchScalarGridSpec(
            num_scalar_prefetch=1,
            grid=(nb, npages // pps),
            in_specs=in_specs,
            out_specs=[pl.BlockSpec((None, 1, GROUP_W), lambda b, j, pt: (b, 0, 0)),
                       pl.BlockSpec((None, SUBLANES, LANES), lambda b, j, pt: (b, 0, 0))],
            scratch_shapes=[pltpu.VMEM((SUBLANES, GROUP_W), BF16), pltpu.VMEM((SUBLANES, 1), F32),
                            pltpu.VMEM((SUBLANES, 1), F32), pltpu.VMEM((SUBLANES, GROUP_W), F32),
                            pltpu.VMEM((SUBLANES, 1), F32)]),
        out_shape=[jax.ShapeDtypeStruct((nb, 1, GROUP_W), F32), jax.ShapeDtypeStruct((nb, SUBLANES, LANES), F32)],
        compiler_params=_cparams(("arbitrary", "arbitrary"), 32),
        name="fox_step",
    )(page_table, p3, p3, p3, ffc, bfc, *([ck] * pps), *([cv] * pps), *([clf] * pps))
    return y.reshape(nb, GROUP_W), logf[:, :N_HEADS, 0]


N_TOP = PEER_TOPK + 1


def _top_values(s):
    rows = lax.broadcasted_iota(jnp.int32, (24, 1), 0)

    def rnd(r, carry):
        work, top = carry
        m = jnp.max(work, axis=0, keepdims=True)
        return jnp.where(work == m, NEG_INF, work), jnp.where(rows == r, m, top)

    _, top = lax.fori_loop(0, N_TOP, rnd, (s, jnp.full((24, s.shape[1]), NEG_INF, F32)))
    return top


def _peer_route_kernel(q_ref, sk_ref, s1_ref, s2_ref, tau_ref, *, tmr):
    s1 = _dot_nt(sk_ref[0], q_ref[:, :PEER_NKEYS].astype(BF16))
    s2 = _dot_nt(sk_ref[1], q_ref[:, PEER_NKEYS:].astype(BF16))
    a, b = _top_values(s1), _top_values(s2)
    r24 = lax.broadcasted_iota(jnp.int32, (24, 1), 0)
    r8 = lax.broadcasted_iota(jnp.int32, (8, 1), 0)
    a8, b8 = a[0:8], b[0:8]
    slabs = [a[0:1] + b, jnp.where(r24 >= 1, a + b[0:1], NEG_INF)]
    for i in range(1, 5):
        slabs.append(jnp.where((r8 >= 1) & (r8 < N_TOP // (i + 1)), a[i:i + 1] + b8, NEG_INF))
    slabs.append(jnp.where(r8 >= 5, a8 + b[1:2], NEG_INF))
    cand = jnp.concatenate(slabs, axis=0)

    def rnd(_, carry):
        work, _, cur = carry
        m = jnp.max(work, axis=0, keepdims=True)
        return jnp.where(work == m, NEG_INF, work), cur, m

    init = jnp.max(cand, axis=0, keepdims=True)
    _, t16, t17 = lax.fori_loop(0, N_TOP, rnd, (cand, init, init))
    m0 = a[0:1] + b[0:1]
    z = jnp.sum(jnp.where(cand >= t16, jnp.exp(cand - m0), 0.0), axis=0, keepdims=True)
    log2z = jnp.log2(z)
    s1n = (s1 - a[0:1]) * LOG2E
    s2n = (s2 - b[0:1]) * LOG2E - log2z
    taun = (0.5 * (t16 + t17) - m0) * LOG2E - log2z
    for q in range(tmr // LANES):
        lanes = slice(q * LANES, (q + 1) * LANES)
        s1_ref[q] = s1n[:, lanes]
        s2_ref[q] = s2n[:, lanes]
        tau_ref[q] = jnp.broadcast_to(taun[:, lanes], (SUBLANES, LANES))


def _peer_route(qp, subkeys):
    t = qp.shape[0]
    tmr = 2 * LANES if t > LANES else LANES
    ntile = pl.cdiv(t, tmr)
    k = tmr // LANES
    sblk = pl.BlockSpec((None, k, PEER_NKEYS, LANES), lambda i, h: (h, i, 0, 0))
    return pl.pallas_call(
        functools.partial(_peer_route_kernel, tmr=tmr),
        grid=(ntile, PEER_HEADS),
        in_specs=[pl.BlockSpec((tmr, 2 * PEER_NKEYS), lambda i, h: (i, h)),
                  pl.BlockSpec((None, 2, PEER_NKEYS, PEER_NKEYS), lambda i, h: (h, 0, 0, 0))],
        out_specs=[sblk, sblk, pl.BlockSpec((None, k, SUBLANES, LANES), lambda i, h: (h, i, 0, 0))],
        out_shape=[jax.ShapeDtypeStruct((PEER_HEADS, ntile * k, PEER_NKEYS, LANES), F32)] * 2
        + [jax.ShapeDtypeStruct((PEER_HEADS, ntile * k, SUBLANES, LANES), F32)],
        compiler_params=_cparams(("arbitrary", "arbitrary"), 32),
        name="peer_route",
    )(qp, subkeys)


def _peer_dense_kernel(x_ref, h_ref, u_ref, v_ref, s1_ref, s2_ref, tau_ref, o_ref, xu_ref, ht_ref, *, tm, et):
    j = pl.program_id(1)

    @pl.when(j == 0)
    def _():
        o_ref[...] = h_ref[...]

    xu_ref[...] = _dot_nt(u_ref[...], x_ref[...])
    na = et // PEER_NKEYS
    for ap in range(na):
        a_glob = j * na + ap
        rows = slice(ap * PEER_NKEYS, (ap + 1) * PEER_NKEYS)
        for lt in range(tm // LANES):
            lanes = slice(lt * LANES, (lt + 1) * LANES)
            gate = jnp.zeros((PEER_NKEYS, LANES), F32)
            for h in range(PEER_HEADS):
                v2 = s2_ref[h, lt] + s1_ref[h, lt, pl.ds(a_glob, 1), :]
                gate = gate + jnp.exp2(jnp.where(v2 >= tau_ref[h, lt, 0:1, :], v2, NEG_INF))
            ht_ref[rows, lanes] = (_gelu(xu_ref[rows, lanes]) * gate).astype(BF16)
    o_ref[...] += _dot_tn(ht_ref[...], v_ref[...])


def _peer_dense(xn, h, u_tab, v_tab, s1n, s2n, tau):
    t, d = h.shape
    ne = u_tab.shape[0]
    tm = 5 * LANES if t > 5 * LANES else LANES
    et = 4 * PEER_NKEYS
    k = tm // LANES
    rblk = lambda rows: pl.BlockSpec((PEER_HEADS, k, rows, LANES), lambda i, j: (0, i, 0, 0))
    return pl.pallas_call(
        functools.partial(_peer_dense_kernel, tm=tm, et=et),
        grid=(pl.cdiv(t, tm), ne // et),
        in_specs=[pl.BlockSpec((tm, d), lambda i, j: (i, 0)), pl.BlockSpec((tm, d), lambda i, j: (i, 0)),
                  pl.BlockSpec((et, d), lambda i, j: (j, 0)), pl.BlockSpec((et, d), lambda i, j: (j, 0)),
                  rblk(PEER_NKEYS), rblk(PEER_NKEYS), rblk(SUBLANES)],
        out_specs=pl.BlockSpec((tm, d), lambda i, j: (i, 0)),
        out_shape=jax.ShapeDtypeStruct((t, d), F32),
        scratch_shapes=[pltpu.VMEM((et, tm), F32), pltpu.VMEM((et, tm), BF16)],
        compiler_params=_cparams(("arbitrary", "arbitrary"), 56),
        name="peer_dense",
    )(xn, h, u_tab, v_tab, s1n, s2n, tau)


def _peer(h, norm_g, wq, subkeys, u_tab, v_tab):
    t = h.shape[0]
    if t < LANES:
        h = jnp.pad(h, ((0, LANES - t), (0, 0)))
    qp, xn = _q_proj(h, norm_g, wq)
    s1n, s2n, tau = _peer_route(qp, subkeys)
    out = _peer_dense(xn, h, u_tab, v_tab, s1n, s2n, tau)
    return out[:t]


def _prep_w_in(w):
    d = w.shape[0]
    g0 = 4 * GROUP_W
    main = jnp.concatenate([w[:, :g0], w[:, g0 + N_HEADS:]], axis=1).astype(BF16)
    ff = jnp.pad(w[:, g0:g0 + N_HEADS], ((0, 0), (0, LANES - N_HEADS))).astype(BF16)
    return main, ff


def kernel(x_prompt, x_sample, cache_fox_k, cache_fox_v, cache_fox_logf, page_table, state_s5_re, state_s5_im, state_ret, state_lru, state_conv, meta_tokens, norm_mix_g, w_in, w_out, s5_lam_re, s5_lam_im, s5_log_dt, s5_b_re, s5_b_im, s5_c_re, s5_c_im, s5_d, s5_glu_w, s5_glu_b, fox_bf, ret_norm_g, lru_conv_w, lru_conv_b, lru_wa, lru_ba, lru_wx, lru_bx, lru_lam, norm_ffn_g, peer_wq, peer_subkeys, peer_u, peer_v, norm_final_g):
    bp, seq_x, d = x_prompt.shape
    bs = x_sample.shape[0]
    depth = w_in.shape[0]
    seq = seq_x + N_META
    meta = jnp.broadcast_to(meta_tokens[None], (bp, N_META, d))
    h_p = jnp.concatenate([meta, x_prompt], axis=1).reshape(bp * seq, d)
    h_s = x_sample.reshape(bs, d)
    zeros = lambda *shape: jnp.zeros(shape, F32)
    outs_p, outs_s = [], []
    for l in range(depth):
        w_main, w_ff = _prep_w_in(w_in[l])
        w_o = w_out[l].astype(BF16).reshape(4, GROUP_W, d)
        s5w = _s5_weights(s5_lam_re[l], s5_lam_im[l], s5_log_dt[l], s5_b_re[l], s5_b_im[l], s5_c_re[l], s5_c_im[l])
        glu_w = s5_glu_w[l].astype(BF16)
        lru_w = (lru_conv_w[l], lru_conv_b[l], lru_wa[l], lru_ba[l], lru_wx[l], lru_bx[l], lru_lam[l])
        wq = peer_wq[l].astype(BF16)
        subkeys = peer_subkeys[l].astype(BF16)
        u_tab = peer_u[l].astype(BF16)
        v_tab = peer_v[l].astype(BF16)

        proj, ff = _in_proj(h_p, norm_mix_g[l], w_main, w_ff)
        y_s5, s5re, s5im = _s5(proj, zeros(bp, S5_STATE_W), zeros(bp, S5_STATE_W), s5w, s5_d[l], glu_w, s5_glu_b[l],
                               nb=bp, seq=seq)
        logf, c_blk = _fox_prep(ff, fox_bf[l], nb=bp, seq=seq)
        y_fox = _fox_prompt(proj, c_blk, nb=bp, seq=seq)
        y_ret, ret_s = _ret_prompt(proj, ret_norm_g[l], nb=bp, seq=seq)
        y_lru, lru_h, conv_buf = _lru(proj, zeros(bp, GROUP_W), zeros(bp, CONV_W - 1, GROUP_W), *lru_w, nb=bp, seq=seq)
        h_p = _out_proj(h_p, (y_s5, y_fox, y_ret, y_lru), w_o)
        h_p = _peer(h_p, norm_ffn_g[l], wq, subkeys, u_tab, v_tab)
        p3 = proj.reshape(bp, seq, -1)
        outs_p.append((p3[:, :, COL_FK * GROUP_W:(COL_FK + 1) * GROUP_W].reshape(bp, seq, N_HEADS, HEAD_D),
                       p3[:, :, COL_FV * GROUP_W:(COL_FV + 1) * GROUP_W].reshape(bp, seq, N_HEADS, HEAD_D),
                       logf, s5re.reshape(bp, -1, 64), s5im.reshape(bp, -1, 64), ret_s, lru_h, conv_buf))

        proj, ff = _in_proj(h_s, norm_mix_g[l], w_main, w_ff)
        y_s5, s5re, s5im = _s5(proj, state_s5_re[l].reshape(bs, S5_STATE_W), state_s5_im[l].reshape(bs, S5_STATE_W),
                               s5w, s5_d[l], glu_w, s5_glu_b[l], nb=bs, seq=1)
        y_fox, logf = _fox_step(proj, ff, fox_bf[l], cache_fox_k[l], cache_fox_v[l], cache_fox_logf[l], page_table, nb=bs)
        y_ret, ret_s = _ret_step(proj, ret_norm_g[l], state_ret[l], nb=bs)
        y_lru, lru_h, conv_buf = _lru(proj, state_lru[l], state_conv[l], *lru_w, nb=bs, seq=1)
        h_s = _out_proj(h_s, (y_s5, y_fox, y_ret, y_lru), w_o)
        h_s = _peer(h_s, norm_ffn_g[l], wq, subkeys, u_tab, v_tab)
        outs_s.append((proj[:, COL_FK * GROUP_W:(COL_FK + 1) * GROUP_W].reshape(bs, 1, N_HEADS, HEAD_D),
                       proj[:, COL_FV * GROUP_W:(COL_FV + 1) * GROUP_W].reshape(bs, 1, N_HEADS, HEAD_D),
                       logf.reshape(bs, 1, N_HEADS), s5re.reshape(bs, -1, 64), s5im.reshape(bs, -1, 64),
                       ret_s, lru_h, conv_buf))

    y_prompt = _rmsnorm(h_p, norm_final_g).reshape(bp, seq, d)[:, N_META:]
    y_sample = _rmsnorm(h_s, norm_final_g).reshape(bs, 1, d)
    stk = lambda outs, j: jnp.stack([o[j] for o in outs], axis=0)
    return ((y_prompt, y_sample) + tuple(stk(outs_p, j) for j in range(8)) + tuple(stk(outs_s, j) for j in range(8)))
```

```python
import functools
import math

import jax
import jax.numpy as jnp
from jax import lax
from jax.experimental import pallas as pl
from jax.experimental.pallas import tpu as pltpu

F32 = jnp.float32
BF16 = jnp.bfloat16
NEG_INF = float("-inf")

EPS = 1e-6
N_META = 16
PAST_LEN = 16384
PAGE_SIZE = 128
GROUP_W = 512
HEAD_D = 128
N_HEADS = 4
S5_STATE_W = 2048
LRU_C = 8.0
CONV_W = 4
PEER_HEADS = 8
PEER_NKEYS = 128
PEER_TOPK = 16
LANES = 128
SUBLANES = 8
MXU_W = 256
LOG2E = 1.4426950408889634
MIB = 1024 * 1024

COL_S5, COL_FQ, COL_FK, COL_FV, COL_RQ, COL_RK, COL_RV, COL_RG, COL_LX, COL_LG = range(10)


def _cparams(sem, vmem_mib, **kw):
    return pltpu.CompilerParams(dimension_semantics=sem, vmem_limit_bytes=vmem_mib * MIB, **kw)


def _dot(a, b):
    return jnp.dot(a, b, preferred_element_type=F32)


def _dot_nt(a, b):
    return lax.dot_general(a, b, (((1,), (1,)), ((), ())), preferred_element_type=F32)


def _dot_tn(a, b):
    return lax.dot_general(a, b, (((0,), (0,)), ((), ())), preferred_element_type=F32)


def _dot_f32(a, b):
    return jnp.dot(a, b, preferred_element_type=F32, precision=lax.Precision.HIGHEST)


def _gelu(x):
    return 0.5 * x * (1.0 + jnp.tanh(0.7978845608028654 * (x + 0.044715 * (x * x * x))))


def _row_tile(n, target):
    if n <= target:
        return n
    best = None
    for t in range(SUBLANES, target + 1, SUBLANES):
        if n % t == 0:
            best = t
    assert best is not None, n
    return best


def _in_proj_kernel(x_ref, g_ref, w_ref, wff_ref, o_ref, off_ref, xn_ref):
    @pl.when(pl.program_id(1) == 0)
    def _():
        x = x_ref[...]
        ms = jnp.mean(x * x, axis=-1, keepdims=True)
        xn = (x * lax.rsqrt(ms + EPS) * g_ref[...]).astype(BF16)
        xn_ref[...] = xn
        off_ref[...] = _dot(xn, wff_ref[...])
    o_ref[...] = _dot(xn_ref[...], w_ref[...])


def _in_proj(x, g, w_main, w_ff):
    t, d = x.shape
    n = w_main.shape[1]
    tm, tn = _row_tile(t, 1032), 512
    return pl.pallas_call(
        _in_proj_kernel,
        grid=(t // tm, n // tn),
        in_specs=[pl.BlockSpec((tm, d), lambda i, j: (i, 0)),
                  pl.BlockSpec((1, d), lambda i, j: (0, 0)),
                  pl.BlockSpec((d, tn), lambda i, j: (0, j)),
                  pl.BlockSpec((d, LANES), lambda i, j: (0, 0))],
        out_specs=[pl.BlockSpec((tm, tn), lambda i, j: (i, j)),
                   pl.BlockSpec((tm, LANES), lambda i, j: (i, 0))],
        out_shape=[jax.ShapeDtypeStruct((t, n), F32), jax.ShapeDtypeStruct((t, LANES), F32)],
        scratch_shapes=[pltpu.VMEM((tm, d), BF16)],
        compiler_params=_cparams(("arbitrary", "arbitrary"), 48),
        name="in_proj",
    )(x, g.reshape(1, d), w_main, w_ff)


def _q_proj_kernel(x_ref, g_ref, w_ref, o_ref, xn_ref):
    @pl.when(pl.program_id(1) == 0)
    def _():
        x = x_ref[...]
        ms = jnp.mean(x * x, axis=-1, keepdims=True)
        xn_ref[...] = (x * lax.rsqrt(ms + EPS) * g_ref[...]).astype(BF16)
    o_ref[...] = _dot(xn_ref[...], w_ref[...])


def _q_proj(x, g, w):
    t, d = x.shape
    n = w.shape[1]
    tm, tn = _row_tile(t, 1032), 512
    return pl.pallas_call(
        _q_proj_kernel,
        grid=(t // tm, n // tn),
        in_specs=[pl.BlockSpec((tm, d), lambda i, j: (i, 0)),
                  pl.BlockSpec((1, d), lambda i, j: (0, 0)),
                  pl.BlockSpec((d, tn), lambda i, j: (0, j))],
        out_specs=[pl.BlockSpec((tm, tn), lambda i, j: (i, j)),
                   pl.BlockSpec((tm, d), lambda i, j: (i, 0))],
        out_shape=[jax.ShapeDtypeStruct((t, n), F32), jax.ShapeDtypeStruct((t, d), BF16)],
        compiler_params=_cparams(("arbitrary", "arbitrary"), 48),
        name="q_proj",
    )(x, g.reshape(1, d), w)


def _out_proj_kernel(h_ref, y0_ref, y1_ref, y2_ref, y3_ref, w_ref, o_ref):
    acc = h_ref[...]
    for gi, y_ref in enumerate((y0_ref, y1_ref, y2_ref, y3_ref)):
        acc = acc + _dot(y_ref[...].astype(BF16), w_ref[gi])
    o_ref[...] = acc


def _out_proj(h, ys, w):
    t, d = h.shape
    tm, tn = _row_tile(t, 1032), 512
    yspec = pl.BlockSpec((tm, GROUP_W), lambda i, j: (i, 0))
    return pl.pallas_call(
        _out_proj_kernel,
        grid=(t // tm, d // tn),
        in_specs=[pl.BlockSpec((tm, tn), lambda i, j: (i, j)), yspec, yspec, yspec, yspec,
                  pl.BlockSpec((4, GROUP_W, tn), lambda i, j: (0, 0, j))],
        out_specs=pl.BlockSpec((tm, tn), lambda i, j: (i, j)),
        out_shape=jax.ShapeDtypeStruct((t, d), F32),
        compiler_params=_cparams(("arbitrary", "arbitrary"), 48),
        name="out_proj",
    )(h, *ys, w)


def _rmsnorm_kernel(x_ref, g_ref, o_ref):
    x = x_ref[...]
    ms = jnp.mean(x * x, axis=-1, keepdims=True)
    o_ref[...] = x * lax.rsqrt(ms + EPS) * g_ref[...]


def _rmsnorm(x, g):
    t, d = x.shape
    tm = _row_tile(t, 1032)
    return pl.pallas_call(
        _rmsnorm_kernel,
        grid=(t // tm,),
        in_specs=[pl.BlockSpec((tm, d), lambda i: (i, 0)), pl.BlockSpec((1, d), lambda i: (0, 0))],
        out_specs=pl.BlockSpec((tm, d), lambda i: (i, 0)),
        out_shape=jax.ShapeDtypeStruct((t, d), F32),
        compiler_params=_cparams(("arbitrary",), 48),
        name="final_norm",
    )(x, g.reshape(1, d))


def _s5_kernel(u_ref, h0re_ref, h0im_ref, lre_ref, lim_ref, bw_ref, cw_ref, d_ref, gw_ref, gb_ref,
               y_ref, hre_ref, him_ref, sre_ref, sim_ref, *, nb, tc):
    single = tc == 1

    @pl.when(pl.program_id(0) == 0)
    def _():
        hre_ref[...] = h0re_ref[...]
        him_ref[...] = h0im_ref[...]

    def get_u(b):
        return u_ref[...] if single else u_ref[b]

    seqs = (0,) if single else tuple(range(nb))
    nrow = nb if single else tc
    for b in seqs:
        ub = get_u(b).astype(BF16)
        for r in range(4):
            bu = _dot(ub[:, r * LANES:(r + 1) * LANES], bw_ref[r])
            for q in range(4):
                sre_ref[4 * r + q, b * nrow:(b + 1) * nrow, :] = bu[:, q * LANES:(q + 1) * LANES]
                sim_ref[4 * r + q, b * nrow:(b + 1) * nrow, :] = bu[:, GROUP_W + q * LANES:GROUP_W + (q + 1) * LANES]

    for r in range(4):
        tiles = tuple(range(4 * r, 4 * r + 4))
        lanes = [slice(lt * LANES, (lt + 1) * LANES) for lt in tiles]
        lr = [jnp.broadcast_to(lre_ref[:, ln], (nb, LANES)) for ln in lanes]
        li = [jnp.broadcast_to(lim_ref[:, ln], (nb, LANES)) for ln in lanes]

        def body(t, carry, tiles=tiles, lr=lr, li=li):
            idx = pl.ds(t, nb, stride=tc)
            new = []
            for q, lt in enumerate(tiles):
                hr, hi = carry[q]
                nr = lr[q] * hr - li[q] * hi + sre_ref[lt, idx, :]
                ni = lr[q] * hi + li[q] * hr + sim_ref[lt, idx, :]
                sre_ref[lt, idx, :] = nr
                sim_ref[lt, idx, :] = ni
                new.append((nr, ni))
            return tuple(new)

        fin = lax.fori_loop(0, tc, body, tuple((hre_ref[:, ln], him_ref[:, ln]) for ln in lanes))
        for q, ln in enumerate(lanes):
            hre_ref[:, ln] = fin[q][0]
            him_ref[:, ln] = fin[q][1]

    for b in seqs:
        rows = slice(b * nrow, (b + 1) * nrow)
        ys = []
        for r in range(4):
            hre = jnp.concatenate([sre_ref[4 * r + q, rows, :] for q in range(4)], axis=1).astype(BF16)
            him = jnp.concatenate([sim_ref[4 * r + q, rows, :] for q in range(4)], axis=1).astype(BF16)
            ys.append(_dot(hre, cw_ref[r, :GROUP_W, :]) + _dot(him, cw_ref[r, GROUP_W:, :]))
        y = jnp.concatenate(ys, axis=1) + d_ref[...] * get_u(b)
        y = _gelu(y)
        z = _dot(y.astype(BF16), gw_ref[...]) + gb_ref[...]
        out = y * jax.nn.sigmoid(z)
        if single:
            y_ref[...] = out
        else:
            y_ref[b] = out


def _s5_weights(lam_re, lam_im, log_dt, b_re, b_im, c_re, c_im):
    lr, li = jnp.minimum(lam_re, -1e-4), lam_im
    dt = jnp.exp(log_dt)[:, None]
    mag = jnp.exp(lr * dt)
    bar_re, bar_im = mag * jnp.cos(li * dt), mag * jnp.sin(li * dt)
    den = lr * lr + li * li
    f_re = ((bar_re - 1.0) * lr + bar_im * li) / den
    f_im = (bar_im * lr - (bar_re - 1.0) * li) / den
    bb_re = f_re[:, :, None] * b_re - f_im[:, :, None] * b_im
    bb_im = f_re[:, :, None] * b_im + f_im[:, :, None] * b_re
    eye = jnp.eye(8, dtype=F32)

    def pack_b(m):
        m = m.reshape(4, 8, 64, 16)
        return jnp.einsum("rgpc,gh->rgchp", m, eye).reshape(4, LANES, GROUP_W)

    def pack_c(m):
        m = m.reshape(4, 8, 16, 64)
        return jnp.einsum("rgcp,hg->rhpgc", m, eye).reshape(4, GROUP_W, LANES)

    bw = jnp.concatenate([pack_b(bb_re), pack_b(bb_im)], axis=2).astype(BF16)
    cw = jnp.concatenate([pack_c(c_re), -pack_c(c_im)], axis=1).astype(BF16)
    return (bar_re.reshape(1, S5_STATE_W), bar_im.reshape(1, S5_STATE_W), bw, cw)


def _s5(proj, h0re, h0im, wts, d, glu_w, glu_b, *, nb, seq):
    lre, lim, bw, cw = wts
    single = seq == 1
    tc = 1 if single else _row_tile(seq, 344)
    nchunk = seq // tc
    full = lambda shape: pl.BlockSpec(shape, lambda c: tuple(0 for _ in shape))
    if single:
        u_in = proj
        u_spec = pl.BlockSpec((nb, GROUP_W), lambda c: (0, COL_S5))
        y_spec = pl.BlockSpec((nb, GROUP_W), lambda c: (0, 0))
        y_shape = jax.ShapeDtypeStruct((nb, GROUP_W), F32)
    else:
        u_in = proj.reshape(nb, seq, proj.shape[1])
        u_spec = pl.BlockSpec((nb, tc, GROUP_W), lambda c: (0, c, COL_S5))
        y_spec = pl.BlockSpec((nb, tc, GROUP_W), lambda c: (0, c, 0))
        y_shape = jax.ShapeDtypeStruct((nb, seq, GROUP_W), F32)
    y, hre, him = pl.pallas_call(
        functools.partial(_s5_kernel, nb=nb, tc=tc),
        grid=(nchunk,),
        in_specs=[u_spec, full((nb, S5_STATE_W)), full((nb, S5_STATE_W)), full((1, S5_STATE_W)),
                  full((1, S5_STATE_W)), full((4, LANES, 2 * GROUP_W)), full((4, 2 * GROUP_W, LANES)),
                  full((1, GROUP_W)), full((GROUP_W, GROUP_W)), full((1, GROUP_W))],
        out_specs=[y_spec, full((nb, S5_STATE_W)), full((nb, S5_STATE_W))],
        out_shape=[y_shape, jax.ShapeDtypeStruct((nb, S5_STATE_W), F32),
                   jax.ShapeDtypeStruct((nb, S5_STATE_W), F32)],
        scratch_shapes=[pltpu.VMEM((S5_STATE_W // LANES, nb * tc, LANES), F32)] * 2,
        compiler_params=_cparams(("arbitrary",), 56),
        name="s5",
    )(u_in, h0re, h0im, lre, lim, bw, cw, d.reshape(1, GROUP_W), glu_w, glu_b.reshape(1, GROUP_W))
    return y.reshape(nb * seq, GROUP_W), hre, him


def _softplus(x):
    return jnp.maximum(x, 0.0) + jnp.log1p(jnp.exp(-jnp.abs(x)))


def _lru_gates(conv, wa_ref, wx_ref, ba_ref, bx_ref, lam_ref):
    cb = conv.astype(BF16)
    r = jax.nn.sigmoid(_dot(cb, wa_ref[...]) + ba_ref[...])
    i = jax.nn.sigmoid(_dot(cb, wx_ref[...]) + bx_ref[...])
    log_a = -LRU_C * r * _softplus(-lam_ref[...])
    a = jnp.exp(log_a)
    return a, jnp.sqrt(-jnp.tanh(log_a) * (a * a + 1.0)) * (i * conv)


def _lru_kernel(x_ref, g_ref, h0_ref, buf0_ref, cw_ref, cb_ref, wa_ref, ba_ref, wx_ref, bx_ref, lam_ref,
                y_ref, h_ref, buf_ref, xe_ref, sa_ref, sx_ref, *, nb, tc):
    nq = GROUP_W // LANES

    @pl.when(pl.program_id(0) == 0)
    def _():
        h_ref[...] = h0_ref[...]
        for b in range(nb):
            xe_ref[b, 5:8, :] = buf0_ref[b]

    for b in range(nb):
        xe_ref[b, 8:8 + tc, :] = x_ref[b]
    for b in range(nb):
        conv = cb_ref[...] + sum(xe_ref[b, 5 + j:5 + j + tc, :] * cw_ref[j:j + 1, :] for j in range(CONV_W))
        a, xin = _lru_gates(conv, wa_ref, wx_ref, ba_ref, bx_ref, lam_ref)
        for q in range(nq):
            sa_ref[q, b * tc:(b + 1) * tc, :] = a[:, q * LANES:(q + 1) * LANES]
            sx_ref[q, b * tc:(b + 1) * tc, :] = xin[:, q * LANES:(q + 1) * LANES]

    def body(t, hs):
        idx = pl.ds(t, nb, stride=tc)
        new = []
        for q in range(nq):
            h = sa_ref[q, idx, :] * hs[q] + sx_ref[q, idx, :]
            sx_ref[q, idx, :] = h
            new.append(h)
        return tuple(new)

    fin = lax.fori_loop(0, tc, body, tuple(h_ref[:, q * LANES:(q + 1) * LANES] for q in range(nq)))
    for q in range(nq):
        h_ref[:, q * LANES:(q + 1) * LANES] = fin[q]
    for b in range(nb):
        hseq = jnp.concatenate([sx_ref[q, b * tc:(b + 1) * tc, :] for q in range(nq)], axis=1)
        y_ref[b] = hseq * _gelu(g_ref[b])
        tail = xe_ref[b, tc + 5:tc + 8, :]
        xe_ref[b, 5:8, :] = tail
        buf_ref[b] = tail


def _lru_step_kernel(x_ref, g_ref, h0_ref, buf0_ref, cw_ref, cb_ref, wa_ref, ba_ref, wx_ref, bx_ref, lam_ref,
                     y_ref, h_ref, buf_ref):
    x = x_ref[...]
    conv = cb_ref[...] + x * cw_ref[3:4, :] + sum(buf0_ref[j] * cw_ref[j:j + 1, :] for j in range(CONV_W - 1))
    a, xin = _lru_gates(conv, wa_ref, wx_ref, ba_ref, bx_ref, lam_ref)
    h = a * h0_ref[...] + xin
    h_ref[...] = h
    y_ref[...] = h * _gelu(g_ref[...])
    buf_ref[0] = buf0_ref[1]
    buf_ref[1] = buf0_ref[2]
    buf_ref[2] = x


def _block_diag(w):
    return jnp.einsum("hij,hg->higj", w, jnp.eye(8, dtype=w.dtype)).reshape(GROUP_W, GROUP_W)


def _lru(proj, h0, buf0, conv_w, conv_b, wa, ba, wx, bx, lam, *, nb, seq):
    single = seq == 1
    row = lambda v: v.reshape(1, GROUP_W)
    wts = (conv_w, row(conv_b), _block_diag(wa).astype(BF16), row(ba), _block_diag(wx).astype(BF16), row(bx), row(lam))
    full = lambda shape: pl.BlockSpec(shape, lambda c: tuple(0 for _ in shape))
    wspecs = [full((CONV_W, GROUP_W)), full((1, GROUP_W)), full((GROUP_W, GROUP_W)), full((1, GROUP_W)),
              full((GROUP_W, GROUP_W)), full((1, GROUP_W)), full((1, GROUP_W))]
    if single:
        y, h, buf = pl.pallas_call(
            _lru_step_kernel,
            grid=(1,),
            in_specs=[pl.BlockSpec((nb, GROUP_W), lambda c: (0, COL_LX)), pl.BlockSpec((nb, GROUP_W), lambda c: (0, COL_LG)),
                      full((nb, GROUP_W)), full((CONV_W - 1, nb, GROUP_W))] + wspecs,
            out_specs=[full((nb, GROUP_W)), full((nb, GROUP_W)), full((CONV_W - 1, nb, GROUP_W))],
            out_shape=[jax.ShapeDtypeStruct((nb, GROUP_W), F32), jax.ShapeDtypeStruct((nb, GROUP_W), F32),
                       jax.ShapeDtypeStruct((CONV_W - 1, nb, GROUP_W), F32)],
            compiler_params=_cparams(("arbitrary",), 32),
            name="lru_step",
        )(proj, proj, h0, jnp.transpose(buf0, (1, 0, 2)), *wts)
        return y, h, jnp.transpose(buf, (1, 0, 2))
    tc = _row_tile(seq, 344)
    p3 = proj.reshape(nb, seq, proj.shape[1])
    y, h, buf = pl.pallas_call(
        functools.partial(_lru_kernel, nb=nb, tc=tc),
        grid=(seq // tc,),
        in_specs=[pl.BlockSpec((nb, tc, GROUP_W), lambda c: (0, c, COL_LX)),
                  pl.BlockSpec((nb, tc, GROUP_W), lambda c: (0, c, COL_LG)),
                  full((nb, GROUP_W)), full((nb, CONV_W - 1, GROUP_W))] + wspecs,
        out_specs=[pl.BlockSpec((nb, tc, GROUP_W), lambda c: (0, c, 0)), full((nb, GROUP_W)),
                   full((nb, CONV_W - 1, GROUP_W))],
        out_shape=[jax.ShapeDtypeStruct((nb, seq, GROUP_W), F32), jax.ShapeDtypeStruct((nb, GROUP_W), F32),
                   jax.ShapeDtypeStruct((nb, CONV_W - 1, GROUP_W), F32)],
        scratch_shapes=[pltpu.VMEM((nb, tc + 8, GROUP_W), F32), pltpu.VMEM((GROUP_W // LANES, nb * tc, LANES), F32),
                        pltpu.VMEM((GROUP_W // LANES, nb * tc, LANES), F32)],
        compiler_params=_cparams(("arbitrary",), 48),
        name="lru",
    )(p3, p3, h0, buf0, *wts)
    return y.reshape(nb * seq, GROUP_W), h, buf


def _ret_log_gamma(h):
    return math.log(1.0 - 2.0 ** (-5.0 - h))


def _rotary_tables(pos):
    half = HEAD_D // 2
    inv = 1.0 / (10000.0 ** (jnp.arange(half, dtype=F32) / half))
    ang = pos.astype(F32)[:, None] * inv[None, :]
    cos, sin = jnp.cos(ang), jnp.sin(ang)
    return jnp.concatenate([cos, cos], axis=1), jnp.concatenate([-sin, sin], axis=1)


def _rotate(x, cos2, sin2):
    return x * cos2 + pltpu.roll(x, HEAD_D // 2, 1) * sin2


def _ret_kernel(q_ref, k_ref, v_ref, g_ref, cos_ref, sin_ref, ng_ref, y_ref, s_ref, *, seq):
    c = pl.program_id(1)

    @pl.when(c == 0)
    def _():
        s_ref[...] = jnp.zeros_like(s_ref)

    nv = jnp.minimum(LANES, seq - c * LANES)
    nvf = nv.astype(F32)
    row = lax.broadcasted_iota(jnp.int32, (LANES, 1), 0)
    rowf = row.astype(F32)
    valid = row < nv
    diff = (lax.broadcasted_iota(jnp.int32, (LANES, LANES), 0)
            - lax.broadcasted_iota(jnp.int32, (LANES, LANES), 1)).astype(F32)
    cos2, sin2 = cos_ref[...], sin_ref[...]
    for h in range(N_HEADS):
        lg = _ret_log_gamma(h)
        cols = slice(h * HEAD_D, (h + 1) * HEAD_D)
        decay = jnp.exp(jnp.where(diff >= 0, diff * lg, NEG_INF))
        q_dec = jnp.exp((rowf + 1.0) * lg)
        k_dec = jnp.exp((nvf - 1.0 - rowf) * lg)
        c_dec = jnp.exp(jnp.full((1, 1), lg, F32) * nvf)
        q = q_ref[:, cols]
        k = jnp.where(valid, k_ref[:, cols], 0.0)
        v = jnp.where(valid, v_ref[:, cols], 0.0)
        qr = _rotate(q, cos2, sin2)
        kr = _rotate(k, cos2, sin2) * (HEAD_D ** -0.5)
        vb = v.astype(BF16)
        att = _dot_nt(qr.astype(BF16), kr.astype(BF16)) * decay
        s_old = s_ref[h]
        o = _dot(att.astype(BF16), vb) + _dot((qr * q_dec).astype(BF16), s_old.astype(BF16))
        s_ref[h] = s_old * c_dec + _dot_tn((kr * k_dec).astype(BF16), vb)
        o = o * lax.rsqrt(jnp.mean(o * o, axis=-1, keepdims=True) + EPS) * ng_ref[:, cols]
        g = g_ref[:, cols]
        y_ref[:, cols] = o * (g * jax.nn.sigmoid(g))


def _ret_prompt(proj, norm_g, *, nb, seq):
    p3 = proj.reshape(nb, seq, proj.shape[1])
    nchunk = pl.cdiv(seq, LANES)
    cos2, sin2 = _rotary_tables(jnp.arange(nchunk * LANES))
    blk = lambda col: pl.BlockSpec((None, LANES, GROUP_W), lambda b, c: (b, c, col))
    tab = pl.BlockSpec((LANES, HEAD_D), lambda b, c: (c, 0))
    y, s = pl.pallas_call(
        functools.partial(_ret_kernel, seq=seq),
        grid=(nb, nchunk),
        in_specs=[blk(COL_RQ), blk(COL_RK), blk(COL_RV), blk(COL_RG), tab, tab,
                  pl.BlockSpec((1, GROUP_W), lambda b, c: (0, 0))],
        out_specs=[pl.BlockSpec((None, LANES, GROUP_W), lambda b, c: (b, c, 0)),
                   pl.BlockSpec((None, N_HEADS, HEAD_D, HEAD_D), lambda b, c: (b, 0, 0, 0))],
        out_shape=[jax.ShapeDtypeStruct((nb, seq, GROUP_W), F32),
                   jax.ShapeDtypeStruct((nb, N_HEADS, HEAD_D, HEAD_D), F32)],
        compiler_params=_cparams(("arbitrary", "arbitrary"), 32),
        name="retention",
    )(p3, p3, p3, p3, cos2, sin2, norm_g.reshape(1, GROUP_W))
    return y.reshape(nb * seq, GROUP_W), s


def _ret_step_kernel(q_ref, k_ref, v_ref, g_ref, cos_ref, sin_ref, ng_ref, s0_ref, y_ref, s_ref):
    cos2, sin2 = cos_ref[...], sin_ref[...]
    eye = (lax.broadcasted_iota(jnp.int32, (HEAD_D, HEAD_D), 0)
           == lax.broadcasted_iota(jnp.int32, (HEAD_D, HEAD_D), 1))
    for h in range(N_HEADS):
        gamma = 1.0 - 2.0 ** (-5.0 - h)
        cols = slice(h * HEAD_D, (h + 1) * HEAD_D)
        qr = _rotate(q_ref[:, cols], cos2, sin2)
        kr = _rotate(k_ref[:, cols], cos2, sin2) * (HEAD_D ** -0.5)
        v = v_ref[:, cols]
        s0 = s0_ref[h]
        qs = _dot_f32(jnp.broadcast_to(qr, (SUBLANES, HEAD_D)), s0)[0:1]
        o = jnp.sum(qr * kr, axis=-1, keepdims=True) * v + gamma * qs
        kcol = jnp.sum(jnp.where(eye, jnp.broadcast_to(kr, (HEAD_D, HEAD_D)), 0.0), axis=1, keepdims=True)
        s_ref[h] = gamma * s0 + kcol * v
        o = o * lax.rsqrt(jnp.mean(o * o, axis=-1, keepdims=True) + EPS) * ng_ref[:, cols]
        g = g_ref[:, cols]
        y_ref[:, cols] = o * (g * jax.nn.sigmoid(g))


def _ret_step(proj, norm_g, s0, *, nb):
    p3 = proj.reshape(nb, 1, proj.shape[1])
    cos2, sin2 = _rotary_tables(jnp.full((1,), PAST_LEN))
    blk = lambda col: pl.BlockSpec((None, 1, GROUP_W), lambda b: (b, 0, col))
    one = lambda n: pl.BlockSpec((1, n), lambda b: (0, 0))
    st = pl.BlockSpec((None, N_HEADS, HEAD_D, HEAD_D), lambda b: (b, 0, 0, 0))
    y, s = pl.pallas_call(
        _ret_step_kernel,
        grid=(nb,),
        in_specs=[blk(COL_RQ), blk(COL_RK), blk(COL_RV), blk(COL_RG), one(HEAD_D), one(HEAD_D), one(GROUP_W), st],
        out_specs=[pl.BlockSpec((None, 1, GROUP_W), lambda b: (b, 0, 0)), st],
        out_shape=[jax.ShapeDtypeStruct((nb, 1, GROUP_W), F32),
                   jax.ShapeDtypeStruct((nb, N_HEADS, HEAD_D, HEAD_D), F32)],
        compiler_params=_cparams(("arbitrary",), 32),
        name="retention_step",
    )(p3, p3, p3, p3, cos2, sin2, norm_g.reshape(1, GROUP_W), s0)
    return y.reshape(nb, GROUP_W), s


def _log_sigmoid(x):
    return jnp.minimum(x, 0.0) - jnp.log1p(jnp.exp(-jnp.abs(x)))


def _fox_prep_kernel(ff_ref, bf_ref, logf_ref, c_ref):
    logf = _log_sigmoid(ff_ref[...] + bf_ref[...])
    logf_ref[...] = logf
    tri = (lax.broadcasted_iota(jnp.int32, (LANES, LANES), 0)
           <= lax.broadcasted_iota(jnp.int32, (LANES, LANES), 1)).astype(F32)
    carry = jnp.zeros((SUBLANES, 1), F32)
    for j in range(logf.shape[1] // LANES):
        blk = logf[:, j * LANES:(j + 1) * LANES]
        c_ref[:, j * LANES:(j + 1) * LANES] = carry + _dot_f32(blk, tri)
        carry = carry + jnp.sum(blk, axis=1, keepdims=True)


FOX_TK = 512


def _fox_prep(ff, bf, *, nb, seq):
    nblk = pl.cdiv(seq, FOX_TK)
    lp = nblk * FOX_TK
    fft = jnp.transpose(ff[:, :N_HEADS].reshape(nb, seq, N_HEADS), (0, 2, 1))
    fft = jnp.pad(fft, ((0, 0), (0, SUBLANES - N_HEADS), (0, lp - seq)))
    bfc = jnp.pad(bf, (0, SUBLANES - N_HEADS)).reshape(SUBLANES, 1)
    spec = pl.BlockSpec((None, SUBLANES, lp), lambda b: (b, 0, 0))
    logf_t, c_t = pl.pallas_call(
        _fox_prep_kernel,
        grid=(nb,),
        in_specs=[spec, pl.BlockSpec((SUBLANES, 1), lambda b: (0, 0))],
        out_specs=[spec, spec],
        out_shape=[jax.ShapeDtypeStruct((nb, SUBLANES, lp), F32)] * 2,
        compiler_params=_cparams(("arbitrary",), 56),
        name="fox_prep",
    )(fft, bfc)
    logf = jnp.transpose(logf_t[:, :N_HEADS, :seq], (0, 2, 1))
    c_blk = jnp.transpose(c_t.reshape(nb, SUBLANES, nblk, FOX_TK), (0, 2, 1, 3))
    return logf, c_blk


def _softmax_update(carry, s, vb):
    m, l, acc = carry
    mn = jnp.maximum(m, jnp.max(s, axis=-1, keepdims=True))
    p = jnp.exp(s - mn)
    al = jnp.exp(m - mn)
    return mn, al * l + jnp.sum(p, axis=-1, keepdims=True), al * acc + _dot(p.astype(BF16), vb)


def _fox_attn_kernel(q_ref, k_ref, v_ref, c_ref, o_ref, kb_ref, vb_ref, m_ref, l_ref, acc_ref, *, seq):
    i = pl.program_id(1)
    tk = FOX_TK
    nfull = seq // tk
    rem = seq - nfull * tk
    scale = HEAD_D ** -0.5

    @pl.when(i == 0)
    def _():
        kb_ref[...] = k_ref[...].astype(BF16)
        vb_ref[...] = v_ref[...].astype(BF16)

    m_ref[...] = jnp.full_like(m_ref, NEG_INF)
    l_ref[...] = jnp.zeros_like(l_ref)
    acc_ref[...] = jnp.zeros_like(acc_ref)
    qpos = i * LANES + lax.broadcasted_iota(jnp.int32, (LANES, 1), 0)
    head_cols = [slice(h * HEAD_D, (h + 1) * HEAD_D) for h in range(N_HEADS)]
    qbs = [q_ref[:, cols].astype(BF16) for cols in head_cols]

    def tile(rows, cj, kpos):
        mask = kpos <= qpos
        for h, cols in enumerate(head_cols):
            s = _dot_nt(qbs[h], kb_ref[rows, cols]) * scale - cj[h:h + 1, :]
            s = jnp.where(mask, s, NEG_INF)
            m_ref[h], l_ref[h], acc_ref[h] = _softmax_update((m_ref[h], l_ref[h], acc_ref[h]), s, vb_ref[rows, cols])

    def body(j, carry):
        r0 = pl.multiple_of(j * tk, tk)
        tile(pl.ds(r0, tk), c_ref[j], j * tk + lax.broadcasted_iota(jnp.int32, (1, tk), 1))
        return carry

    lax.fori_loop(0, jnp.minimum((i * LANES + LANES + tk - 1) // tk, nfull), body, 0)
    if rem:
        tile(slice(nfull * tk, seq), c_ref[nfull][:, 0:rem],
             nfull * tk + lax.broadcasted_iota(jnp.int32, (1, rem), 1))
    for h, cols in enumerate(head_cols):
        o_ref[:, cols] = acc_ref[h] / l_ref[h]


def _fox_prompt(proj, c_blk, *, nb, seq):
    p3 = proj.reshape(nb, seq, proj.shape[1])
    nkt = c_blk.shape[1]
    kv = lambda col: pl.BlockSpec((None, seq, GROUP_W), lambda b, i: (b, 0, col))
    y = pl.pallas_call(
        functools.partial(_fox_attn_kernel, seq=seq),
        grid=(nb, pl.cdiv(seq, LANES)),
        in_specs=[pl.BlockSpec((None, LANES, GROUP_W), lambda b, i: (b, i, COL_FQ)), kv(COL_FK), kv(COL_FV),
                  pl.BlockSpec((None, nkt, SUBLANES, FOX_TK), lambda b, i: (b, 0, 0, 0))],
        out_specs=pl.BlockSpec((None, LANES, GROUP_W), lambda b, i: (b, i, 0)),
        out_shape=jax.ShapeDtypeStruct((nb, seq, GROUP_W), F32),
        scratch_shapes=[pltpu.VMEM((seq, GROUP_W), BF16), pltpu.VMEM((seq, GROUP_W), BF16),
                        pltpu.VMEM((N_HEADS, LANES, 1), F32), pltpu.VMEM((N_HEADS, LANES, 1), F32),
                        pltpu.VMEM((N_HEADS, LANES, HEAD_D), F32)],
        compiler_params=_cparams(("arbitrary", "arbitrary"), 48),
        name="fox_attn",
    )(p3, p3, p3, c_blk)
    return y.reshape(nb * seq, GROUP_W)


def _fox_step_kernel(pt_ref, q_ref, kn_ref, vn_ref, ff_ref, bf_ref, *rest, pps):
    k_refs, v_refs, lf_refs = rest[:pps], rest[pps:2 * pps], rest[2 * pps:3 * pps]
    o_ref, logf_ref, qbd_ref, m_ref, l_ref, acc_ref, carry_ref = rest[3 * pps:]
    j = pl.program_id(1)
    scale = HEAD_D ** -0.5
    head_of_col = lax.broadcasted_iota(jnp.int32, (SUBLANES, GROUP_W), 1) // HEAD_D
    bd = head_of_col == lax.broadcasted_iota(jnp.int32, (SUBLANES, GROUP_W), 0)

    @pl.when(j == 0)
    def _():
        qbd_ref[...] = jnp.where(bd, jnp.broadcast_to(q_ref[...], (SUBLANES, GROUP_W)), 0.0).astype(BF16)
        m_ref[...] = jnp.full_like(m_ref, NEG_INF)
        l_ref[...] = jnp.zeros_like(l_ref)
        acc_ref[...] = jnp.zeros_like(acc_ref)
        carry_ref[...] = jnp.zeros_like(carry_ref)

    later = (lax.broadcasted_iota(jnp.int32, (PAGE_SIZE, PAGE_SIZE), 0)
             > lax.broadcasted_iota(jnp.int32, (PAGE_SIZE, PAGE_SIZE), 1)).astype(F32)
    qbd = qbd_ref[...]
    run = carry_ref[...]
    scores = []
    for r in range(pps):
        lp = lf_refs[r][...]
        bias = run + _dot_f32(lp, later)
        run = run + jnp.sum(lp, axis=1, keepdims=True)
        scores.append(_dot_nt(qbd, k_refs[r][...].astype(BF16)) * scale + bias)
    carry_ref[...] = run
    m = m_ref[...]
    mn = m
    for s in scores:
        mn = jnp.maximum(mn, jnp.max(s, axis=-1, keepdims=True))
    al = jnp.exp(m - mn)
    l = al * l_ref[...]
    acc = al * acc_ref[...]
    for r, s in enumerate(scores):
        p = jnp.exp(s - mn)
        l = l + jnp.sum(p, axis=-1, keepdims=True)
        acc = acc + _dot(p.astype(BF16), v_refs[r][...].astype(BF16))
    m_ref[...], l_ref[...], acc_ref[...] = mn, l, acc

    @pl.when(j == pl.num_programs(1) - 1)
    def _():
        logf = _log_sigmoid(ff_ref[...] + bf_ref[...])
        logf_ref[...] = jnp.broadcast_to(logf, (SUBLANES, LANES))
        qf = qbd_ref[...].astype(F32)
        kn = kn_ref[...].astype(BF16).astype(F32)
        s_new = jnp.sum(qf * kn, axis=1, keepdims=True) * scale - logf
        m, l, acc = m_ref[...], l_ref[...], acc_ref[...]
        mn = jnp.maximum(m, s_new)
        p = jnp.exp(s_new - mn)
        al = jnp.exp(m - mn)
        out = (al * acc + p * vn_ref[...]) / (al * l + p)
        o_ref[...] = jnp.sum(jnp.where(bd, out, 0.0), axis=0, keepdims=True)


def _fox_caches(cache_k, cache_v, cache_lf):
    depth, pool = cache_k.shape[:2]
    ck = cache_k.reshape(depth, pool, PAGE_SIZE, GROUP_W)
    cv = cache_v.reshape(depth, pool, PAGE_SIZE, GROUP_W)
    clf = jnp.pad(jnp.transpose(cache_lf, (0, 1, 3, 2)), ((0, 0), (0, 0), (0, SUBLANES - N_HEADS), (0, 0)))
    return ck, cv, clf


def _fox_step(proj, ff, bf, caches, layer, page_table, *, nb, pps=8):
    ck, cv, clf = caches
    npages = page_table.shape[1]
    p3 = proj.reshape(nb, 1, proj.shape[1])
    ffc = jnp.pad(ff[:, :N_HEADS], ((0, 0), (0, SUBLANES - N_HEADS))).reshape(nb, SUBLANES, 1)
    bfc = jnp.pad(bf, (0, SUBLANES - N_HEADS)).reshape(SUBLANES, 1)
    row = lambda col: pl.BlockSpec((None, 1, GROUP_W), lambda b, j, pt: (b, 0, col))

    def page(shape, r):
        return pl.BlockSpec((None, None) + shape,
                            lambda b, j, pt: (layer, pt[b, npages - 1 - (j * pps + r)], 0, 0))

    in_specs = ([row(COL_FQ), row(COL_FK), row(COL_FV),
                 pl.BlockSpec((None, SUBLANES, 1), lambda b, j, pt: (b, 0, 0)),
                 pl.BlockSpec((SUBLANES, 1), lambda b, j, pt: (0, 0))]
                + [page((PAGE_SIZE, GROUP_W), r) for r in range(pps)]
                + [page((PAGE_SIZE, GROUP_W), r) for r in range(pps)]
                + [page((SUBLANES, PAGE_SIZE), r) for r in range(pps)])
    y, logf = pl.pallas_call(
        functools.partial(_fox_step_kernel, pps=pps),
        grid_spec=pltpu.PrefetchScalarGridSpec(
            num_scalar_prefetch=1,
            grid=(nb, npages // pps),
            in_specs=in_specs,
            out_specs=[pl.BlockSpec((None, 1, GROUP_W), lambda b, j, pt: (b, 0, 0)),
                       pl.BlockSpec((None, SUBLANES, LANES), lambda b, j, pt: (b, 0, 0))],
            scratch_shapes=[pltpu.VMEM((SUBLANES, GROUP_W), BF16), pltpu.VMEM((SUBLANES, 1), F32),
                            pltpu.VMEM((SUBLANES, 1), F32), pltpu.VMEM((SUBLANES, GROUP_W), F32),
                            pltpu.VMEM((SUBLANES, 1), F32)]),
        out_shape=[jax.ShapeDtypeStruct((nb, 1, GROUP_W), F32), jax.ShapeDtypeStruct((nb, SUBLANES, LANES), F32)],
        compiler_params=_cparams(("arbitrary", "arbitrary"), 32),
        name="fox_step",
    )(page_table, p3, p3, p3, ffc, bfc, *([ck] * pps), *([cv] * pps), *([clf] * pps))
    return y.reshape(nb, GROUP_W), logf[:, :N_HEADS, 0]


N_TOP = PEER_TOPK + 1


def _top_values(s1, s2):
    rows = lax.broadcasted_iota(jnp.int32, (24, 1), 0)

    def rnd(r, carry):
        out = []
        for work, top in carry:
            m = jnp.max(work, axis=0, keepdims=True)
            out.append((jnp.where(work == m, NEG_INF, work), jnp.where(rows == r, m, top)))
        return tuple(out)

    init = jnp.full((24, s1.shape[1]), NEG_INF, F32)
    (_, a), (_, b) = lax.fori_loop(0, N_TOP, rnd, ((s1, init), (s2, init)))
    return a, b


def _peer_route_kernel(q_ref, sk_ref, s1_ref, s2_ref, tau_ref, *, tmr):
    s1 = _dot_nt(sk_ref[0], q_ref[:, :PEER_NKEYS].astype(BF16))
    s2 = _dot_nt(sk_ref[1], q_ref[:, PEER_NKEYS:].astype(BF16))
    a, b = _top_values(s1, s2)
    r24 = lax.broadcasted_iota(jnp.int32, (24, 1), 0)
    r8 = lax.broadcasted_iota(jnp.int32, (8, 1), 0)
    a8, b8 = a[0:8], b[0:8]
    slabs = [a[0:1] + b, jnp.where(r24 >= 1, a + b[0:1], NEG_INF)]
    for i in range(1, 5):
        slabs.append(jnp.where((r8 >= 1) & (r8 < N_TOP // (i + 1)), a[i:i + 1] + b8, NEG_INF))
    slabs.append(jnp.where(r8 >= 5, a8 + b[1:2], NEG_INF))
    cand = jnp.concatenate(slabs, axis=0)

    def rnd(_, carry):
        work, _, cur = carry
        m = jnp.max(work, axis=0, keepdims=True)
        return jnp.where(work == m, NEG_INF, work), cur, m

    init = jnp.max(cand, axis=0, keepdims=True)
    _, t16, t17 = lax.fori_loop(0, N_TOP, rnd, (cand, init, init))
    m0 = a[0:1] + b[0:1]
    z = jnp.sum(jnp.where(cand >= t16, jnp.exp(cand - m0), 0.0), axis=0, keepdims=True)
    log2z = jnp.log2(z)
    s1n = (s1 - a[0:1]) * LOG2E
    s2n = (s2 - b[0:1]) * LOG2E - log2z
    taun = (0.5 * (t16 + t17) - m0) * LOG2E - log2z
    for q in range(tmr // LANES):
        lanes = slice(q * LANES, (q + 1) * LANES)
        s1_ref[q] = s1n[:, lanes]
        s2_ref[q] = s2n[:, lanes]
        tau_ref[q] = jnp.broadcast_to(taun[:, lanes], (SUBLANES, LANES))


def _peer_route(qp, subkeys):
    t = qp.shape[0]
    tmr = 2 * LANES if t > LANES else LANES
    ntile = pl.cdiv(t, tmr)
    k = tmr // LANES
    sblk = pl.BlockSpec((None, k, PEER_NKEYS, LANES), lambda i, h: (h, i, 0, 0))
    return pl.pallas_call(
        functools.partial(_peer_route_kernel, tmr=tmr),
        grid=(ntile, PEER_HEADS),
        in_specs=[pl.BlockSpec((tmr, 2 * PEER_NKEYS), lambda i, h: (i, h)),
                  pl.BlockSpec((None, 2, PEER_NKEYS, PEER_NKEYS), lambda i, h: (h, 0, 0, 0))],
        out_specs=[sblk, sblk, pl.BlockSpec((None, k, SUBLANES, LANES), lambda i, h: (h, i, 0, 0))],
        out_shape=[jax.ShapeDtypeStruct((PEER_HEADS, ntile * k, PEER_NKEYS, LANES), F32)] * 2
        + [jax.ShapeDtypeStruct((PEER_HEADS, ntile * k, SUBLANES, LANES), F32)],
        compiler_params=_cparams(("arbitrary", "arbitrary"), 32),
        name="peer_route",
    )(qp, subkeys)


def _peer_dense_kernel(x_ref, h_ref, u_ref, v_ref, s1_ref, s2_ref, tau_ref, o_ref, xu_ref, ht_ref, *, tm, et):
    j = pl.program_id(1)

    @pl.when(j == 0)
    def _():
        o_ref[...] = h_ref[...]

    xu_ref[...] = _dot_nt(u_ref[...], x_ref[...])
    na = et // PEER_NKEYS
    for ap in range(na):
        a_glob = j * na + ap
        rows = slice(ap * PEER_NKEYS, (ap + 1) * PEER_NKEYS)
        for lt in range(tm // LANES):
            lanes = slice(lt * LANES, (lt + 1) * LANES)
            gate = jnp.zeros((PEER_NKEYS, LANES), F32)
            for h in range(PEER_HEADS):
                v2 = s2_ref[h, lt] + s1_ref[h, lt, pl.ds(a_glob, 1), :]
                gate = gate + jnp.exp2(jnp.where(v2 >= tau_ref[h, lt, 0:1, :], v2, NEG_INF))
            ht_ref[rows, lanes] = (_gelu(xu_ref[rows, lanes]) * gate).astype(BF16)
    o_ref[...] += _dot_tn(ht_ref[...], v_ref[...])


def _peer_dense(xn, h, u_tab, v_tab, layer, s1n, s2n, tau):
    t, d = h.shape
    ne = u_tab.shape[1]
    tm = 5 * LANES if t > 5 * LANES else LANES
    et = 4 * PEER_NKEYS
    k = tm // LANES
    rblk = lambda rows: pl.BlockSpec((PEER_HEADS, k, rows, LANES), lambda i, j: (0, i, 0, 0))
    tab = pl.BlockSpec((None, et, d), lambda i, j: (layer, j, 0))
    return pl.pallas_call(
        functools.partial(_peer_dense_kernel, tm=tm, et=et),
        grid=(pl.cdiv(t, tm), ne // et),
        in_specs=[pl.BlockSpec((tm, d), lambda i, j: (i, 0)), pl.BlockSpec((tm, d), lambda i, j: (i, 0)),
                  tab, tab, rblk(PEER_NKEYS), rblk(PEER_NKEYS), rblk(SUBLANES)],
        out_specs=pl.BlockSpec((tm, d), lambda i, j: (i, 0)),
        out_shape=jax.ShapeDtypeStruct((t, d), F32),
        scratch_shapes=[pltpu.VMEM((et, tm), F32), pltpu.VMEM((et, tm), BF16)],
        compiler_params=_cparams(("arbitrary", "arbitrary"), 56),
        name="peer_dense",
    )(xn, h, u_tab, v_tab, s1n, s2n, tau)


def _peer(h, norm_g, wq, subkeys, u_tab, v_tab, layer):
    t = h.shape[0]
    if t < LANES:
        h = jnp.pad(h, ((0, LANES - t), (0, 0)))
    qp, xn = _q_proj(h, norm_g, wq)
    s1n, s2n, tau = _peer_route(qp, subkeys)
    out = _peer_dense(xn, h, u_tab, v_tab, layer, s1n, s2n, tau)
    return out[:t]


def _prep_w_in(w):
    d = w.shape[0]
    g0 = 4 * GROUP_W
    main = jnp.concatenate([w[:, :g0], w[:, g0 + N_HEADS:]], axis=1).astype(BF16)
    ff = jnp.pad(w[:, g0:g0 + N_HEADS], ((0, 0), (0, LANES - N_HEADS))).astype(BF16)
    return main, ff


def kernel(x_prompt, x_sample, cache_fox_k, cache_fox_v, cache_fox_logf, page_table, state_s5_re, state_s5_im, state_ret, state_lru, state_conv, meta_tokens, norm_mix_g, w_in, w_out, s5_lam_re, s5_lam_im, s5_log_dt, s5_b_re, s5_b_im, s5_c_re, s5_c_im, s5_d, s5_glu_w, s5_glu_b, fox_bf, ret_norm_g, lru_conv_w, lru_conv_b, lru_wa, lru_ba, lru_wx, lru_bx, lru_lam, norm_ffn_g, peer_wq, peer_subkeys, peer_u, peer_v, norm_final_g):
    bp, seq_x, d = x_prompt.shape
    bs = x_sample.shape[0]
    depth = w_in.shape[0]
    seq = seq_x + N_META
    meta = jnp.broadcast_to(meta_tokens[None], (bp, N_META, d))
    h_p = jnp.concatenate([meta, x_prompt], axis=1).reshape(bp * seq, d)
    h_s = x_sample.reshape(bs, d)
    zeros = lambda *shape: jnp.zeros(shape, F32)
    u_tab, v_tab = peer_u.astype(BF16), peer_v.astype(BF16)
    caches = _fox_caches(cache_fox_k, cache_fox_v, cache_fox_logf)
    outs_p, outs_s = [], []
    for l in range(depth):
        w_main, w_ff = _prep_w_in(w_in[l])
        w_o = w_out[l].astype(BF16).reshape(4, GROUP_W, d)
        s5w = _s5_weights(s5_lam_re[l], s5_lam_im[l], s5_log_dt[l], s5_b_re[l], s5_b_im[l], s5_c_re[l], s5_c_im[l])
        glu_w = s5_glu_w[l].astype(BF16)
        lru_w = (lru_conv_w[l], lru_conv_b[l], lru_wa[l], lru_ba[l], lru_wx[l], lru_bx[l], lru_lam[l])
        wq = peer_wq[l].astype(BF16)
        subkeys = peer_subkeys[l].astype(BF16)

        proj, ff = _in_proj(h_p, norm_mix_g[l], w_main, w_ff)
        y_s5, s5re, s5im = _s5(proj, zeros(bp, S5_STATE_W), zeros(bp, S5_STATE_W), s5w, s5_d[l], glu_w, s5_glu_b[l],
                               nb=bp, seq=seq)
        logf, c_blk = _fox_prep(ff, fox_bf[l], nb=bp, seq=seq)
        y_fox = _fox_prompt(proj, c_blk, nb=bp, seq=seq)
        y_ret, ret_s = _ret_prompt(proj, ret_norm_g[l], nb=bp, seq=seq)
        y_lru, lru_h, conv_buf = _lru(proj, zeros(bp, GROUP_W), zeros(bp, CONV_W - 1, GROUP_W), *lru_w, nb=bp, seq=seq)
        h_p = _out_proj(h_p, (y_s5, y_fox, y_ret, y_lru), w_o)
        h_p = _peer(h_p, norm_ffn_g[l], wq, subkeys, u_tab, v_tab, l)
        p3 = proj.reshape(bp, seq, -1)
        outs_p.append((p3[:, :, COL_FK * GROUP_W:(COL_FK + 1) * GROUP_W].reshape(bp, seq, N_HEADS, HEAD_D),
                       p3[:, :, COL_FV * GROUP_W:(COL_FV + 1) * GROUP_W].reshape(bp, seq, N_HEADS, HEAD_D),
                       logf, s5re.reshape(bp, -1, 64), s5im.reshape(bp, -1, 64), ret_s, lru_h, conv_buf))

        proj, ff = _in_proj(h_s, norm_mix_g[l], w_main, w_ff)
        y_s5, s5re, s5im = _s5(proj, state_s5_re[l].reshape(bs, S5_STATE_W), state_s5_im[l].reshape(bs, S5_STATE_W),
                               s5w, s5_d[l], glu_w, s5_glu_b[l], nb=bs, seq=1)
        y_fox, logf = _fox_step(proj, ff, fox_bf[l], caches, l, page_table, nb=bs)
        y_ret, ret_s = _ret_step(proj, ret_norm_g[l], state_ret[l], nb=bs)
        y_lru, lru_h, conv_buf = _lru(proj, state_lru[l], state_conv[l], *lru_w, nb=bs, seq=1)
        h_s = _out_proj(h_s, (y_s5, y_fox, y_ret, y_lru), w_o)
        h_s = _peer(h_s, norm_ffn_g[l], wq, subkeys, u_tab, v_tab, l)
        outs_s.append((proj[:, COL_FK * GROUP_W:(COL_FK + 1) * GROUP_W].reshape(bs, 1, N_HEADS, HEAD_D),
                       proj[:, COL_FV * GROUP_W:(COL_FV + 1) * GROUP_W].reshape(bs, 1, N_HEADS, HEAD_D),
                       logf.reshape(bs, 1, N_HEADS), s5re.reshape(bs, -1, 64), s5im.reshape(bs, -1, 64),
                       ret_s, lru_h, conv_buf))

    y_prompt = _rmsnorm(h_p, norm_final_g).reshape(bp, seq, d)[:, N_META:]
    y_sample = _rmsnorm(h_s, norm_final_g).reshape(bs, 1, d)
    stk = lambda outs, j: jnp.stack([o[j] for o in outs], axis=0)
    return ((y_prompt, y_sample) + tuple(stk(outs_p, j) for j in range(8)) + tuple(stk(outs_s, j) for j in range(8)))
```

```python
import functools
import math

import jax
import jax.numpy as jnp
from jax import lax
from jax.experimental import pallas as pl
from jax.experimental.pallas import tpu as pltpu

F32 = jnp.float32
BF16 = jnp.bfloat16
NEG_INF = float("-inf")

EPS = 1e-6
N_META = 16
PAST_LEN = 16384
PAGE_SIZE = 128
GROUP_W = 512
HEAD_D = 128
N_HEADS = 4
S5_STATE_W = 2048
LRU_C = 8.0
CONV_W = 4
PEER_HEADS = 8
PEER_NKEYS = 128
PEER_TOPK = 16
LANES = 128
SUBLANES = 8
MXU_W = 256
LOG2E = 1.4426950408889634
MIB = 1024 * 1024

COL_S5, COL_FQ, COL_FK, COL_FV, COL_RQ, COL_RK, COL_RV, COL_RG, COL_LX, COL_LG = range(10)


def _cparams(sem, vmem_mib, **kw):
    return pltpu.CompilerParams(dimension_semantics=sem, vmem_limit_bytes=vmem_mib * MIB, **kw)


def _dot(a, b):
    return jnp.dot(a, b, preferred_element_type=F32)


def _dot_nt(a, b):
    return lax.dot_general(a, b, (((1,), (1,)), ((), ())), preferred_element_type=F32)


def _dot_tn(a, b):
    return lax.dot_general(a, b, (((0,), (0,)), ((), ())), preferred_element_type=F32)


def _dot_f32(a, b):
    return jnp.dot(a, b, preferred_element_type=F32, precision=lax.Precision.HIGHEST)


def _gelu(x):
    return 0.5 * x * (1.0 + jnp.tanh(0.7978845608028654 * (x + 0.044715 * (x * x * x))))


def _row_tile(n, target):
    if n <= target:
        return n
    best = None
    for t in range(SUBLANES, target + 1, SUBLANES):
        if n % t == 0:
            best = t
    assert best is not None, n
    return best


def _in_proj_kernel(x_ref, g_ref, w_ref, wff_ref, o_ref, off_ref, xn_ref):
    @pl.when(pl.program_id(1) == 0)
    def _():
        x = x_ref[...]
        ms = jnp.mean(x * x, axis=-1, keepdims=True)
        xn = (x * lax.rsqrt(ms + EPS) * g_ref[...]).astype(BF16)
        xn_ref[...] = xn
        off_ref[...] = _dot(xn, wff_ref[...])
    o_ref[...] = _dot(xn_ref[...], w_ref[...])


def _in_proj(x, g, w_main, w_ff):
    t, d = x.shape
    n = w_main.shape[1]
    tm, tn = _row_tile(t, 1032), 512
    return pl.pallas_call(
        _in_proj_kernel,
        grid=(t // tm, n // tn),
        in_specs=[pl.BlockSpec((tm, d), lambda i, j: (i, 0)),
                  pl.BlockSpec((1, d), lambda i, j: (0, 0)),
                  pl.BlockSpec((d, tn), lambda i, j: (0, j)),
                  pl.BlockSpec((d, LANES), lambda i, j: (0, 0))],
        out_specs=[pl.BlockSpec((tm, tn), lambda i, j: (i, j)),
                   pl.BlockSpec((tm, LANES), lambda i, j: (i, 0))],
        out_shape=[jax.ShapeDtypeStruct((t, n), F32), jax.ShapeDtypeStruct((t, LANES), F32)],
        scratch_shapes=[pltpu.VMEM((tm, d), BF16)],
        compiler_params=_cparams(("arbitrary", "arbitrary"), 48),
        name="in_proj",
    )(x, g.reshape(1, d), w_main, w_ff)


def _q_proj_kernel(x_ref, g_ref, w_ref, o_ref, xn_ref):
    @pl.when(pl.program_id(1) == 0)
    def _():
        x = x_ref[...]
        ms = jnp.mean(x * x, axis=-1, keepdims=True)
        xn_ref[...] = (x * lax.rsqrt(ms + EPS) * g_ref[...]).astype(BF16)
    o_ref[...] = _dot(xn_ref[...], w_ref[...])


def _q_proj(x, g, w):
    t, d = x.shape
    n = w.shape[1]
    tm, tn = _row_tile(t, 1032), 512
    return pl.pallas_call(
        _q_proj_kernel,
        grid=(t // tm, n // tn),
        in_specs=[pl.BlockSpec((tm, d), lambda i, j: (i, 0)),
                  pl.BlockSpec((1, d), lambda i, j: (0, 0)),
                  pl.BlockSpec((d, tn), lambda i, j: (0, j))],
        out_specs=[pl.BlockSpec((tm, tn), lambda i, j: (i, j)),
                   pl.BlockSpec((tm, d), lambda i, j: (i, 0))],
        out_shape=[jax.ShapeDtypeStruct((t, n), F32), jax.ShapeDtypeStruct((t, d), BF16)],
        compiler_params=_cparams(("arbitrary", "arbitrary"), 48),
        name="q_proj",
    )(x, g.reshape(1, d), w)


def _out_proj_kernel(h_ref, y0_ref, y1_ref, y2_ref, y3_ref, w_ref, o_ref):
    acc = h_ref[...]
    for gi, y_ref in enumerate((y0_ref, y1_ref, y2_ref, y3_ref)):
        acc = acc + _dot(y_ref[...].astype(BF16), w_ref[gi])
    o_ref[...] = acc


def _out_proj(h, ys, w):
    t, d = h.shape
    tm, tn = _row_tile(t, 1032), 512
    yspec = pl.BlockSpec((tm, GROUP_W), lambda i, j: (i, 0))
    return pl.pallas_call(
        _out_proj_kernel,
        grid=(t // tm, d // tn),
        in_specs=[pl.BlockSpec((tm, tn), lambda i, j: (i, j)), yspec, yspec, yspec, yspec,
                  pl.BlockSpec((4, GROUP_W, tn), lambda i, j: (0, 0, j))],
        out_specs=pl.BlockSpec((tm, tn), lambda i, j: (i, j)),
        out_shape=jax.ShapeDtypeStruct((t, d), F32),
        compiler_params=_cparams(("arbitrary", "arbitrary"), 48),
        name="out_proj",
    )(h, *ys, w)


def _rmsnorm_kernel(x_ref, g_ref, o_ref):
    x = x_ref[...]
    ms = jnp.mean(x * x, axis=-1, keepdims=True)
    o_ref[...] = x * lax.rsqrt(ms + EPS) * g_ref[...]


def _rmsnorm(x, g):
    t, d = x.shape
    tm = _row_tile(t, 1032)
    return pl.pallas_call(
        _rmsnorm_kernel,
        grid=(t // tm,),
        in_specs=[pl.BlockSpec((tm, d), lambda i: (i, 0)), pl.BlockSpec((1, d), lambda i: (0, 0))],
        out_specs=pl.BlockSpec((tm, d), lambda i: (i, 0)),
        out_shape=jax.ShapeDtypeStruct((t, d), F32),
        compiler_params=_cparams(("arbitrary",), 48),
        name="final_norm",
    )(x, g.reshape(1, d))


def _s5_kernel(u_ref, h0re_ref, h0im_ref, lre_ref, lim_ref, bw_ref, cw_ref, d_ref, gw_ref, gb_ref,
               y_ref, hre_ref, him_ref, sre_ref, sim_ref, *, nb, tc):
    single = tc == 1

    @pl.when(pl.program_id(0) == 0)
    def _():
        hre_ref[...] = h0re_ref[...]
        him_ref[...] = h0im_ref[...]

    def get_u(b):
        return u_ref[...] if single else u_ref[b]

    seqs = (0,) if single else tuple(range(nb))
    nrow = nb if single else tc
    for b in seqs:
        ub = get_u(b).astype(BF16)
        for r in range(4):
            bu = _dot(ub[:, r * LANES:(r + 1) * LANES], bw_ref[r])
            for q in range(4):
                sre_ref[4 * r + q, b * nrow:(b + 1) * nrow, :] = bu[:, q * LANES:(q + 1) * LANES]
                sim_ref[4 * r + q, b * nrow:(b + 1) * nrow, :] = bu[:, GROUP_W + q * LANES:GROUP_W + (q + 1) * LANES]

    for r in range(4):
        tiles = tuple(range(4 * r, 4 * r + 4))
        lanes = [slice(lt * LANES, (lt + 1) * LANES) for lt in tiles]
        lr = [jnp.broadcast_to(lre_ref[:, ln], (nb, LANES)) for ln in lanes]
        li = [jnp.broadcast_to(lim_ref[:, ln], (nb, LANES)) for ln in lanes]

        def body(t, carry, tiles=tiles, lr=lr, li=li):
            idx = pl.ds(t, nb, stride=tc)
            new = []
            for q, lt in enumerate(tiles):
                hr, hi = carry[q]
                nr = lr[q] * hr - li[q] * hi + sre_ref[lt, idx, :]
                ni = lr[q] * hi + li[q] * hr + sim_ref[lt, idx, :]
                sre_ref[lt, idx, :] = nr
                sim_ref[lt, idx, :] = ni
                new.append((nr, ni))
            return tuple(new)

        fin = lax.fori_loop(0, tc, body, tuple((hre_ref[:, ln], him_ref[:, ln]) for ln in lanes))
        for q, ln in enumerate(lanes):
            hre_ref[:, ln] = fin[q][0]
            him_ref[:, ln] = fin[q][1]

    for b in seqs:
        rows = slice(b * nrow, (b + 1) * nrow)
        ys = []
        for r in range(4):
            hre = jnp.concatenate([sre_ref[4 * r + q, rows, :] for q in range(4)], axis=1).astype(BF16)
            him = jnp.concatenate([sim_ref[4 * r + q, rows, :] for q in range(4)], axis=1).astype(BF16)
            ys.append(_dot(hre, cw_ref[r, :GROUP_W, :]) + _dot(him, cw_ref[r, GROUP_W:, :]))
        y = jnp.concatenate(ys, axis=1) + d_ref[...] * get_u(b)
        y = _gelu(y)
        z = _dot(y.astype(BF16), gw_ref[...]) + gb_ref[...]
        out = y * jax.nn.sigmoid(z)
        if single:
            y_ref[...] = out
        else:
            y_ref[b] = out


def _s5_weights(lam_re, lam_im, log_dt, b_re, b_im, c_re, c_im):
    lr, li = jnp.minimum(lam_re, -1e-4), lam_im
    dt = jnp.exp(log_dt)[:, None]
    mag = jnp.exp(lr * dt)
    bar_re, bar_im = mag * jnp.cos(li * dt), mag * jnp.sin(li * dt)
    den = lr * lr + li * li
    f_re = ((bar_re - 1.0) * lr + bar_im * li) / den
    f_im = (bar_im * lr - (bar_re - 1.0) * li) / den
    bb_re = f_re[:, :, None] * b_re - f_im[:, :, None] * b_im
    bb_im = f_re[:, :, None] * b_im + f_im[:, :, None] * b_re
    eye = jnp.eye(8, dtype=F32)

    def pack_b(m):
        m = m.reshape(4, 8, 64, 16)
        return jnp.einsum("rgpc,gh->rgchp", m, eye).reshape(4, LANES, GROUP_W)

    def pack_c(m):
        m = m.reshape(4, 8, 16, 64)
        return jnp.einsum("rgcp,hg->rhpgc", m, eye).reshape(4, GROUP_W, LANES)

    bw = jnp.concatenate([pack_b(bb_re), pack_b(bb_im)], axis=2).astype(BF16)
    cw = jnp.concatenate([pack_c(c_re), -pack_c(c_im)], axis=1).astype(BF16)
    return (bar_re.reshape(1, S5_STATE_W), bar_im.reshape(1, S5_STATE_W), bw, cw)


def _s5(proj, h0re, h0im, wts, d, glu_w, glu_b, *, nb, seq):
    lre, lim, bw, cw = wts
    single = seq == 1
    tc = 1 if single else _row_tile(seq, 344)
    nchunk = seq // tc
    full = lambda shape: pl.BlockSpec(shape, lambda c: tuple(0 for _ in shape))
    if single:
        u_in = proj
        u_spec = pl.BlockSpec((nb, GROUP_W), lambda c: (0, COL_S5))
        y_spec = pl.BlockSpec((nb, GROUP_W), lambda c: (0, 0))
        y_shape = jax.ShapeDtypeStruct((nb, GROUP_W), F32)
    else:
        u_in = proj.reshape(nb, seq, proj.shape[1])
        u_spec = pl.BlockSpec((nb, tc, GROUP_W), lambda c: (0, c, COL_S5))
        y_spec = pl.BlockSpec((nb, tc, GROUP_W), lambda c: (0, c, 0))
        y_shape = jax.ShapeDtypeStruct((nb, seq, GROUP_W), F32)
    y, hre, him = pl.pallas_call(
        functools.partial(_s5_kernel, nb=nb, tc=tc),
        grid=(nchunk,),
        in_specs=[u_spec, full((nb, S5_STATE_W)), full((nb, S5_STATE_W)), full((1, S5_STATE_W)),
                  full((1, S5_STATE_W)), full((4, LANES, 2 * GROUP_W)), full((4, 2 * GROUP_W, LANES)),
                  full((1, GROUP_W)), full((GROUP_W, GROUP_W)), full((1, GROUP_W))],
        out_specs=[y_spec, full((nb, S5_STATE_W)), full((nb, S5_STATE_W))],
        out_shape=[y_shape, jax.ShapeDtypeStruct((nb, S5_STATE_W), F32),
                   jax.ShapeDtypeStruct((nb, S5_STATE_W), F32)],
        scratch_shapes=[pltpu.VMEM((S5_STATE_W // LANES, nb * tc, LANES), F32)] * 2,
        compiler_params=_cparams(("arbitrary",), 56),
        name="s5",
    )(u_in, h0re, h0im, lre, lim, bw, cw, d.reshape(1, GROUP_W), glu_w, glu_b.reshape(1, GROUP_W))
    return y.reshape(nb * seq, GROUP_W), hre, him


def _softplus(x):
    return jnp.maximum(x, 0.0) + jnp.log1p(jnp.exp(-jnp.abs(x)))


def _lru_gates(conv, wa_ref, wx_ref, ba_ref, bx_ref, lam_ref):
    cb = conv.astype(BF16)
    r = jax.nn.sigmoid(_dot(cb, wa_ref[...]) + ba_ref[...])
    i = jax.nn.sigmoid(_dot(cb, wx_ref[...]) + bx_ref[...])
    log_a = -LRU_C * r * _softplus(-lam_ref[...])
    a = jnp.exp(log_a)
    return a, jnp.sqrt(-jnp.tanh(log_a) * (a * a + 1.0)) * (i * conv)


def _lru_kernel(x_ref, g_ref, h0_ref, buf0_ref, cw_ref, cb_ref, wa_ref, ba_ref, wx_ref, bx_ref, lam_ref,
                y_ref, h_ref, buf_ref, xe_ref, sa_ref, sx_ref, *, nb, tc):
    nq = GROUP_W // LANES

    @pl.when(pl.program_id(0) == 0)
    def _():
        h_ref[...] = h0_ref[...]
        for b in range(nb):
            xe_ref[b, 5:8, :] = buf0_ref[b]

    for b in range(nb):
        xe_ref[b, 8:8 + tc, :] = x_ref[b]
    for b in range(nb):
        conv = cb_ref[...] + sum(xe_ref[b, 5 + j:5 + j + tc, :] * cw_ref[j:j + 1, :] for j in range(CONV_W))
        a, xin = _lru_gates(conv, wa_ref, wx_ref, ba_ref, bx_ref, lam_ref)
        for q in range(nq):
            sa_ref[q, b * tc:(b + 1) * tc, :] = a[:, q * LANES:(q + 1) * LANES]
            sx_ref[q, b * tc:(b + 1) * tc, :] = xin[:, q * LANES:(q + 1) * LANES]

    def body(t, hs):
        idx = pl.ds(t, nb, stride=tc)
        new = []
        for q in range(nq):
            h = sa_ref[q, idx, :] * hs[q] + sx_ref[q, idx, :]
            sx_ref[q, idx, :] = h
            new.append(h)
        return tuple(new)

    fin = lax.fori_loop(0, tc, body, tuple(h_ref[:, q * LANES:(q + 1) * LANES] for q in range(nq)))
    for q in range(nq):
        h_ref[:, q * LANES:(q + 1) * LANES] = fin[q]
    for b in range(nb):
        hseq = jnp.concatenate([sx_ref[q, b * tc:(b + 1) * tc, :] for q in range(nq)], axis=1)
        y_ref[b] = hseq * _gelu(g_ref[b])
        tail = xe_ref[b, tc + 5:tc + 8, :]
        xe_ref[b, 5:8, :] = tail
        buf_ref[b] = tail


def _lru_step_kernel(x_ref, g_ref, h0_ref, buf0_ref, cw_ref, cb_ref, wa_ref, ba_ref, wx_ref, bx_ref, lam_ref,
                     y_ref, h_ref, buf_ref):
    x = x_ref[...]
    conv = cb_ref[...] + x * cw_ref[3:4, :] + sum(buf0_ref[j] * cw_ref[j:j + 1, :] for j in range(CONV_W - 1))
    a, xin = _lru_gates(conv, wa_ref, wx_ref, ba_ref, bx_ref, lam_ref)
    h = a * h0_ref[...] + xin
    h_ref[...] = h
    y_ref[...] = h * _gelu(g_ref[...])
    buf_ref[0] = buf0_ref[1]
    buf_ref[1] = buf0_ref[2]
    buf_ref[2] = x


def _block_diag(w):
    return jnp.einsum("hij,hg->higj", w, jnp.eye(8, dtype=w.dtype)).reshape(GROUP_W, GROUP_W)


def _lru(proj, h0, buf0, conv_w, conv_b, wa, ba, wx, bx, lam, *, nb, seq):
    single = seq == 1
    row = lambda v: v.reshape(1, GROUP_W)
    wts = (conv_w, row(conv_b), _block_diag(wa).astype(BF16), row(ba), _block_diag(wx).astype(BF16), row(bx), row(lam))
    full = lambda shape: pl.BlockSpec(shape, lambda c: tuple(0 for _ in shape))
    wspecs = [full((CONV_W, GROUP_W)), full((1, GROUP_W)), full((GROUP_W, GROUP_W)), full((1, GROUP_W)),
              full((GROUP_W, GROUP_W)), full((1, GROUP_W)), full((1, GROUP_W))]
    if single:
        y, h, buf = pl.pallas_call(
            _lru_step_kernel,
            grid=(1,),
            in_specs=[pl.BlockSpec((nb, GROUP_W), lambda c: (0, COL_LX)), pl.BlockSpec((nb, GROUP_W), lambda c: (0, COL_LG)),
                      full((nb, GROUP_W)), full((CONV_W - 1, nb, GROUP_W))] + wspecs,
            out_specs=[full((nb, GROUP_W)), full((nb, GROUP_W)), full((CONV_W - 1, nb, GROUP_W))],
            out_shape=[jax.ShapeDtypeStruct((nb, GROUP_W), F32), jax.ShapeDtypeStruct((nb, GROUP_W), F32),
                       jax.ShapeDtypeStruct((CONV_W - 1, nb, GROUP_W), F32)],
            compiler_params=_cparams(("arbitrary",), 32),
            name="lru_step",
        )(proj, proj, h0, jnp.transpose(buf0, (1, 0, 2)), *wts)
        return y, h, jnp.transpose(buf, (1, 0, 2))
    tc = _row_tile(seq, 344)
    p3 = proj.reshape(nb, seq, proj.shape[1])
    y, h, buf = pl.pallas_call(
        functools.partial(_lru_kernel, nb=nb, tc=tc),
        grid=(seq // tc,),
        in_specs=[pl.BlockSpec((nb, tc, GROUP_W), lambda c: (0, c, COL_LX)),
                  pl.BlockSpec((nb, tc, GROUP_W), lambda c: (0, c, COL_LG)),
                  full((nb, GROUP_W)), full((nb, CONV_W - 1, GROUP_W))] + wspecs,
        out_specs=[pl.BlockSpec((nb, tc, GROUP_W), lambda c: (0, c, 0)), full((nb, GROUP_W)),
                   full((nb, CONV_W - 1, GROUP_W))],
        out_shape=[jax.ShapeDtypeStruct((nb, seq, GROUP_W), F32), jax.ShapeDtypeStruct((nb, GROUP_W), F32),
                   jax.ShapeDtypeStruct((nb, CONV_W - 1, GROUP_W), F32)],
        scratch_shapes=[pltpu.VMEM((nb, tc + 8, GROUP_W), F32), pltpu.VMEM((GROUP_W // LANES, nb * tc, LANES), F32),
                        pltpu.VMEM((GROUP_W // LANES, nb * tc, LANES), F32)],
        compiler_params=_cparams(("arbitrary",), 48),
        name="lru",
    )(p3, p3, h0, buf0, *wts)
    return y.reshape(nb * seq, GROUP_W), h, buf


def _ret_log_gamma(h):
    return math.log(1.0 - 2.0 ** (-5.0 - h))


def _rotary_tables(pos):
    half = HEAD_D // 2
    inv = 1.0 / (10000.0 ** (jnp.arange(half, dtype=F32) / half))
    ang = pos.astype(F32)[:, None] * inv[None, :]
    cos, sin = jnp.cos(ang), jnp.sin(ang)
    return jnp.concatenate([cos, cos], axis=1), jnp.concatenate([-sin, sin], axis=1)


def _rotate(x, cos2, sin2):
    return x * cos2 + pltpu.roll(x, HEAD_D // 2, 1) * sin2


def _ret_kernel(q_ref, k_ref, v_ref, g_ref, cos_ref, sin_ref, ng_ref, y_ref, s_ref, *, seq):
    c = pl.program_id(1)

    @pl.when(c == 0)
    def _():
        s_ref[...] = jnp.zeros_like(s_ref)

    nv = jnp.minimum(LANES, seq - c * LANES)
    nvf = nv.astype(F32)
    row = lax.broadcasted_iota(jnp.int32, (LANES, 1), 0)
    rowf = row.astype(F32)
    valid = row < nv
    diff = (lax.broadcasted_iota(jnp.int32, (LANES, LANES), 0)
            - lax.broadcasted_iota(jnp.int32, (LANES, LANES), 1)).astype(F32)
    cos2, sin2 = cos_ref[...], sin_ref[...]
    for h in range(N_HEADS):
        lg = _ret_log_gamma(h)
        cols = slice(h * HEAD_D, (h + 1) * HEAD_D)
        decay = jnp.exp(jnp.where(diff >= 0, diff * lg, NEG_INF))
        q_dec = jnp.exp((rowf + 1.0) * lg)
        k_dec = jnp.exp((nvf - 1.0 - rowf) * lg)
        c_dec = jnp.exp(jnp.full((1, 1), lg, F32) * nvf)
        q = q_ref[:, cols]
        k = jnp.where(valid, k_ref[:, cols], 0.0)
        v = jnp.where(valid, v_ref[:, cols], 0.0)
        qr = _rotate(q, cos2, sin2)
        kr = _rotate(k, cos2, sin2) * (HEAD_D ** -0.5)
        vb = v.astype(BF16)
        att = _dot_nt(qr.astype(BF16), kr.astype(BF16)) * decay
        s_old = s_ref[h]
        o = _dot(att.astype(BF16), vb) + _dot((qr * q_dec).astype(BF16), s_old.astype(BF16))
        s_ref[h] = s_old * c_dec + _dot_tn((kr * k_dec).astype(BF16), vb)
        o = o * lax.rsqrt(jnp.mean(o * o, axis=-1, keepdims=True) + EPS) * ng_ref[:, cols]
        g = g_ref[:, cols]
        y_ref[:, cols] = o * (g * jax.nn.sigmoid(g))


def _ret_prompt(proj, norm_g, *, nb, seq):
    p3 = proj.reshape(nb, seq, proj.shape[1])
    nchunk = pl.cdiv(seq, LANES)
    cos2, sin2 = _rotary_tables(jnp.arange(nchunk * LANES))
    blk = lambda col: pl.BlockSpec((None, LANES, GROUP_W), lambda b, c: (b, c, col))
    tab = pl.BlockSpec((LANES, HEAD_D), lambda b, c: (c, 0))
    y, s = pl.pallas_call(
        functools.partial(_ret_kernel, seq=seq),
        grid=(nb, nchunk),
        in_specs=[blk(COL_RQ), blk(COL_RK), blk(COL_RV), blk(COL_RG), tab, tab,
                  pl.BlockSpec((1, GROUP_W), lambda b, c: (0, 0))],
        out_specs=[pl.BlockSpec((None, LANES, GROUP_W), lambda b, c: (b, c, 0)),
                   pl.BlockSpec((None, N_HEADS, HEAD_D, HEAD_D), lambda b, c: (b, 0, 0, 0))],
        out_shape=[jax.ShapeDtypeStruct((nb, seq, GROUP_W), F32),
                   jax.ShapeDtypeStruct((nb, N_HEADS, HEAD_D, HEAD_D), F32)],
        compiler_params=_cparams(("arbitrary", "arbitrary"), 32),
        name="retention",
    )(p3, p3, p3, p3, cos2, sin2, norm_g.reshape(1, GROUP_W))
    return y.reshape(nb * seq, GROUP_W), s


def _ret_step_kernel(q_ref, k_ref, v_ref, g_ref, cos_ref, sin_ref, ng_ref, s0_ref, y_ref, s_ref):
    cos2, sin2 = cos_ref[...], sin_ref[...]
    eye = (lax.broadcasted_iota(jnp.int32, (HEAD_D, HEAD_D), 0)
           == lax.broadcasted_iota(jnp.int32, (HEAD_D, HEAD_D), 1))
    for h in range(N_HEADS):
        gamma = 1.0 - 2.0 ** (-5.0 - h)
        cols = slice(h * HEAD_D, (h + 1) * HEAD_D)
        qr = _rotate(q_ref[:, cols], cos2, sin2)
        kr = _rotate(k_ref[:, cols], cos2, sin2) * (HEAD_D ** -0.5)
        v = v_ref[:, cols]
        s0 = s0_ref[h]
        qs = _dot_f32(jnp.broadcast_to(qr, (SUBLANES, HEAD_D)), s0)[0:1]
        o = jnp.sum(qr * kr, axis=-1, keepdims=True) * v + gamma * qs
        kcol = jnp.sum(jnp.where(eye, jnp.broadcast_to(kr, (HEAD_D, HEAD_D)), 0.0), axis=1, keepdims=True)
        s_ref[h] = gamma * s0 + kcol * v
        o = o * lax.rsqrt(jnp.mean(o * o, axis=-1, keepdims=True) + EPS) * ng_ref[:, cols]
        g = g_ref[:, cols]
        y_ref[:, cols] = o * (g * jax.nn.sigmoid(g))


def _ret_step(proj, norm_g, s0, *, nb):
    p3 = proj.reshape(nb, 1, proj.shape[1])
    cos2, sin2 = _rotary_tables(jnp.full((1,), PAST_LEN))
    blk = lambda col: pl.BlockSpec((None, 1, GROUP_W), lambda b: (b, 0, col))
    one = lambda n: pl.BlockSpec((1, n), lambda b: (0, 0))
    st = pl.BlockSpec((None, N_HEADS, HEAD_D, HEAD_D), lambda b: (b, 0, 0, 0))
    y, s = pl.pallas_call(
        _ret_step_kernel,
        grid=(nb,),
        in_specs=[blk(COL_RQ), blk(COL_RK), blk(COL_RV), blk(COL_RG), one(HEAD_D), one(HEAD_D), one(GROUP_W), st],
        out_specs=[pl.BlockSpec((None, 1, GROUP_W), lambda b: (b, 0, 0)), st],
        out_shape=[jax.ShapeDtypeStruct((nb, 1, GROUP_W), F32),
                   jax.ShapeDtypeStruct((nb, N_HEADS, HEAD_D, HEAD_D), F32)],
        compiler_params=_cparams(("arbitrary",), 32),
        name="retention_step",
    )(p3, p3, p3, p3, cos2, sin2, norm_g.reshape(1, GROUP_W), s0)
    return y.reshape(nb, GROUP_W), s


def _log_sigmoid(x):
    return jnp.minimum(x, 0.0) - jnp.log1p(jnp.exp(-jnp.abs(x)))


def _fox_prep_kernel(ff_ref, bf_ref, logf_ref, c_ref):
    logf = _log_sigmoid(ff_ref[...] + bf_ref[...])
    logf_ref[...] = logf
    tri = (lax.broadcasted_iota(jnp.int32, (LANES, LANES), 0)
           <= lax.broadcasted_iota(jnp.int32, (LANES, LANES), 1)).astype(F32)
    carry = jnp.zeros((SUBLANES, 1), F32)
    for j in range(logf.shape[1] // LANES):
        blk = logf[:, j * LANES:(j + 1) * LANES]
        c_ref[:, j * LANES:(j + 1) * LANES] = carry + _dot_f32(blk, tri)
        carry = carry + jnp.sum(blk, axis=1, keepdims=True)


FOX_TK = 512


def _fox_prep(ff, bf, *, nb, seq):
    nblk = pl.cdiv(seq, FOX_TK)
    lp = nblk * FOX_TK
    fft = jnp.transpose(ff[:, :N_HEADS].reshape(nb, seq, N_HEADS), (0, 2, 1))
    fft = jnp.pad(fft, ((0, 0), (0, SUBLANES - N_HEADS), (0, lp - seq)))
    bfc = jnp.pad(bf, (0, SUBLANES - N_HEADS)).reshape(SUBLANES, 1)
    spec = pl.BlockSpec((None, SUBLANES, lp), lambda b: (b, 0, 0))
    logf_t, c_t = pl.pallas_call(
        _fox_prep_kernel,
        grid=(nb,),
        in_specs=[spec, pl.BlockSpec((SUBLANES, 1), lambda b: (0, 0))],
        out_specs=[spec, spec],
        out_shape=[jax.ShapeDtypeStruct((nb, SUBLANES, lp), F32)] * 2,
        compiler_params=_cparams(("arbitrary",), 56),
        name="fox_prep",
    )(fft, bfc)
    logf = jnp.transpose(logf_t[:, :N_HEADS, :seq], (0, 2, 1))
    c_blk = jnp.transpose(c_t.reshape(nb, SUBLANES, nblk, FOX_TK), (0, 2, 1, 3))
    return logf, c_blk


def _softmax_update(carry, s, vb):
    m, l, acc = carry
    mn = jnp.maximum(m, jnp.max(s, axis=-1, keepdims=True))
    p = jnp.exp(s - mn)
    al = jnp.exp(m - mn)
    return mn, al * l + jnp.sum(p, axis=-1, keepdims=True), al * acc + _dot(p.astype(BF16), vb)


def _fox_attn_kernel(q_ref, k_ref, v_ref, c_ref, o_ref, kb_ref, vb_ref, m_ref, l_ref, acc_ref, *, seq):
    i = pl.program_id(1)
    tk = FOX_TK
    nfull = seq // tk
    rem = seq - nfull * tk
    scale = HEAD_D ** -0.5

    @pl.when(i == 0)
    def _():
        kb_ref[...] = k_ref[...].astype(BF16)
        vb_ref[...] = v_ref[...].astype(BF16)

    m_ref[...] = jnp.full_like(m_ref, NEG_INF)
    l_ref[...] = jnp.zeros_like(l_ref)
    acc_ref[...] = jnp.zeros_like(acc_ref)
    qpos = i * LANES + lax.broadcasted_iota(jnp.int32, (LANES, 1), 0)
    head_cols = [slice(h * HEAD_D, (h + 1) * HEAD_D) for h in range(N_HEADS)]
    qbs = [q_ref[:, cols].astype(BF16) for cols in head_cols]

    def tile(rows, cj, kpos):
        mask = kpos <= qpos
        heads = range(N_HEADS)
        ss = [jnp.where(mask, _dot_nt(qbs[h], kb_ref[rows, head_cols[h]]) * scale - cj[h:h + 1, :], NEG_INF)
              for h in heads]
        ms = [m_ref[h] for h in heads]
        mns = [jnp.maximum(ms[h], jnp.max(ss[h], axis=-1, keepdims=True)) for h in heads]
        ps = [jnp.exp(ss[h] - mns[h]) for h in heads]
        als = [jnp.exp(ms[h] - mns[h]) for h in heads]
        pvs = [_dot(ps[h].astype(BF16), vb_ref[rows, head_cols[h]]) for h in heads]
        for h in heads:
            m_ref[h] = mns[h]
            l_ref[h] = als[h] * l_ref[h] + jnp.sum(ps[h], axis=-1, keepdims=True)
            acc_ref[h] = als[h] * acc_ref[h] + pvs[h]

    def body(j, carry):
        r0 = pl.multiple_of(j * tk, tk)
        tile(pl.ds(r0, tk), c_ref[j], j * tk + lax.broadcasted_iota(jnp.int32, (1, tk), 1))
        return carry

    lax.fori_loop(0, jnp.minimum((i * LANES + LANES + tk - 1) // tk, nfull), body, 0)
    if rem:
        @pl.when(i * LANES + LANES > nfull * tk)
        def _():
            tile(slice(nfull * tk, seq), c_ref[nfull][:, 0:rem],
                 nfull * tk + lax.broadcasted_iota(jnp.int32, (1, rem), 1))
    for h, cols in enumerate(head_cols):
        o_ref[:, cols] = acc_ref[h] / l_ref[h]


def _fox_prompt(proj, c_blk, *, nb, seq):
    p3 = proj.reshape(nb, seq, proj.shape[1])
    nkt = c_blk.shape[1]
    kv = lambda col: pl.BlockSpec((None, seq, GROUP_W), lambda b, i: (b, 0, col))
    y = pl.pallas_call(
        functools.partial(_fox_attn_kernel, seq=seq),
        grid=(nb, pl.cdiv(seq, LANES)),
        in_specs=[pl.BlockSpec((None, LANES, GROUP_W), lambda b, i: (b, i, COL_FQ)), kv(COL_FK), kv(COL_FV),
                  pl.BlockSpec((None, nkt, SUBLANES, FOX_TK), lambda b, i: (b, 0, 0, 0))],
        out_specs=pl.BlockSpec((None, LANES, GROUP_W), lambda b, i: (b, i, 0)),
        out_shape=jax.ShapeDtypeStruct((nb, seq, GROUP_W), F32),
        scratch_shapes=[pltpu.VMEM((seq, GROUP_W), BF16), pltpu.VMEM((seq, GROUP_W), BF16),
                        pltpu.VMEM((N_HEADS, LANES, 1), F32), pltpu.VMEM((N_HEADS, LANES, 1), F32),
                        pltpu.VMEM((N_HEADS, LANES, HEAD_D), F32)],
        compiler_params=_cparams(("arbitrary", "arbitrary"), 48),
        name="fox_attn",
    )(p3, p3, p3, c_blk)
    return y.reshape(nb * seq, GROUP_W)


def _fox_step_kernel(pt_ref, q_ref, kn_ref, vn_ref, ff_ref, bf_ref, *rest, pps):
    k_refs, v_refs, lf_refs = rest[:pps], rest[pps:2 * pps], rest[2 * pps:3 * pps]
    o_ref, logf_ref, qbd_ref, m_ref, l_ref, acc_ref, carry_ref = rest[3 * pps:]
    j = pl.program_id(1)
    scale = HEAD_D ** -0.5
    head_of_col = lax.broadcasted_iota(jnp.int32, (SUBLANES, GROUP_W), 1) // HEAD_D
    bd = head_of_col == lax.broadcasted_iota(jnp.int32, (SUBLANES, GROUP_W), 0)

    @pl.when(j == 0)
    def _():
        qbd_ref[...] = jnp.where(bd, jnp.broadcast_to(q_ref[...], (SUBLANES, GROUP_W)), 0.0).astype(BF16)
        m_ref[...] = jnp.full_like(m_ref, NEG_INF)
        l_ref[...] = jnp.zeros_like(l_ref)
        acc_ref[...] = jnp.zeros_like(acc_ref)
        carry_ref[...] = jnp.zeros_like(carry_ref)

    later = (lax.broadcasted_iota(jnp.int32, (PAGE_SIZE, PAGE_SIZE), 0)
             > lax.broadcasted_iota(jnp.int32, (PAGE_SIZE, PAGE_SIZE), 1)).astype(F32)
    def page(ref):
        flat = ref.reshape(PAGE_SIZE * N_HEADS, HEAD_D)
        return jnp.concatenate([flat[pl.ds(h, PAGE_SIZE, stride=N_HEADS), :] for h in range(N_HEADS)],
                               axis=1).astype(BF16)

    qbd = qbd_ref[...]
    run = carry_ref[...]
    lps = [lf_refs[r][...] for r in range(pps)]
    suffix = _dot_f32(jnp.concatenate(lps, axis=0), later)
    scores = []
    for r in range(pps):
        bias = run + suffix[r * SUBLANES:(r + 1) * SUBLANES]
        run = run + jnp.sum(lps[r], axis=1, keepdims=True)
        scores.append(_dot_nt(qbd, page(k_refs[r])) * scale + bias)
    carry_ref[...] = run
    m = m_ref[...]
    mn = m
    for s in scores:
        mn = jnp.maximum(mn, jnp.max(s, axis=-1, keepdims=True))
    al = jnp.exp(m - mn)
    l = al * l_ref[...]
    acc = al * acc_ref[...]
    for r, s in enumerate(scores):
        p = jnp.exp(s - mn)
        l = l + jnp.sum(p, axis=-1, keepdims=True)
        acc = acc + _dot(p.astype(BF16), page(v_refs[r]))
    m_ref[...], l_ref[...], acc_ref[...] = mn, l, acc

    @pl.when(j == pl.num_programs(1) - 1)
    def _():
        logf = _log_sigmoid(ff_ref[...] + bf_ref[...])
        logf_ref[...] = jnp.broadcast_to(logf, (SUBLANES, LANES))
        qf = qbd_ref[...].astype(F32)
        kn = kn_ref[...].astype(BF16).astype(F32)
        s_new = jnp.sum(qf * kn, axis=1, keepdims=True) * scale - logf
        m, l, acc = m_ref[...], l_ref[...], acc_ref[...]
        mn = jnp.maximum(m, s_new)
        p = jnp.exp(s_new - mn)
        al = jnp.exp(m - mn)
        out = (al * acc + p * vn_ref[...]) / (al * l + p)
        o_ref[...] = jnp.sum(jnp.where(bd, out, 0.0), axis=0, keepdims=True)


def _fox_caches(cache_k, cache_v, cache_lf):
    ck, cv = cache_k, cache_v
    clf = jnp.pad(jnp.transpose(cache_lf, (0, 1, 3, 2)), ((0, 0), (0, 0), (0, SUBLANES - N_HEADS), (0, 0)))
    return ck, cv, clf


def _fox_step(proj, ff, bf, caches, layer, page_table, *, nb, pps=8):
    ck, cv, clf = caches
    npages = page_table.shape[1]
    p3 = proj.reshape(nb, 1, proj.shape[1])
    ffc = jnp.pad(ff[:, :N_HEADS], ((0, 0), (0, SUBLANES - N_HEADS))).reshape(nb, SUBLANES, 1)
    bfc = jnp.pad(bf, (0, SUBLANES - N_HEADS)).reshape(SUBLANES, 1)
    row = lambda col: pl.BlockSpec((None, 1, GROUP_W), lambda b, j, pt: (b, 0, col))

    def page(shape, r):
        return pl.BlockSpec((None, None) + shape,
                            lambda b, j, pt: (layer, pt[b, npages - 1 - (j * pps + r)]) + (0,) * len(shape))

    in_specs = ([row(COL_FQ), row(COL_FK), row(COL_FV),
                 pl.BlockSpec((None, SUBLANES, 1), lambda b, j, pt: (b, 0, 0)),
                 pl.BlockSpec((SUBLANES, 1), lambda b, j, pt: (0, 0))]
                + [page((PAGE_SIZE, N_HEADS, HEAD_D), r) for r in range(pps)]
                + [page((PAGE_SIZE, N_HEADS, HEAD_D), r) for r in range(pps)]
                + [page((SUBLANES, PAGE_SIZE), r) for r in range(pps)])
    y, logf = pl.pallas_call(
        functools.partial(_fox_step_kernel, pps=pps),
        grid_spec=pltpu.PrefetchScalarGridSpec(
            num_scalar_prefetch=1,
            grid=(nb, npages // pps),
            in_specs=in_specs,
            out_specs=[pl.BlockSpec((None, 1, GROUP_W), lambda b, j, pt: (b, 0, 0)),
                       pl.BlockSpec((None, SUBLANES, LANES), lambda b, j, pt: (b, 0, 0))],
            scratch_shapes=[pltpu.VMEM((SUBLANES, GROUP_W), BF16), pltpu.VMEM((SUBLANES, 1), F32),
                            pltpu.VMEM((SUBLANES, 1), F32), pltpu.VMEM((SUBLANES, GROUP_W), F32),
                            pltpu.VMEM((SUBLANES, 1), F32)]),
        out_shape=[jax.ShapeDtypeStruct((nb, 1, GROUP_W), F32), jax.ShapeDtypeStruct((nb, SUBLANES, LANES), F32)],
        compiler_params=_cparams(("arbitrary", "arbitrary"), 32),
        name="fox_step",
    )(page_table, p3, p3, p3, ffc, bfc, *([ck] * pps), *([cv] * pps), *([clf] * pps))
    return y.reshape(nb, GROUP_W), logf[:, :N_HEADS, 0]


N_TOP = PEER_TOPK + 1


def _top_values(s1, s2):
    rows = lax.broadcasted_iota(jnp.int32, (24, 1), 0)

    def rnd(r, carry):
        out = []
        for work, top in carry:
            m = jnp.max(work, axis=0, keepdims=True)
            out.append((jnp.where(work == m, NEG_INF, work), jnp.where(rows == r, m, top)))
        return tuple(out)

    init = jnp.full((24, s1.shape[1]), NEG_INF, F32)
    (_, a), (_, b) = lax.fori_loop(0, N_TOP, rnd, ((s1, init), (s2, init)))
    return a, b


def _peer_route_kernel(q_ref, sk_ref, s1_ref, s2_ref, tau_ref, *, tmr):
    s1 = _dot_nt(sk_ref[0], q_ref[:, :PEER_NKEYS].astype(BF16))
    s2 = _dot_nt(sk_ref[1], q_ref[:, PEER_NKEYS:].astype(BF16))
    a, b = _top_values(s1, s2)
    r24 = lax.broadcasted_iota(jnp.int32, (24, 1), 0)
    r8 = lax.broadcasted_iota(jnp.int32, (8, 1), 0)
    a8, b8 = a[0:8], b[0:8]
    slabs = [a[0:1] + b, jnp.where(r24 >= 1, a + b[0:1], NEG_INF)]
    for i in range(1, 5):
        slabs.append(jnp.where((r8 >= 1) & (r8 < N_TOP // (i + 1)), a[i:i + 1] + b8, NEG_INF))
    slabs.append(jnp.where(r8 >= 5, a8 + b[1:2], NEG_INF))
    cand = jnp.concatenate(slabs, axis=0)

    def rnd(_, carry):
        work, _, cur = carry
        m = jnp.max(work, axis=0, keepdims=True)
        return jnp.where(work == m, NEG_INF, work), cur, m

    init = jnp.max(cand, axis=0, keepdims=True)
    _, t16, t17 = lax.fori_loop(0, N_TOP, rnd, (cand, init, init))
    m0 = a[0:1] + b[0:1]
    z = jnp.sum(jnp.where(cand >= t16, jnp.exp(cand - m0), 0.0), axis=0, keepdims=True)
    log2z = jnp.log2(z)
    s1n = (s1 - a[0:1]) * LOG2E
    s2n = (s2 - b[0:1]) * LOG2E - log2z
    taun = (0.5 * (t16 + t17) - m0) * LOG2E - log2z
    for q in range(tmr // LANES):
        lanes = slice(q * LANES, (q + 1) * LANES)
        s1_ref[q] = s1n[:, lanes]
        s2_ref[q] = s2n[:, lanes]
        tau_ref[q] = jnp.broadcast_to(taun[:, lanes], (SUBLANES, LANES))


def _peer_route(qp, subkeys):
    t = qp.shape[0]
    tmr = 2 * LANES if t > LANES else LANES
    ntile = pl.cdiv(t, tmr)
    k = tmr // LANES
    sblk = pl.BlockSpec((None, k, PEER_NKEYS, LANES), lambda i, h: (h, i, 0, 0))
    return pl.pallas_call(
        functools.partial(_peer_route_kernel, tmr=tmr),
        grid=(ntile, PEER_HEADS),
        in_specs=[pl.BlockSpec((tmr, 2 * PEER_NKEYS), lambda i, h: (i, h)),
                  pl.BlockSpec((None, 2, PEER_NKEYS, PEER_NKEYS), lambda i, h: (h, 0, 0, 0))],
        out_specs=[sblk, sblk, pl.BlockSpec((None, k, SUBLANES, LANES), lambda i, h: (h, i, 0, 0))],
        out_shape=[jax.ShapeDtypeStruct((PEER_HEADS, ntile * k, PEER_NKEYS, LANES), F32)] * 2
        + [jax.ShapeDtypeStruct((PEER_HEADS, ntile * k, SUBLANES, LANES), F32)],
        compiler_params=_cparams(("arbitrary", "arbitrary"), 32),
        name="peer_route",
    )(qp, subkeys)


def _peer_dense_kernel(x_ref, h_ref, u_ref, v_ref, s1_ref, s2_ref, tau_ref, o_ref, xu_ref, ht_ref, *, tm, et):
    j = pl.program_id(1)

    @pl.when(j == 0)
    def _():
        o_ref[...] = h_ref[...]

    xu_ref[...] = _dot_nt(u_ref[...], x_ref[...])
    na = et // PEER_NKEYS
    for ap in range(na):
        a_glob = j * na + ap
        rows = slice(ap * PEER_NKEYS, (ap + 1) * PEER_NKEYS)
        for lt in range(tm // LANES):
            lanes = slice(lt * LANES, (lt + 1) * LANES)
            gate = jnp.zeros((PEER_NKEYS, LANES), F32)
            for h in range(PEER_HEADS):
                v2 = s2_ref[h, lt] + s1_ref[h, lt, pl.ds(a_glob, 1), :]
                gate = gate + jnp.exp2(jnp.where(v2 >= tau_ref[h, lt, 0:1, :], v2, NEG_INF))
            ht_ref[rows, lanes] = (_gelu(xu_ref[rows, lanes]) * gate).astype(BF16)
    o_ref[...] += _dot_tn(ht_ref[...], v_ref[...])


def _peer_dense(xn, h, u_tab, v_tab, layer, s1n, s2n, tau):
    t, d = h.shape
    ne = u_tab.shape[1]
    tm = 5 * LANES if t > 5 * LANES else LANES
    et = 4 * PEER_NKEYS
    k = tm // LANES
    rblk = lambda rows: pl.BlockSpec((PEER_HEADS, k, rows, LANES), lambda i, j: (0, i, 0, 0))
    tab = pl.BlockSpec((None, et, d), lambda i, j: (layer, j, 0))
    return pl.pallas_call(
        functools.partial(_peer_dense_kernel, tm=tm, et=et),
        grid=(pl.cdiv(t, tm), ne // et),
        in_specs=[pl.BlockSpec((tm, d), lambda i, j: (i, 0)), pl.BlockSpec((tm, d), lambda i, j: (i, 0)),
                  tab, tab, rblk(PEER_NKEYS), rblk(PEER_NKEYS), rblk(SUBLANES)],
        out_specs=pl.BlockSpec((tm, d), lambda i, j: (i, 0)),
        out_shape=jax.ShapeDtypeStruct((t, d), F32),
        scratch_shapes=[pltpu.VMEM((et, tm), F32), pltpu.VMEM((et, tm), BF16)],
        compiler_params=_cparams(("arbitrary", "arbitrary"), 56),
        name="peer_dense",
    )(xn, h, u_tab, v_tab, s1n, s2n, tau)


def _peer(h, norm_g, wq, subkeys, u_tab, v_tab, layer):
    t = h.shape[0]
    if t < LANES:
        h = jnp.pad(h, ((0, LANES - t), (0, 0)))
    qp, xn = _q_proj(h, norm_g, wq)
    s1n, s2n, tau = _peer_route(qp, subkeys)
    out = _peer_dense(xn, h, u_tab, v_tab, layer, s1n, s2n, tau)
    return out[:t]


def _prep_w_in(w):
    d = w.shape[0]
    g0 = 4 * GROUP_W
    main = jnp.concatenate([w[:, :g0], w[:, g0 + N_HEADS:]], axis=1).astype(BF16)
    ff = jnp.pad(w[:, g0:g0 + N_HEADS], ((0, 0), (0, LANES - N_HEADS))).astype(BF16)
    return main, ff


def kernel(x_prompt, x_sample, cache_fox_k, cache_fox_v, cache_fox_logf, page_table, state_s5_re, state_s5_im, state_ret, state_lru, state_conv, meta_tokens, norm_mix_g, w_in, w_out, s5_lam_re, s5_lam_im, s5_log_dt, s5_b_re, s5_b_im, s5_c_re, s5_c_im, s5_d, s5_glu_w, s5_glu_b, fox_bf, ret_norm_g, lru_conv_w, lru_conv_b, lru_wa, lru_ba, lru_wx, lru_bx, lru_lam, norm_ffn_g, peer_wq, peer_subkeys, peer_u, peer_v, norm_final_g):
    bp, seq_x, d = x_prompt.shape
    bs = x_sample.shape[0]
    depth = w_in.shape[0]
    seq = seq_x + N_META
    meta = jnp.broadcast_to(meta_tokens[None], (bp, N_META, d))
    h_p = jnp.concatenate([meta, x_prompt], axis=1).reshape(bp * seq, d)
    h_s = x_sample.reshape(bs, d)
    zeros = lambda *shape: jnp.zeros(shape, F32)
    u_tab, v_tab = peer_u.astype(BF16), peer_v.astype(BF16)
    caches = _fox_caches(cache_fox_k, cache_fox_v, cache_fox_logf)
    outs_p, outs_s = [], []
    for l in range(depth):
        w_main, w_ff = _prep_w_in(w_in[l])
        w_o = w_out[l].astype(BF16).reshape(4, GROUP_W, d)
        s5w = _s5_weights(s5_lam_re[l], s5_lam_im[l], s5_log_dt[l], s5_b_re[l], s5_b_im[l], s5_c_re[l], s5_c_im[l])
        glu_w = s5_glu_w[l].astype(BF16)
        lru_w = (lru_conv_w[l], lru_conv_b[l], lru_wa[l], lru_ba[l], lru_wx[l], lru_bx[l], lru_lam[l])
        wq = peer_wq[l].astype(BF16)
        subkeys = peer_subkeys[l].astype(BF16)

        proj, ff = _in_proj(h_p, norm_mix_g[l], w_main, w_ff)
        y_s5, s5re, s5im = _s5(proj, zeros(bp, S5_STATE_W), zeros(bp, S5_STATE_W), s5w, s5_d[l], glu_w, s5_glu_b[l],
                               nb=bp, seq=seq)
        logf, c_blk = _fox_prep(ff, fox_bf[l], nb=bp, seq=seq)
        y_fox = _fox_prompt(proj, c_blk, nb=bp, seq=seq)
        y_ret, ret_s = _ret_prompt(proj, ret_norm_g[l], nb=bp, seq=seq)
        y_lru, lru_h, conv_buf = _lru(proj, zeros(bp, GROUP_W), zeros(bp, CONV_W - 1, GROUP_W), *lru_w, nb=bp, seq=seq)
        h_p = _out_proj(h_p, (y_s5, y_fox, y_ret, y_lru), w_o)
        h_p = _peer(h_p, norm_ffn_g[l], wq, subkeys, u_tab, v_tab, l)
        p3 = proj.reshape(bp, seq, -1)
        outs_p.append((p3[:, :, COL_FK * GROUP_W:(COL_FK + 1) * GROUP_W].reshape(bp, seq, N_HEADS, HEAD_D),
                       p3[:, :, COL_FV * GROUP_W:(COL_FV + 1) * GROUP_W].reshape(bp, seq, N_HEADS, HEAD_D),
                       logf, s5re.reshape(bp, -1, 64), s5im.reshape(bp, -1, 64), ret_s, lru_h, conv_buf))

        proj, ff = _in_proj(h_s, norm_mix_g[l], w_main, w_ff)
        y_s5, s5re, s5im = _s5(proj, state_s5_re[l].reshape(bs, S5_STATE_W), state_s5_im[l].reshape(bs, S5_STATE_W),
                               s5w, s5_d[l], glu_w, s5_glu_b[l], nb=bs, seq=1)
        y_fox, logf = _fox_step(proj, ff, fox_bf[l], caches, l, page_table, nb=bs)
        y_ret, ret_s = _ret_step(proj, ret_norm_g[l], state_ret[l], nb=bs)
        y_lru, lru_h, conv_buf = _lru(proj, state_lru[l], state_conv[l], *lru_w, nb=bs, seq=1)
        h_s = _out_proj(h_s, (y_s5, y_fox, y_ret, y_lru), w_o)
        h_s = _peer(h_s, norm_ffn_g[l], wq, subkeys, u_tab, v_tab, l)
        outs_s.append((proj[:, COL_FK * GROUP_W:(COL_FK + 1) * GROUP_W].reshape(bs, 1, N_HEADS, HEAD_D),
                       proj[:, COL_FV * GROUP_W:(COL_FV + 1) * GROUP_W].reshape(bs, 1, N_HEADS, HEAD_D),
                       logf.reshape(bs, 1, N_HEADS), s5re.reshape(bs, -1, 64), s5im.reshape(bs, -1, 64),
                       ret_s, lru_h, conv_buf))

    y_prompt = _rmsnorm(h_p, norm_final_g).reshape(bp, seq, d)[:, N_META:]
    y_sample = _rmsnorm(h_s, norm_final_g).reshape(bs, 1, d)
    stk = lambda outs, j: jnp.stack([o[j] for o in outs], axis=0)
    return ((y_prompt, y_sample) + tuple(stk(outs_p, j) for j in range(8)) + tuple(stk(outs_s, j) for j in range(8)))
```

```python
import functools
import math

import jax
import jax.numpy as jnp
from jax import lax
from jax.experimental import pallas as pl
from jax.experimental.pallas import tpu as pltpu

F32 = jnp.float32
BF16 = jnp.bfloat16
NEG_INF = float("-inf")

EPS = 1e-6
N_META = 16
PAST_LEN = 16384
PAGE_SIZE = 128
GROUP_W = 512
HEAD_D = 128
N_HEADS = 4
S5_STATE_W = 2048
LRU_C = 8.0
CONV_W = 4
PEER_HEADS = 8
PEER_NKEYS = 128
PEER_TOPK = 16
LANES = 128
SUBLANES = 8
MXU_W = 256
LOG2E = 1.4426950408889634
MIB = 1024 * 1024

COL_S5, COL_FQ, COL_FK, COL_FV, COL_RQ, COL_RK, COL_RV, COL_RG, COL_LX, COL_LG = range(10)


def _cparams(sem, vmem_mib, **kw):
    return pltpu.CompilerParams(dimension_semantics=sem, vmem_limit_bytes=vmem_mib * MIB, **kw)


def _dot(a, b):
    return jnp.dot(a, b, preferred_element_type=F32)


def _dot_nt(a, b):
    return lax.dot_general(a, b, (((1,), (1,)), ((), ())), preferred_element_type=F32)


def _dot_tn(a, b):
    return lax.dot_general(a, b, (((0,), (0,)), ((), ())), preferred_element_type=F32)


def _dot_f32(a, b):
    return jnp.dot(a, b, preferred_element_type=F32, precision=lax.Precision.HIGHEST)


def _gelu(x):
    return 0.5 * x * (1.0 + jnp.tanh(0.7978845608028654 * (x + 0.044715 * (x * x * x))))


def _row_tile(n, target):
    if n <= target:
        return n
    best = None
    for t in range(SUBLANES, target + 1, SUBLANES):
        if n % t == 0:
            best = t
    assert best is not None, n
    return best


def _in_proj_kernel(x_ref, g_ref, w_ref, wff_ref, o_ref, off_ref, xn_ref):
    @pl.when(pl.program_id(1) == 0)
    def _():
        x = x_ref[...]
        ms = jnp.mean(x * x, axis=-1, keepdims=True)
        xn = (x * lax.rsqrt(ms + EPS) * g_ref[...]).astype(BF16)
        xn_ref[...] = xn
        off_ref[...] = _dot(xn, wff_ref[...])
    o_ref[...] = _dot(xn_ref[...], w_ref[...])


def _in_proj(x, g, w_main, w_ff):
    t, d = x.shape
    n = w_main.shape[1]
    tm, tn = _row_tile(t, 1032), 512
    return pl.pallas_call(
        _in_proj_kernel,
        grid=(t // tm, n // tn),
        in_specs=[pl.BlockSpec((tm, d), lambda i, j: (i, 0)),
                  pl.BlockSpec((1, d), lambda i, j: (0, 0)),
                  pl.BlockSpec((d, tn), lambda i, j: (0, j)),
                  pl.BlockSpec((d, LANES), lambda i, j: (0, 0))],
        out_specs=[pl.BlockSpec((tm, tn), lambda i, j: (i, j)),
                   pl.BlockSpec((tm, LANES), lambda i, j: (i, 0))],
        out_shape=[jax.ShapeDtypeStruct((t, n), F32), jax.ShapeDtypeStruct((t, LANES), F32)],
        scratch_shapes=[pltpu.VMEM((tm, d), BF16)],
        compiler_params=_cparams(("arbitrary", "arbitrary"), 48),
        name="in_proj",
    )(x, g.reshape(1, d), w_main, w_ff)


def _q_proj_kernel(x_ref, g_ref, w_ref, o_ref, xn_ref):
    @pl.when(pl.program_id(1) == 0)
    def _():
        x = x_ref[...]
        ms = jnp.mean(x * x, axis=-1, keepdims=True)
        xn_ref[...] = (x * lax.rsqrt(ms + EPS) * g_ref[...]).astype(BF16)
    o_ref[...] = _dot(xn_ref[...], w_ref[...])


def _q_proj(x, g, w):
    t, d = x.shape
    n = w.shape[1]
    tm, tn = _row_tile(t, 1032), 512
    return pl.pallas_call(
        _q_proj_kernel,
        grid=(t // tm, n // tn),
        in_specs=[pl.BlockSpec((tm, d), lambda i, j: (i, 0)),
                  pl.BlockSpec((1, d), lambda i, j: (0, 0)),
                  pl.BlockSpec((d, tn), lambda i, j: (0, j))],
        out_specs=[pl.BlockSpec((tm, tn), lambda i, j: (i, j)),
                   pl.BlockSpec((tm, d), lambda i, j: (i, 0))],
        out_shape=[jax.ShapeDtypeStruct((t, n), F32), jax.ShapeDtypeStruct((t, d), BF16)],
        compiler_params=_cparams(("arbitrary", "arbitrary"), 48),
        name="q_proj",
    )(x, g.reshape(1, d), w)


def _out_proj_kernel(h_ref, y0_ref, y1_ref, y2_ref, y3_ref, w_ref, o_ref):
    acc = h_ref[...]
    for gi, y_ref in enumerate((y0_ref, y1_ref, y2_ref, y3_ref)):
        acc = acc + _dot(y_ref[...].astype(BF16), w_ref[gi])
    o_ref[...] = acc


def _out_proj(h, ys, w):
    t, d = h.shape
    tm, tn = _row_tile(t, 1032), 512
    yspec = pl.BlockSpec((tm, GROUP_W), lambda i, j: (i, 0))
    return pl.pallas_call(
        _out_proj_kernel,
        grid=(t // tm, d // tn),
        in_specs=[pl.BlockSpec((tm, tn), lambda i, j: (i, j)), yspec, yspec, yspec, yspec,
                  pl.BlockSpec((4, GROUP_W, tn), lambda i, j: (0, 0, j))],
        out_specs=pl.BlockSpec((tm, tn), lambda i, j: (i, j)),
        out_shape=jax.ShapeDtypeStruct((t, d), F32),
        compiler_params=_cparams(("arbitrary", "arbitrary"), 48),
        name="out_proj",
    )(h, *ys, w)


def _rmsnorm_kernel(x_ref, g_ref, o_ref):
    x = x_ref[...]
    ms = jnp.mean(x * x, axis=-1, keepdims=True)
    o_ref[...] = x * lax.rsqrt(ms + EPS) * g_ref[...]


def _rmsnorm(x, g):
    t, d = x.shape
    tm = _row_tile(t, 1032)
    return pl.pallas_call(
        _rmsnorm_kernel,
        grid=(t // tm,),
        in_specs=[pl.BlockSpec((tm, d), lambda i: (i, 0)), pl.BlockSpec((1, d), lambda i: (0, 0))],
        out_specs=pl.BlockSpec((tm, d), lambda i: (i, 0)),
        out_shape=jax.ShapeDtypeStruct((t, d), F32),
        compiler_params=_cparams(("arbitrary",), 48),
        name="final_norm",
    )(x, g.reshape(1, d))


def _s5_kernel(u_ref, h0re_ref, h0im_ref, lre_ref, lim_ref, bw_ref, cw_ref, d_ref, gw_ref, gb_ref,
               y_ref, hre_ref, him_ref, sre_ref, sim_ref, *, nb, tc):
    single = tc == 1

    @pl.when(pl.program_id(0) == 0)
    def _():
        hre_ref[...] = h0re_ref[...]
        him_ref[...] = h0im_ref[...]

    def get_u(b):
        return u_ref[...] if single else u_ref[b]

    seqs = (0,) if single else tuple(range(nb))
    nrow = nb if single else tc
    for b in seqs:
        ub = get_u(b).astype(BF16)
        for r in range(4):
            bu = _dot(ub[:, r * LANES:(r + 1) * LANES], bw_ref[r])
            for q in range(4):
                sre_ref[4 * r + q, b * nrow:(b + 1) * nrow, :] = bu[:, q * LANES:(q + 1) * LANES]
                sim_ref[4 * r + q, b * nrow:(b + 1) * nrow, :] = bu[:, GROUP_W + q * LANES:GROUP_W + (q + 1) * LANES]

    for r in range(4):
        tiles = tuple(range(4 * r, 4 * r + 4))
        lanes = [slice(lt * LANES, (lt + 1) * LANES) for lt in tiles]
        lr = [jnp.broadcast_to(lre_ref[:, ln], (nb, LANES)) for ln in lanes]
        li = [jnp.broadcast_to(lim_ref[:, ln], (nb, LANES)) for ln in lanes]

        def body(t, carry, tiles=tiles, lr=lr, li=li):
            idx = pl.ds(t, nb, stride=tc)
            new = []
            for q, lt in enumerate(tiles):
                hr, hi = carry[q]
                nr = lr[q] * hr - li[q] * hi + sre_ref[lt, idx, :]
                ni = lr[q] * hi + li[q] * hr + sim_ref[lt, idx, :]
                sre_ref[lt, idx, :] = nr
                sim_ref[lt, idx, :] = ni
                new.append((nr, ni))
            return tuple(new)

        fin = lax.fori_loop(0, tc, body, tuple((hre_ref[:, ln], him_ref[:, ln]) for ln in lanes))
        for q, ln in enumerate(lanes):
            hre_ref[:, ln] = fin[q][0]
            him_ref[:, ln] = fin[q][1]

    for b in seqs:
        rows = slice(b * nrow, (b + 1) * nrow)
        ys = []
        for r in range(4):
            hre = jnp.concatenate([sre_ref[4 * r + q, rows, :] for q in range(4)], axis=1).astype(BF16)
            him = jnp.concatenate([sim_ref[4 * r + q, rows, :] for q in range(4)], axis=1).astype(BF16)
            ys.append(_dot(hre, cw_ref[r, :GROUP_W, :]) + _dot(him, cw_ref[r, GROUP_W:, :]))
        y = jnp.concatenate(ys, axis=1) + d_ref[...] * get_u(b)
        y = _gelu(y)
        z = _dot(y.astype(BF16), gw_ref[...]) + gb_ref[...]
        out = y * jax.nn.sigmoid(z)
        if single:
            y_ref[...] = out
        else:
            y_ref[b] = out


def _s5_weights(lam_re, lam_im, log_dt, b_re, b_im, c_re, c_im):
    lr, li = jnp.minimum(lam_re, -1e-4), lam_im
    dt = jnp.exp(log_dt)[:, None]
    mag = jnp.exp(lr * dt)
    bar_re, bar_im = mag * jnp.cos(li * dt), mag * jnp.sin(li * dt)
    den = lr * lr + li * li
    f_re = ((bar_re - 1.0) * lr + bar_im * li) / den
    f_im = (bar_im * lr - (bar_re - 1.0) * li) / den
    bb_re = f_re[:, :, None] * b_re - f_im[:, :, None] * b_im
    bb_im = f_re[:, :, None] * b_im + f_im[:, :, None] * b_re
    eye = jnp.eye(8, dtype=F32)

    def pack_b(m):
        m = m.reshape(4, 8, 64, 16)
        return jnp.einsum("rgpc,gh->rgchp", m, eye).reshape(4, LANES, GROUP_W)

    def pack_c(m):
        m = m.reshape(4, 8, 16, 64)
        return jnp.einsum("rgcp,hg->rhpgc", m, eye).reshape(4, GROUP_W, LANES)

    bw = jnp.concatenate([pack_b(bb_re), pack_b(bb_im)], axis=2).astype(BF16)
    cw = jnp.concatenate([pack_c(c_re), -pack_c(c_im)], axis=1).astype(BF16)
    return (bar_re.reshape(1, S5_STATE_W), bar_im.reshape(1, S5_STATE_W), bw, cw)


def _s5(proj, h0re, h0im, wts, d, glu_w, glu_b, *, nb, seq):
    lre, lim, bw, cw = wts
    single = seq == 1
    tc = 1 if single else _row_tile(seq, 344)
    nchunk = seq // tc
    full = lambda shape: pl.BlockSpec(shape, lambda c: tuple(0 for _ in shape))
    if single:
        u_in = proj
        u_spec = pl.BlockSpec((nb, GROUP_W), lambda c: (0, COL_S5))
        y_spec = pl.BlockSpec((nb, GROUP_W), lambda c: (0, 0))
        y_shape = jax.ShapeDtypeStruct((nb, GROUP_W), F32)
    else:
        u_in = proj.reshape(nb, seq, proj.shape[1])
        u_spec = pl.BlockSpec((nb, tc, GROUP_W), lambda c: (0, c, COL_S5))
        y_spec = pl.BlockSpec((nb, tc, GROUP_W), lambda c: (0, c, 0))
        y_shape = jax.ShapeDtypeStruct((nb, seq, GROUP_W), F32)
    y, hre, him = pl.pallas_call(
        functools.partial(_s5_kernel, nb=nb, tc=tc),
        grid=(nchunk,),
        in_specs=[u_spec, full((nb, S5_STATE_W)), full((nb, S5_STATE_W)), full((1, S5_STATE_W)),
                  full((1, S5_STATE_W)), full((4, LANES, 2 * GROUP_W)), full((4, 2 * GROUP_W, LANES)),
                  full((1, GROUP_W)), full((GROUP_W, GROUP_W)), full((1, GROUP_W))],
        out_specs=[y_spec, full((nb, S5_STATE_W)), full((nb, S5_STATE_W))],
        out_shape=[y_shape, jax.ShapeDtypeStruct((nb, S5_STATE_W), F32),
                   jax.ShapeDtypeStruct((nb, S5_STATE_W), F32)],
        scratch_shapes=[pltpu.VMEM((S5_STATE_W // LANES, nb * tc, LANES), F32)] * 2,
        compiler_params=_cparams(("arbitrary",), 56),
        name="s5",
    )(u_in, h0re, h0im, lre, lim, bw, cw, d.reshape(1, GROUP_W), glu_w, glu_b.reshape(1, GROUP_W))
    return y.reshape(nb * seq, GROUP_W), hre, him


def _softplus(x):
    return jnp.maximum(x, 0.0) + jnp.log1p(jnp.exp(-jnp.abs(x)))


def _lru_gates(conv, wa_ref, wx_ref, ba_ref, bx_ref, lam_ref):
    cb = conv.astype(BF16)
    r = jax.nn.sigmoid(_dot(cb, wa_ref[...]) + ba_ref[...])
    i = jax.nn.sigmoid(_dot(cb, wx_ref[...]) + bx_ref[...])
    log_a = -LRU_C * r * _softplus(-lam_ref[...])
    a = jnp.exp(log_a)
    return a, jnp.sqrt(-jnp.tanh(log_a) * (a * a + 1.0)) * (i * conv)


def _lru_kernel(x_ref, g_ref, h0_ref, buf0_ref, cw_ref, cb_ref, wa_ref, ba_ref, wx_ref, bx_ref, lam_ref,
                y_ref, h_ref, buf_ref, xe_ref, sa_ref, sx_ref, *, nb, tc):
    nq = GROUP_W // LANES

    @pl.when(pl.program_id(0) == 0)
    def _():
        h_ref[...] = h0_ref[...]
        for b in range(nb):
            xe_ref[b, 5:8, :] = buf0_ref[b]

    for b in range(nb):
        xe_ref[b, 8:8 + tc, :] = x_ref[b]
    for b in range(nb):
        conv = cb_ref[...] + sum(xe_ref[b, 5 + j:5 + j + tc, :] * cw_ref[j:j + 1, :] for j in range(CONV_W))
        a, xin = _lru_gates(conv, wa_ref, wx_ref, ba_ref, bx_ref, lam_ref)
        for q in range(nq):
            sa_ref[q, b * tc:(b + 1) * tc, :] = a[:, q * LANES:(q + 1) * LANES]
            sx_ref[q, b * tc:(b + 1) * tc, :] = xin[:, q * LANES:(q + 1) * LANES]

    def body(t, hs):
        idx = pl.ds(t, nb, stride=tc)
        new = []
        for q in range(nq):
            h = sa_ref[q, idx, :] * hs[q] + sx_ref[q, idx, :]
            sx_ref[q, idx, :] = h
            new.append(h)
        return tuple(new)

    fin = lax.fori_loop(0, tc, body, tuple(h_ref[:, q * LANES:(q + 1) * LANES] for q in range(nq)))
    for q in range(nq):
        h_ref[:, q * LANES:(q + 1) * LANES] = fin[q]
    for b in range(nb):
        hseq = jnp.concatenate([sx_ref[q, b * tc:(b + 1) * tc, :] for q in range(nq)], axis=1)
        y_ref[b] = hseq * _gelu(g_ref[b])
        tail = xe_ref[b, tc + 5:tc + 8, :]
        xe_ref[b, 5:8, :] = tail
        buf_ref[b] = tail


def _lru_step_kernel(x_ref, g_ref, h0_ref, buf0_ref, cw_ref, cb_ref, wa_ref, ba_ref, wx_ref, bx_ref, lam_ref,
                     y_ref, h_ref, buf_ref):
    x = x_ref[...]
    conv = cb_ref[...] + x * cw_ref[3:4, :] + sum(buf0_ref[j] * cw_ref[j:j + 1, :] for j in range(CONV_W - 1))
    a, xin = _lru_gates(conv, wa_ref, wx_ref, ba_ref, bx_ref, lam_ref)
    h = a * h0_ref[...] + xin
    h_ref[...] = h
    y_ref[...] = h * _gelu(g_ref[...])
    buf_ref[0] = buf0_ref[1]
    buf_ref[1] = buf0_ref[2]
    buf_ref[2] = x


def _block_diag(w):
    return jnp.einsum("hij,hg->higj", w, jnp.eye(8, dtype=w.dtype)).reshape(GROUP_W, GROUP_W)


def _lru(proj, h0, buf0, conv_w, conv_b, wa, ba, wx, bx, lam, *, nb, seq):
    single = seq == 1
    row = lambda v: v.reshape(1, GROUP_W)
    wts = (conv_w, row(conv_b), _block_diag(wa).astype(BF16), row(ba), _block_diag(wx).astype(BF16), row(bx), row(lam))
    full = lambda shape: pl.BlockSpec(shape, lambda c: tuple(0 for _ in shape))
    wspecs = [full((CONV_W, GROUP_W)), full((1, GROUP_W)), full((GROUP_W, GROUP_W)), full((1, GROUP_W)),
              full((GROUP_W, GROUP_W)), full((1, GROUP_W)), full((1, GROUP_W))]
    if single:
        y, h, buf = pl.pallas_call(
            _lru_step_kernel,
            grid=(1,),
            in_specs=[pl.BlockSpec((nb, GROUP_W), lambda c: (0, COL_LX)), pl.BlockSpec((nb, GROUP_W), lambda c: (0, COL_LG)),
                      full((nb, GROUP_W)), full((CONV_W - 1, nb, GROUP_W))] + wspecs,
            out_specs=[full((nb, GROUP_W)), full((nb, GROUP_W)), full((CONV_W - 1, nb, GROUP_W))],
            out_shape=[jax.ShapeDtypeStruct((nb, GROUP_W), F32), jax.ShapeDtypeStruct((nb, GROUP_W), F32),
                       jax.ShapeDtypeStruct((CONV_W - 1, nb, GROUP_W), F32)],
            compiler_params=_cparams(("arbitrary",), 32),
            name="lru_step",
        )(proj, proj, h0, jnp.transpose(buf0, (1, 0, 2)), *wts)
        return y, h, jnp.transpose(buf, (1, 0, 2))
    tc = _row_tile(seq, 344)
    p3 = proj.reshape(nb, seq, proj.shape[1])
    y, h, buf = pl.pallas_call(
        functools.partial(_lru_kernel, nb=nb, tc=tc),
        grid=(seq // tc,),
        in_specs=[pl.BlockSpec((nb, tc, GROUP_W), lambda c: (0, c, COL_LX)),
                  pl.BlockSpec((nb, tc, GROUP_W), lambda c: (0, c, COL_LG)),
                  full((nb, GROUP_W)), full((nb, CONV_W - 1, GROUP_W))] + wspecs,
        out_specs=[pl.BlockSpec((nb, tc, GROUP_W), lambda c: (0, c, 0)), full((nb, GROUP_W)),
                   full((nb, CONV_W - 1, GROUP_W))],
        out_shape=[jax.ShapeDtypeStruct((nb, seq, GROUP_W), F32), jax.ShapeDtypeStruct((nb, GROUP_W), F32),
                   jax.ShapeDtypeStruct((nb, CONV_W - 1, GROUP_W), F32)],
        scratch_shapes=[pltpu.VMEM((nb, tc + 8, GROUP_W), F32), pltpu.VMEM((GROUP_W // LANES, nb * tc, LANES), F32),
                        pltpu.VMEM((GROUP_W // LANES, nb * tc, LANES), F32)],
        compiler_params=_cparams(("arbitrary",), 48),
        name="lru",
    )(p3, p3, h0, buf0, *wts)
    return y.reshape(nb * seq, GROUP_W), h, buf


def _ret_log_gamma(h):
    return math.log(1.0 - 2.0 ** (-5.0 - h))


def _rotary_tables(pos):
    half = HEAD_D // 2
    inv = 1.0 / (10000.0 ** (jnp.arange(half, dtype=F32) / half))
    ang = pos.astype(F32)[:, None] * inv[None, :]
    cos, sin = jnp.cos(ang), jnp.sin(ang)
    return jnp.concatenate([cos, cos], axis=1), jnp.concatenate([-sin, sin], axis=1)


def _rotate(x, cos2, sin2):
    return x * cos2 + pltpu.roll(x, HEAD_D // 2, 1) * sin2


def _ret_kernel(q_ref, k_ref, v_ref, g_ref, cos_ref, sin_ref, ng_ref, y_ref, s_ref, *, seq):
    c = pl.program_id(1)

    @pl.when(c == 0)
    def _():
        s_ref[...] = jnp.zeros_like(s_ref)

    nv = jnp.minimum(LANES, seq - c * LANES)
    nvf = nv.astype(F32)
    row = lax.broadcasted_iota(jnp.int32, (LANES, 1), 0)
    rowf = row.astype(F32)
    valid = row < nv
    diff = (lax.broadcasted_iota(jnp.int32, (LANES, LANES), 0)
            - lax.broadcasted_iota(jnp.int32, (LANES, LANES), 1)).astype(F32)
    cos2, sin2 = cos_ref[...], sin_ref[...]
    for h in range(N_HEADS):
        lg = _ret_log_gamma(h)
        cols = slice(h * HEAD_D, (h + 1) * HEAD_D)
        decay = jnp.exp(jnp.where(diff >= 0, diff * lg, NEG_INF))
        q_dec = jnp.exp((rowf + 1.0) * lg)
        k_dec = jnp.exp((nvf - 1.0 - rowf) * lg)
        c_dec = jnp.exp(jnp.full((1, 1), lg, F32) * nvf)
        q = jnp.where(valid, q_ref[:, cols], 0.0)
        k = jnp.where(valid, k_ref[:, cols], 0.0)
        v = jnp.where(valid, v_ref[:, cols], 0.0)
        qr = _rotate(q, cos2, sin2)
        kr = _rotate(k, cos2, sin2) * (HEAD_D ** -0.5)
        vb = v.astype(BF16)
        att = _dot_nt(qr.astype(BF16), kr.astype(BF16)) * decay
        s_old = s_ref[h]
        o = _dot(att.astype(BF16), vb) + _dot((qr * q_dec).astype(BF16), s_old.astype(BF16))
        s_ref[h] = s_old * c_dec + _dot_tn((kr * k_dec).astype(BF16), vb)
        o = o * lax.rsqrt(jnp.mean(o * o, axis=-1, keepdims=True) + EPS) * ng_ref[:, cols]
        g = jnp.where(valid, g_ref[:, cols], 0.0)
        y_ref[:, cols] = o * (g * jax.nn.sigmoid(g))


def _ret_prompt(proj, norm_g, *, nb, seq):
    p3 = proj.reshape(nb, seq, proj.shape[1])
    nchunk = pl.cdiv(seq, LANES)
    cos2, sin2 = _rotary_tables(jnp.arange(nchunk * LANES))
    blk = lambda col: pl.BlockSpec((None, LANES, GROUP_W), lambda b, c: (b, c, col))
    tab = pl.BlockSpec((LANES, HEAD_D), lambda b, c: (c, 0))
    y, s = pl.pallas_call(
        functools.partial(_ret_kernel, seq=seq),
        grid=(nb, nchunk),
        in_specs=[blk(COL_RQ), blk(COL_RK), blk(COL_RV), blk(COL_RG), tab, tab,
                  pl.BlockSpec((1, GROUP_W), lambda b, c: (0, 0))],
        out_specs=[pl.BlockSpec((None, LANES, GROUP_W), lambda b, c: (b, c, 0)),
                   pl.BlockSpec((None, N_HEADS, HEAD_D, HEAD_D), lambda b, c: (b, 0, 0, 0))],
        out_shape=[jax.ShapeDtypeStruct((nb, seq, GROUP_W), F32),
                   jax.ShapeDtypeStruct((nb, N_HEADS, HEAD_D, HEAD_D), F32)],
        compiler_params=_cparams(("arbitrary", "arbitrary"), 32),
        name="retention",
    )(p3, p3, p3, p3, cos2, sin2, norm_g.reshape(1, GROUP_W))
    return y.reshape(nb * seq, GROUP_W), s


def _ret_step_kernel(q_ref, k_ref, v_ref, g_ref, cos_ref, sin_ref, ng_ref, s0_ref, y_ref, s_ref):
    cos2, sin2 = cos_ref[...], sin_ref[...]
    eye = (lax.broadcasted_iota(jnp.int32, (HEAD_D, HEAD_D), 0)
           == lax.broadcasted_iota(jnp.int32, (HEAD_D, HEAD_D), 1))
    for h in range(N_HEADS):
        gamma = 1.0 - 2.0 ** (-5.0 - h)
        cols = slice(h * HEAD_D, (h + 1) * HEAD_D)
        qr = _rotate(q_ref[:, cols], cos2, sin2)
        kr = _rotate(k_ref[:, cols], cos2, sin2) * (HEAD_D ** -0.5)
        v = v_ref[:, cols]
        s0 = s0_ref[h]
        qs = _dot_f32(jnp.broadcast_to(qr, (SUBLANES, HEAD_D)), s0)[0:1]
        o = jnp.sum(qr * kr, axis=-1, keepdims=True) * v + gamma * qs
        kcol = jnp.sum(jnp.where(eye, jnp.broadcast_to(kr, (HEAD_D, HEAD_D)), 0.0), axis=1, keepdims=True)
        s_ref[h] = gamma * s0 + kcol * v
        o = o * lax.rsqrt(jnp.mean(o * o, axis=-1, keepdims=True) + EPS) * ng_ref[:, cols]
        g = g_ref[:, cols]
        y_ref[:, cols] = o * (g * jax.nn.sigmoid(g))


def _ret_step(proj, norm_g, s0, *, nb):
    p3 = proj.reshape(nb, 1, proj.shape[1])
    cos2, sin2 = _rotary_tables(jnp.full((1,), PAST_LEN))
    blk = lambda col: pl.BlockSpec((None, 1, GROUP_W), lambda b: (b, 0, col))
    one = lambda n: pl.BlockSpec((1, n), lambda b: (0, 0))
    st = pl.BlockSpec((None, N_HEADS, HEAD_D, HEAD_D), lambda b: (b, 0, 0, 0))
    y, s = pl.pallas_call(
        _ret_step_kernel,
        grid=(nb,),
        in_specs=[blk(COL_RQ), blk(COL_RK), blk(COL_RV), blk(COL_RG), one(HEAD_D), one(HEAD_D), one(GROUP_W), st],
        out_specs=[pl.BlockSpec((None, 1, GROUP_W), lambda b: (b, 0, 0)), st],
        out_shape=[jax.ShapeDtypeStruct((nb, 1, GROUP_W), F32),
                   jax.ShapeDtypeStruct((nb, N_HEADS, HEAD_D, HEAD_D), F32)],
        compiler_params=_cparams(("arbitrary",), 32),
        name="retention_step",
    )(p3, p3, p3, p3, cos2, sin2, norm_g.reshape(1, GROUP_W), s0)
    return y.reshape(nb, GROUP_W), s


def _log_sigmoid(x):
    return jnp.minimum(x, 0.0) - jnp.log1p(jnp.exp(-jnp.abs(x)))


def _fox_prep_kernel(ff_ref, bf_ref, logf_ref, c_ref):
    logf = _log_sigmoid(ff_ref[...] + bf_ref[...])
    logf_ref[...] = logf
    tri = (lax.broadcasted_iota(jnp.int32, (LANES, LANES), 0)
           <= lax.broadcasted_iota(jnp.int32, (LANES, LANES), 1)).astype(F32)
    carry = jnp.zeros((SUBLANES, 1), F32)
    for j in range(logf.shape[1] // LANES):
        blk = logf[:, j * LANES:(j + 1) * LANES]
        c_ref[:, j * LANES:(j + 1) * LANES] = carry + _dot_f32(blk, tri)
        carry = carry + jnp.sum(blk, axis=1, keepdims=True)


FOX_TK = 512


def _fox_prep(ff, bf, *, nb, seq):
    nblk = pl.cdiv(seq, FOX_TK)
    lp = nblk * FOX_TK
    fft = jnp.transpose(ff[:, :N_HEADS].reshape(nb, seq, N_HEADS), (0, 2, 1))
    fft = jnp.pad(fft, ((0, 0), (0, SUBLANES - N_HEADS), (0, lp - seq)))
    bfc = jnp.pad(bf, (0, SUBLANES - N_HEADS)).reshape(SUBLANES, 1)
    spec = pl.BlockSpec((None, SUBLANES, lp), lambda b: (b, 0, 0))
    logf_t, c_t = pl.pallas_call(
        _fox_prep_kernel,
        grid=(nb,),
        in_specs=[spec, pl.BlockSpec((SUBLANES, 1), lambda b: (0, 0))],
        out_specs=[spec, spec],
        out_shape=[jax.ShapeDtypeStruct((nb, SUBLANES, lp), F32)] * 2,
        compiler_params=_cparams(("arbitrary",), 56),
        name="fox_prep",
    )(fft, bfc)
    logf = jnp.transpose(logf_t[:, :N_HEADS, :seq], (0, 2, 1))
    c_blk = jnp.transpose(c_t.reshape(nb, SUBLANES, nblk, FOX_TK), (0, 2, 1, 3))
    return logf, c_blk


def _softmax_update(carry, s, vb):
    m, l, acc = carry
    mn = jnp.maximum(m, jnp.max(s, axis=-1, keepdims=True))
    p = jnp.exp(s - mn)
    al = jnp.exp(m - mn)
    return mn, al * l + jnp.sum(p, axis=-1, keepdims=True), al * acc + _dot(p.astype(BF16), vb)


def _fox_attn_kernel(q_ref, k_ref, v_ref, c_ref, o_ref, kb_ref, vb_ref, m_ref, l_ref, acc_ref, *, seq):
    i = pl.program_id(1)
    tk = FOX_TK
    nfull = seq // tk
    rem = seq - nfull * tk
    scale = HEAD_D ** -0.5

    @pl.when(i == 0)
    def _():
        kb_ref[...] = k_ref[...].astype(BF16)
        vb_ref[...] = v_ref[...].astype(BF16)

    m_ref[...] = jnp.full_like(m_ref, NEG_INF)
    l_ref[...] = jnp.zeros_like(l_ref)
    acc_ref[...] = jnp.zeros_like(acc_ref)
    qpos = i * LANES + lax.broadcasted_iota(jnp.int32, (LANES, 1), 0)
    head_cols = [slice(h * HEAD_D, (h + 1) * HEAD_D) for h in range(N_HEADS)]
    live = qpos < seq
    qbs = [jnp.where(live, q_ref[:, cols], 0.0).astype(BF16) for cols in head_cols]

    def tile(rows, cj, kpos):
        mask = kpos <= qpos
        heads = range(N_HEADS)
        ss = [jnp.where(mask, _dot_nt(qbs[h], kb_ref[rows, head_cols[h]]) * scale - cj[h:h + 1, :], NEG_INF)
              for h in heads]
        ms = [m_ref[h] for h in heads]
        mns = [jnp.maximum(ms[h], jnp.max(ss[h], axis=-1, keepdims=True)) for h in heads]
        ps = [jnp.exp(ss[h] - mns[h]) for h in heads]
        als = [jnp.exp(ms[h] - mns[h]) for h in heads]
        pvs = [_dot(ps[h].astype(BF16), vb_ref[rows, head_cols[h]]) for h in heads]
        for h in heads:
            m_ref[h] = mns[h]
            l_ref[h] = als[h] * l_ref[h] + jnp.sum(ps[h], axis=-1, keepdims=True)
            acc_ref[h] = als[h] * acc_ref[h] + pvs[h]

    def body(j, carry):
        r0 = pl.multiple_of(j * tk, tk)
        tile(pl.ds(r0, tk), c_ref[j], j * tk + lax.broadcasted_iota(jnp.int32, (1, tk), 1))
        return carry

    lax.fori_loop(0, jnp.minimum((i * LANES + LANES + tk - 1) // tk, nfull), body, 0)
    if rem:
        @pl.when(i * LANES + LANES > nfull * tk)
        def _():
            tile(slice(nfull * tk, seq), c_ref[nfull][:, 0:rem],
                 nfull * tk + lax.broadcasted_iota(jnp.int32, (1, rem), 1))
    for h, cols in enumerate(head_cols):
        o_ref[:, cols] = acc_ref[h] / l_ref[h]


def _fox_prompt(proj, c_blk, *, nb, seq):
    p3 = proj.reshape(nb, seq, proj.shape[1])
    nkt = c_blk.shape[1]
    kv = lambda col: pl.BlockSpec((None, seq, GROUP_W), lambda b, i: (b, 0, col))
    y = pl.pallas_call(
        functools.partial(_fox_attn_kernel, seq=seq),
        grid=(nb, pl.cdiv(seq, LANES)),
        in_specs=[pl.BlockSpec((None, LANES, GROUP_W), lambda b, i: (b, i, COL_FQ)), kv(COL_FK), kv(COL_FV),
                  pl.BlockSpec((None, nkt, SUBLANES, FOX_TK), lambda b, i: (b, 0, 0, 0))],
        out_specs=pl.BlockSpec((None, LANES, GROUP_W), lambda b, i: (b, i, 0)),
        out_shape=jax.ShapeDtypeStruct((nb, seq, GROUP_W), F32),
        scratch_shapes=[pltpu.VMEM((seq, GROUP_W), BF16), pltpu.VMEM((seq, GROUP_W), BF16),
                        pltpu.VMEM((N_HEADS, LANES, 1), F32), pltpu.VMEM((N_HEADS, LANES, 1), F32),
                        pltpu.VMEM((N_HEADS, LANES, HEAD_D), F32)],
        compiler_params=_cparams(("arbitrary", "arbitrary"), 48),
        name="fox_attn",
    )(p3, p3, p3, c_blk)
    return y.reshape(nb * seq, GROUP_W)


def _fox_step_kernel(pt_ref, q_ref, kn_ref, vn_ref, ff_ref, bf_ref, anchor_ref, *rest, pps):
    del anchor_ref
    k_refs, v_refs, lf_refs = rest[:pps], rest[pps:2 * pps], rest[2 * pps:3 * pps]
    o_ref, logf_ref, qbd_ref, m_ref, l_ref, acc_ref, carry_ref = rest[3 * pps:]
    j = pl.program_id(1)
    scale = HEAD_D ** -0.5
    head_of_col = lax.broadcasted_iota(jnp.int32, (SUBLANES, GROUP_W), 1) // HEAD_D
    bd = head_of_col == lax.broadcasted_iota(jnp.int32, (SUBLANES, GROUP_W), 0)

    @pl.when(j == 0)
    def _():
        qbd_ref[...] = jnp.where(bd, jnp.broadcast_to(q_ref[...], (SUBLANES, GROUP_W)), 0.0).astype(BF16)
        m_ref[...] = jnp.full_like(m_ref, NEG_INF)
        l_ref[...] = jnp.zeros_like(l_ref)
        acc_ref[...] = jnp.zeros_like(acc_ref)
        carry_ref[...] = jnp.zeros_like(carry_ref)

    later = (lax.broadcasted_iota(jnp.int32, (PAGE_SIZE, PAGE_SIZE), 0)
             > lax.broadcasted_iota(jnp.int32, (PAGE_SIZE, PAGE_SIZE), 1)).astype(F32)
    def page(ref):
        flat = ref.reshape(PAGE_SIZE * N_HEADS, HEAD_D)
        return jnp.concatenate([flat[pl.ds(h, PAGE_SIZE, stride=N_HEADS), :] for h in range(N_HEADS)],
                               axis=1).astype(BF16)

    qbd = qbd_ref[...]
    run = carry_ref[...]
    lps = [lf_refs[r][...] for r in range(pps)]
    suffix = _dot_f32(jnp.concatenate(lps, axis=0), later)
    scores = []
    for r in range(pps):
        bias = run + suffix[r * SUBLANES:(r + 1) * SUBLANES]
        run = run + jnp.sum(lps[r], axis=1, keepdims=True)
        scores.append(_dot_nt(qbd, page(k_refs[r])) * scale + bias)
    carry_ref[...] = run
    m = m_ref[...]
    mn = m
    for s in scores:
        mn = jnp.maximum(mn, jnp.max(s, axis=-1, keepdims=True))
    al = jnp.exp(m - mn)
    l = al * l_ref[...]
    acc = al * acc_ref[...]
    for r, s in enumerate(scores):
        p = jnp.exp(s - mn)
        l = l + jnp.sum(p, axis=-1, keepdims=True)
        acc = acc + _dot(p.astype(BF16), page(v_refs[r]))
    m_ref[...], l_ref[...], acc_ref[...] = mn, l, acc

    @pl.when(j == pl.num_programs(1) - 1)
    def _():
        logf = _log_sigmoid(ff_ref[...] + bf_ref[...])
        logf_ref[...] = jnp.broadcast_to(logf, (SUBLANES, LANES))
        qf = qbd_ref[...].astype(F32)
        kn = kn_ref[...].astype(BF16).astype(F32)
        s_new = jnp.sum(qf * kn, axis=1, keepdims=True) * scale - logf
        m, l, acc = m_ref[...], l_ref[...], acc_ref[...]
        mn = jnp.maximum(m, s_new)
        p = jnp.exp(s_new - mn)
        al = jnp.exp(m - mn)
        out = (al * acc + p * vn_ref[...]) / (al * l + p)
        o_ref[...] = jnp.sum(jnp.where(bd, out, 0.0), axis=0, keepdims=True)


def _fox_caches(cache_k, cache_v, cache_lf):
    ck, cv = cache_k, cache_v
    clf = jnp.pad(jnp.transpose(cache_lf, (0, 1, 3, 2)), ((0, 0), (0, 0), (0, SUBLANES - N_HEADS), (0, 0)))
    return ck, cv, clf


def _fox_step(proj, ff, bf, caches, layer, page_table, anchor, *, nb, pps=8):
    ck, cv, clf = caches
    npages = page_table.shape[1]
    p3 = proj.reshape(nb, 1, proj.shape[1])
    ffc = jnp.pad(ff[:, :N_HEADS], ((0, 0), (0, SUBLANES - N_HEADS))).reshape(nb, SUBLANES, 1)
    bfc = jnp.pad(bf, (0, SUBLANES - N_HEADS)).reshape(SUBLANES, 1)
    row = lambda col: pl.BlockSpec((None, 1, GROUP_W), lambda b, j, pt: (b, 0, col))

    def page(shape, r):
        return pl.BlockSpec((None, None) + shape,
                            lambda b, j, pt: (layer, pt[b, npages - 1 - (j * pps + r)]) + (0,) * len(shape))

    in_specs = ([row(COL_FQ), row(COL_FK), row(COL_FV),
                 pl.BlockSpec((None, SUBLANES, 1), lambda b, j, pt: (b, 0, 0)),
                 pl.BlockSpec((SUBLANES, 1), lambda b, j, pt: (0, 0)),
                 pl.BlockSpec((SUBLANES, LANES), lambda b, j, pt: (0, 0))]
                + [page((PAGE_SIZE, N_HEADS, HEAD_D), r) for r in range(pps)]
                + [page((PAGE_SIZE, N_HEADS, HEAD_D), r) for r in range(pps)]
                + [page((SUBLANES, PAGE_SIZE), r) for r in range(pps)])
    y, logf = pl.pallas_call(
        functools.partial(_fox_step_kernel, pps=pps),
        grid_spec=pltpu.PrefetchScalarGridSpec(
            num_scalar_prefetch=1,
            grid=(nb, npages // pps),
            in_specs=in_specs,
            out_specs=[pl.BlockSpec((None, 1, GROUP_W), lambda b, j, pt: (b, 0, 0)),
                       pl.BlockSpec((None, SUBLANES, LANES), lambda b, j, pt: (b, 0, 0))],
            scratch_shapes=[pltpu.VMEM((SUBLANES, GROUP_W), BF16), pltpu.VMEM((SUBLANES, 1), F32),
                            pltpu.VMEM((SUBLANES, 1), F32), pltpu.VMEM((SUBLANES, GROUP_W), F32),
                            pltpu.VMEM((SUBLANES, 1), F32)]),
        out_shape=[jax.ShapeDtypeStruct((nb, 1, GROUP_W), F32), jax.ShapeDtypeStruct((nb, SUBLANES, LANES), F32)],
        compiler_params=_cparams(("arbitrary", "arbitrary"), 32),
        name="fox_step",
    )(page_table, p3, p3, p3, ffc, bfc, anchor, *([ck] * pps), *([cv] * pps), *([clf] * pps))
    return y.reshape(nb, GROUP_W), logf[:, :N_HEADS, 0]


N_TOP = PEER_TOPK + 1


def _top_values(s):
    rows = lax.broadcasted_iota(jnp.int32, (24, 1), 0)

    def rnd(r, carry):
        work, top = carry
        m = jnp.max(work, axis=0, keepdims=True)
        return jnp.where(work == m, NEG_INF, work), jnp.where(rows == r, m, top)

    _, top = lax.fori_loop(0, N_TOP, rnd, (s, jnp.full((24, s.shape[1]), NEG_INF, F32)))
    return top


def _peer_route_kernel(q_ref, sk_ref, s1_ref, s2_ref, tau_ref, *, tmr, t_total):
    live = lax.broadcasted_iota(jnp.int32, (tmr, 1), 0) < t_total - pl.program_id(0) * tmr
    q = jnp.where(live, q_ref[...], 0.0).astype(BF16)
    s1 = _dot_nt(sk_ref[0], q[:, :PEER_NKEYS])
    s2 = _dot_nt(sk_ref[1], q[:, PEER_NKEYS:])
    a, b = _top_values(s1), _top_values(s2)
    r24 = lax.broadcasted_iota(jnp.int32, (24, 1), 0)
    r8 = lax.broadcasted_iota(jnp.int32, (8, 1), 0)
    a8, b8 = a[0:8], b[0:8]
    slabs = [a[0:1] + b, jnp.where(r24 >= 1, a + b[0:1], NEG_INF)]
    for i in range(1, 5):
        slabs.append(jnp.where((r8 >= 1) & (r8 < N_TOP // (i + 1)), a[i:i + 1] + b8, NEG_INF))
    slabs.append(jnp.where(r8 >= 5, a8 + b[1:2], NEG_INF))
    cand = jnp.concatenate(slabs, axis=0)

    def rnd(_, carry):
        work, _, cur = carry
        m = jnp.max(work, axis=0, keepdims=True)
        return jnp.where(work == m, NEG_INF, work), cur, m

    init = jnp.max(cand, axis=0, keepdims=True)
    _, t16, t17 = lax.fori_loop(0, N_TOP, rnd, (cand, init, init))
    m0 = a[0:1] + b[0:1]
    z = jnp.sum(jnp.where(cand >= t16, jnp.exp(cand - m0), 0.0), axis=0, keepdims=True)
    log2z = jnp.log2(z)
    s1n = (s1 - a[0:1]) * LOG2E
    s2n = (s2 - b[0:1]) * LOG2E - log2z
    taun = (0.5 * (t16 + t17) - m0) * LOG2E - log2z
    for q in range(tmr // LANES):
        lanes = slice(q * LANES, (q + 1) * LANES)
        s1_ref[q] = s1n[:, lanes]
        s2_ref[q] = s2n[:, lanes]
        tau_ref[q] = jnp.broadcast_to(taun[:, lanes], (SUBLANES, LANES))


def _peer_route(qp, subkeys):
    t = qp.shape[0]
    tmr = 2 * LANES if t > LANES else LANES
    ntile = pl.cdiv(t, tmr)
    k = tmr // LANES
    sblk = pl.BlockSpec((None, k, PEER_NKEYS, LANES), lambda i, h: (h, i, 0, 0))
    return pl.pallas_call(
        functools.partial(_peer_route_kernel, tmr=tmr, t_total=t),
        grid=(ntile, PEER_HEADS),
        in_specs=[pl.BlockSpec((tmr, 2 * PEER_NKEYS), lambda i, h: (i, h)),
                  pl.BlockSpec((None, 2, PEER_NKEYS, PEER_NKEYS), lambda i, h: (h, 0, 0, 0))],
        out_specs=[sblk, sblk, pl.BlockSpec((None, k, SUBLANES, LANES), lambda i, h: (h, i, 0, 0))],
        out_shape=[jax.ShapeDtypeStruct((PEER_HEADS, ntile * k, PEER_NKEYS, LANES), F32)] * 2
        + [jax.ShapeDtypeStruct((PEER_HEADS, ntile * k, SUBLANES, LANES), F32)],
        compiler_params=_cparams(("arbitrary", "arbitrary"), 32),
        name="peer_route",
    )(qp, subkeys)


def _peer_dense_kernel(x_ref, h_ref, u_ref, v_ref, s1_ref, s2_ref, tau_ref, o_ref, xs_ref, xu_ref, ht_ref,
                       *, tm, et, t_total):
    j = pl.program_id(1)

    @pl.when(j == 0)
    def _():
        live = lax.broadcasted_iota(jnp.int32, (tm, 1), 0) < t_total - pl.program_id(0) * tm
        o_ref[...] = jnp.where(live, h_ref[...], 0.0)
        xs_ref[...] = jnp.where(live, x_ref[...], jnp.zeros((), BF16))

    xu_ref[...] = _dot_nt(u_ref[...], xs_ref[...])
    na = et // PEER_NKEYS
    for ap in range(na):
        a_glob = j * na + ap
        rows = slice(ap * PEER_NKEYS, (ap + 1) * PEER_NKEYS)
        for lt in range(tm // LANES):
            lanes = slice(lt * LANES, (lt + 1) * LANES)
            gate = jnp.zeros((PEER_NKEYS, LANES), F32)
            for h in range(PEER_HEADS):
                v2 = s2_ref[h, lt] + s1_ref[h, lt, pl.ds(a_glob, 1), :]
                gate = gate + jnp.exp2(jnp.where(v2 >= tau_ref[h, lt, 0:1, :], v2, NEG_INF))
            ht_ref[rows, lanes] = (_gelu(xu_ref[rows, lanes]) * gate).astype(BF16)
    o_ref[...] += _dot_tn(ht_ref[...], v_ref[...])


def _peer_dense(xn, h, u_tab, v_tab, layer, s1n, s2n, tau):
    t, d = h.shape
    ne = u_tab.shape[1]
    tm = 5 * LANES if t > 5 * LANES else LANES
    et = 4 * PEER_NKEYS
    k = tm // LANES
    rblk = lambda rows: pl.BlockSpec((PEER_HEADS, k, rows, LANES), lambda i, j: (0, i, 0, 0))
    tab = pl.BlockSpec((None, et, d), lambda i, j: (layer, j, 0))
    return pl.pallas_call(
        functools.partial(_peer_dense_kernel, tm=tm, et=et, t_total=t),
        grid=(pl.cdiv(t, tm), ne // et),
        in_specs=[pl.BlockSpec((tm, d), lambda i, j: (i, 0)), pl.BlockSpec((tm, d), lambda i, j: (i, 0)),
                  tab, tab, rblk(PEER_NKEYS), rblk(PEER_NKEYS), rblk(SUBLANES)],
        out_specs=pl.BlockSpec((tm, d), lambda i, j: (i, 0)),
        out_shape=jax.ShapeDtypeStruct((t, d), F32),
        scratch_shapes=[pltpu.VMEM((tm, d), BF16), pltpu.VMEM((et, tm), F32), pltpu.VMEM((et, tm), BF16)],
        compiler_params=_cparams(("arbitrary", "arbitrary"), 56),
        name="peer_dense",
    )(xn, h, u_tab, v_tab, s1n, s2n, tau)


def _peer(h, norm_g, wq, subkeys, u_tab, v_tab, layer):
    t = h.shape[0]
    if t < LANES:
        h = jnp.pad(h, ((0, LANES - t), (0, 0)))
    qp, xn = _q_proj(h, norm_g, wq)
    s1n, s2n, tau = _peer_route(qp, subkeys)
    out = _peer_dense(xn, h, u_tab, v_tab, layer, s1n, s2n, tau)
    return out[:t]


def _prep_w_in(w):
    d = w.shape[0]
    g0 = 4 * GROUP_W
    main = jnp.concatenate([w[:, :g0], w[:, g0 + N_HEADS:]], axis=1).astype(BF16)
    ff = jnp.pad(w[:, g0:g0 + N_HEADS], ((0, 0), (0, LANES - N_HEADS))).astype(BF16)
    return main, ff


def kernel(x_prompt, x_sample, cache_fox_k, cache_fox_v, cache_fox_logf, page_table, state_s5_re, state_s5_im, state_ret, state_lru, state_conv, meta_tokens, norm_mix_g, w_in, w_out, s5_lam_re, s5_lam_im, s5_log_dt, s5_b_re, s5_b_im, s5_c_re, s5_c_im, s5_d, s5_glu_w, s5_glu_b, fox_bf, ret_norm_g, lru_conv_w, lru_conv_b, lru_wa, lru_ba, lru_wx, lru_bx, lru_lam, norm_ffn_g, peer_wq, peer_subkeys, peer_u, peer_v, norm_final_g):
    bp, seq_x, d = x_prompt.shape
    bs = x_sample.shape[0]
    depth = w_in.shape[0]
    seq = seq_x + N_META
    meta = jnp.broadcast_to(meta_tokens[None], (bp, N_META, d))
    h_p = jnp.concatenate([meta, x_prompt], axis=1).reshape(bp * seq, d)
    h_s = x_sample.reshape(bs, d)
    zeros = lambda *shape: jnp.zeros(shape, F32)
    u_tab, v_tab = peer_u.astype(BF16), peer_v.astype(BF16)
    caches = _fox_caches(cache_fox_k, cache_fox_v, cache_fox_logf)
    outs_p, outs_s = [], []
    for l in range(depth):
        w_main, w_ff = _prep_w_in(w_in[l])
        w_o = w_out[l].astype(BF16).reshape(4, GROUP_W, d)
        s5w = _s5_weights(s5_lam_re[l], s5_lam_im[l], s5_log_dt[l], s5_b_re[l], s5_b_im[l], s5_c_re[l], s5_c_im[l])
        glu_w = s5_glu_w[l].astype(BF16)
        lru_w = (lru_conv_w[l], lru_conv_b[l], lru_wa[l], lru_ba[l], lru_wx[l], lru_bx[l], lru_lam[l])
        wq = peer_wq[l].astype(BF16)
        subkeys = peer_subkeys[l].astype(BF16)

        proj, ff = _in_proj(h_p, norm_mix_g[l], w_main, w_ff)
        y_s5, s5re, s5im = _s5(proj, zeros(bp, S5_STATE_W), zeros(bp, S5_STATE_W), s5w, s5_d[l], glu_w, s5_glu_b[l],
                               nb=bp, seq=seq)
        logf, c_blk = _fox_prep(ff, fox_bf[l], nb=bp, seq=seq)
        y_fox = _fox_prompt(proj, c_blk, nb=bp, seq=seq)
        y_ret, ret_s = _ret_prompt(proj, ret_norm_g[l], nb=bp, seq=seq)
        y_lru, lru_h, conv_buf = _lru(proj, zeros(bp, GROUP_W), zeros(bp, CONV_W - 1, GROUP_W), *lru_w, nb=bp, seq=seq)
        h_p = _out_proj(h_p, (y_s5, y_fox, y_ret, y_lru), w_o)
        h_p = _peer(h_p, norm_ffn_g[l], wq, subkeys, u_tab, v_tab, l)
        p3 = proj.reshape(bp, seq, -1)
        outs_p.append((p3[:, :, COL_FK * GROUP_W:(COL_FK + 1) * GROUP_W].reshape(bp, seq, N_HEADS, HEAD_D),
                       p3[:, :, COL_FV * GROUP_W:(COL_FV + 1) * GROUP_W].reshape(bp, seq, N_HEADS, HEAD_D),
                       logf, s5re.reshape(bp, -1, 64), s5im.reshape(bp, -1, 64), ret_s, lru_h, conv_buf))

        proj, ff = _in_proj(h_s, norm_mix_g[l], w_main, w_ff)
        y_s5, s5re, s5im = _s5(proj, state_s5_re[l].reshape(bs, S5_STATE_W), state_s5_im[l].reshape(bs, S5_STATE_W),
                               s5w, s5_d[l], glu_w, s5_glu_b[l], nb=bs, seq=1)
        y_fox, logf = _fox_step(proj, ff, fox_bf[l], caches, l, page_table, h_p, nb=bs)
        y_ret, ret_s = _ret_step(proj, ret_norm_g[l], state_ret[l], nb=bs)
        y_lru, lru_h, conv_buf = _lru(proj, state_lru[l], state_conv[l], *lru_w, nb=bs, seq=1)
        h_s = _out_proj(h_s, (y_s5, y_fox, y_ret, y_lru), w_o)
        h_s = _peer(h_s, norm_ffn_g[l], wq, subkeys, u_tab, v_tab, l)
        outs_s.append((proj[:, COL_FK * GROUP_W:(COL_FK + 1) * GROUP_W].reshape(bs, 1, N_HEADS, HEAD_D),
                       proj[:, COL_FV * GROUP_W:(COL_FV + 1) * GROUP_W].reshape(bs, 1, N_HEADS, HEAD_D),
                       logf.reshape(bs, 1, N_HEADS), s5re.reshape(bs, -1, 64), s5im.reshape(bs, -1, 64),
                       ret_s, lru_h, conv_buf))

    y_prompt = _rmsnorm(h_p, norm_final_g).reshape(bp, seq, d)[:, N_META:]
    y_sample = _rmsnorm(h_s, norm_final_g).reshape(bs, 1, d)
    stk = lambda outs, j: jnp.stack([o[j] for o in outs], axis=0)
    return ((y_prompt, y_sample) + tuple(stk(outs_p, j) for j in range(8)) + tuple(stk(outs_s, j) for j in range(8)))
```

```python
import functools
import math

import jax
import jax.numpy as jnp
from jax import lax
from jax.experimental import pallas as pl
from jax.experimental.pallas import tpu as pltpu

F32 = jnp.float32
BF16 = jnp.bfloat16
NEG_INF = float("-inf")

EPS = 1e-6
N_META = 16
PAST_LEN = 16384
PAGE_SIZE = 128
GROUP_W = 512
HEAD_D = 128
N_HEADS = 4
S5_STATE_W = 2048
LRU_C = 8.0
CONV_W = 4
PEER_HEADS = 8
PEER_NKEYS = 128
PEER_TOPK = 16
LANES = 128
SUBLANES = 8
MXU_W = 256
LOG2E = 1.4426950408889634
MIB = 1024 * 1024

COL_S5, COL_FQ, COL_FK, COL_FV, COL_RQ, COL_RK, COL_RV, COL_RG, COL_LX, COL_LG = range(10)


def _cparams(sem, vmem_mib, **kw):
    return pltpu.CompilerParams(dimension_semantics=sem, vmem_limit_bytes=vmem_mib * MIB, **kw)


def _dot(a, b):
    return jnp.dot(a, b, preferred_element_type=F32)


def _dot_nt(a, b):
    return lax.dot_general(a, b, (((1,), (1,)), ((), ())), preferred_element_type=F32)


def _dot_tn(a, b):
    return lax.dot_general(a, b, (((0,), (0,)), ((), ())), preferred_element_type=F32)


def _dot_f32(a, b):
    return jnp.dot(a, b, preferred_element_type=F32, precision=lax.Precision.HIGHEST)


def _gelu(x):
    return 0.5 * x * (1.0 + jnp.tanh(0.7978845608028654 * (x + 0.044715 * (x * x * x))))


def _row_tile(n, target):
    if n <= target:
        return n
    best = None
    for t in range(SUBLANES, target + 1, SUBLANES):
        if n % t == 0:
            best = t
    assert best is not None, n
    return best


def _in_proj_kernel(x_ref, g_ref, w_ref, wff_ref, o_ref, off_ref, xn_ref):
    @pl.when(pl.program_id(1) == 0)
    def _():
        x = x_ref[...]
        ms = jnp.mean(x * x, axis=-1, keepdims=True)
        xn = (x * lax.rsqrt(ms + EPS) * g_ref[...]).astype(BF16)
        xn_ref[...] = xn
        off_ref[...] = _dot(xn, wff_ref[...])
    o_ref[...] = _dot(xn_ref[...], w_ref[...])


def _repack_w_in_kernel(w_ref, main_ref, ff_ref):
    g0 = 4 * GROUP_W
    w = w_ref[...]
    main_ref[:, :g0] = w[:, :g0].astype(BF16)
    main_ref[:, g0:] = w[:, g0 + N_HEADS:].astype(BF16)
    lane = lax.broadcasted_iota(jnp.int32, (w.shape[0], LANES), 1)
    ff_ref[...] = jnp.where(lane < N_HEADS, w[:, g0:g0 + LANES], 0.0).astype(BF16)


def _repack_w_in(w_in):
    depth, d, n = w_in.shape
    rows = 256
    return pl.pallas_call(
        _repack_w_in_kernel,
        grid=(depth, d // rows),
        in_specs=[pl.BlockSpec((None, rows, n), lambda l, r: (l, r, 0))],
        out_specs=[pl.BlockSpec((None, rows, n - N_HEADS), lambda l, r: (l, r, 0)),
                   pl.BlockSpec((None, rows, LANES), lambda l, r: (l, r, 0))],
        out_shape=[jax.ShapeDtypeStruct((depth, d, n - N_HEADS), BF16), jax.ShapeDtypeStruct((depth, d, LANES), BF16)],
        compiler_params=_cparams(("arbitrary", "arbitrary"), 32),
        name="repack_w_in",
    )(w_in)


def _in_proj(x, g, w_main, w_ff, layer):
    t, d = x.shape
    n = w_main.shape[2]
    tm, tn = _row_tile(t, 1032), 512
    return pl.pallas_call(
        _in_proj_kernel,
        grid=(t // tm, n // tn),
        in_specs=[pl.BlockSpec((tm, d), lambda i, j: (i, 0)),
                  pl.BlockSpec((1, d), lambda i, j: (0, 0)),
                  pl.BlockSpec((None, d, tn), lambda i, j: (layer, 0, j)),
                  pl.BlockSpec((None, d, LANES), lambda i, j: (layer, 0, 0))],
        out_specs=[pl.BlockSpec((tm, tn), lambda i, j: (i, j)),
                   pl.BlockSpec((tm, LANES), lambda i, j: (i, 0))],
        out_shape=[jax.ShapeDtypeStruct((t, n), F32), jax.ShapeDtypeStruct((t, LANES), F32)],
        scratch_shapes=[pltpu.VMEM((tm, d), BF16)],
        compiler_params=_cparams(("arbitrary", "arbitrary"), 48),
        name="in_proj",
    )(x, g.reshape(1, d), w_main, w_ff)


def _q_proj_kernel(x_ref, g_ref, w_ref, o_ref, xn_ref):
    @pl.when(pl.program_id(1) == 0)
    def _():
        x = x_ref[...]
        ms = jnp.mean(x * x, axis=-1, keepdims=True)
        xn_ref[...] = (x * lax.rsqrt(ms + EPS) * g_ref[...]).astype(BF16)
    o_ref[...] = _dot(xn_ref[...], w_ref[...])


def _q_proj(x, g, w):
    t, d = x.shape
    n = w.shape[1]
    tm, tn = _row_tile(t, 1032), 512
    return pl.pallas_call(
        _q_proj_kernel,
        grid=(t // tm, n // tn),
        in_specs=[pl.BlockSpec((tm, d), lambda i, j: (i, 0)),
                  pl.BlockSpec((1, d), lambda i, j: (0, 0)),
                  pl.BlockSpec((d, tn), lambda i, j: (0, j))],
        out_specs=[pl.BlockSpec((tm, tn), lambda i, j: (i, j)),
                   pl.BlockSpec((tm, d), lambda i, j: (i, 0))],
        out_shape=[jax.ShapeDtypeStruct((t, n), F32), jax.ShapeDtypeStruct((t, d), BF16)],
        compiler_params=_cparams(("arbitrary", "arbitrary"), 48),
        name="q_proj",
    )(x, g.reshape(1, d), w)


def _out_proj_kernel(h_ref, y0_ref, y1_ref, y2_ref, y3_ref, w_ref, o_ref):
    acc = h_ref[...]
    for gi, y_ref in enumerate((y0_ref, y1_ref, y2_ref, y3_ref)):
        acc = acc + _dot(y_ref[...].astype(BF16), w_ref[gi])
    o_ref[...] = acc


def _out_proj(h, ys, w):
    t, d = h.shape
    tm, tn = _row_tile(t, 1032), 512
    yspec = pl.BlockSpec((tm, GROUP_W), lambda i, j: (i, 0))
    return pl.pallas_call(
        _out_proj_kernel,
        grid=(t // tm, d // tn),
        in_specs=[pl.BlockSpec((tm, tn), lambda i, j: (i, j)), yspec, yspec, yspec, yspec,
                  pl.BlockSpec((4, GROUP_W, tn), lambda i, j: (0, 0, j))],
        out_specs=pl.BlockSpec((tm, tn), lambda i, j: (i, j)),
        out_shape=jax.ShapeDtypeStruct((t, d), F32),
        compiler_params=_cparams(("arbitrary", "arbitrary"), 48),
        name="out_proj",
    )(h, *ys, w)


def _rmsnorm_kernel(x_ref, g_ref, o_ref):
    x = x_ref[...]
    ms = jnp.mean(x * x, axis=-1, keepdims=True)
    o_ref[...] = x * lax.rsqrt(ms + EPS) * g_ref[...]


def _rmsnorm(x, g):
    t, d = x.shape
    tm = _row_tile(t, 1032)
    return pl.pallas_call(
        _rmsnorm_kernel,
        grid=(t // tm,),
        in_specs=[pl.BlockSpec((tm, d), lambda i: (i, 0)), pl.BlockSpec((1, d), lambda i: (0, 0))],
        out_specs=pl.BlockSpec((tm, d), lambda i: (i, 0)),
        out_shape=jax.ShapeDtypeStruct((t, d), F32),
        compiler_params=_cparams(("arbitrary",), 48),
        name="final_norm",
    )(x, g.reshape(1, d))


def _s5_kernel(u_ref, h0re_ref, h0im_ref, lre_ref, lim_ref, bw_ref, cw_ref, d_ref, gw_ref, gb_ref,
               y_ref, hre_ref, him_ref, sre_ref, sim_ref, *, nb, tc):
    single = tc == 1

    @pl.when(pl.program_id(0) == 0)
    def _():
        hre_ref[...] = h0re_ref[...]
        him_ref[...] = h0im_ref[...]

    def get_u(b):
        return u_ref[...] if single else u_ref[b]

    seqs = (0,) if single else tuple(range(nb))
    nrow = nb if single else tc
    for b in seqs:
        ub = get_u(b).astype(BF16)
        for r in range(4):
            bu = _dot(ub[:, r * LANES:(r + 1) * LANES], bw_ref[r])
            for q in range(4):
                sre_ref[4 * r + q, b * nrow:(b + 1) * nrow, :] = bu[:, q * LANES:(q + 1) * LANES]
                sim_ref[4 * r + q, b * nrow:(b + 1) * nrow, :] = bu[:, GROUP_W + q * LANES:GROUP_W + (q + 1) * LANES]

    for r in range(4):
        tiles = tuple(range(4 * r, 4 * r + 4))
        lanes = [slice(lt * LANES, (lt + 1) * LANES) for lt in tiles]
        lr = [jnp.broadcast_to(lre_ref[:, ln], (nb, LANES)) for ln in lanes]
        li = [jnp.broadcast_to(lim_ref[:, ln], (nb, LANES)) for ln in lanes]

        def body(t, carry, tiles=tiles, lr=lr, li=li):
            idx = pl.ds(t, nb, stride=tc)
            new = []
            for q, lt in enumerate(tiles):
                hr, hi = carry[q]
                nr = lr[q] * hr - li[q] * hi + sre_ref[lt, idx, :]
                ni = lr[q] * hi + li[q] * hr + sim_ref[lt, idx, :]
                sre_ref[lt, idx, :] = nr
                sim_ref[lt, idx, :] = ni
                new.append((nr, ni))
            return tuple(new)

        fin = lax.fori_loop(0, tc, body, tuple((hre_ref[:, ln], him_ref[:, ln]) for ln in lanes))
        for q, ln in enumerate(lanes):
            hre_ref[:, ln] = fin[q][0]
            him_ref[:, ln] = fin[q][1]

    for b in seqs:
        rows = slice(b * nrow, (b + 1) * nrow)
        ys = []
        for r in range(4):
            hre = jnp.concatenate([sre_ref[4 * r + q, rows, :] for q in range(4)], axis=1).astype(BF16)
            him = jnp.concatenate([sim_ref[4 * r + q, rows, :] for q in range(4)], axis=1).astype(BF16)
            ys.append(_dot(hre, cw_ref[r, :GROUP_W, :]) + _dot(him, cw_ref[r, GROUP_W:, :]))
        y = jnp.concatenate(ys, axis=1) + d_ref[...] * get_u(b)
        y = _gelu(y)
        z = _dot(y.astype(BF16), gw_ref[...]) + gb_ref[...]
        out = y * jax.nn.sigmoid(z)
        if single:
            y_ref[...] = out
        else:
            y_ref[b] = out


def _s5_weights(lam_re, lam_im, log_dt, b_re, b_im, c_re, c_im):
    lr, li = jnp.minimum(lam_re, -1e-4), lam_im
    dt = jnp.exp(log_dt)[:, None]
    mag = jnp.exp(lr * dt)
    bar_re, bar_im = mag * jnp.cos(li * dt), mag * jnp.sin(li * dt)
    den = lr * lr + li * li
    f_re = ((bar_re - 1.0) * lr + bar_im * li) / den
    f_im = (bar_im * lr - (bar_re - 1.0) * li) / den
    bb_re = f_re[:, :, None] * b_re - f_im[:, :, None] * b_im
    bb_im = f_re[:, :, None] * b_im + f_im[:, :, None] * b_re
    eye = jnp.eye(8, dtype=F32)

    def pack_b(m):
        m = m.reshape(4, 8, 64, 16)
        return jnp.einsum("rgpc,gh->rgchp", m, eye).reshape(4, LANES, GROUP_W)

    def pack_c(m):
        m = m.reshape(4, 8, 16, 64)
        return jnp.einsum("rgcp,hg->rhpgc", m, eye).reshape(4, GROUP_W, LANES)

    bw = jnp.concatenate([pack_b(bb_re), pack_b(bb_im)], axis=2).astype(BF16)
    cw = jnp.concatenate([pack_c(c_re), -pack_c(c_im)], axis=1).astype(BF16)
    return (bar_re.reshape(1, S5_STATE_W), bar_im.reshape(1, S5_STATE_W), bw, cw)


def _s5(proj, h0re, h0im, wts, d, glu_w, glu_b, *, nb, seq):
    lre, lim, bw, cw = wts
    single = seq == 1
    tc = 1 if single else _row_tile(seq, 344)
    nchunk = seq // tc
    full = lambda shape: pl.BlockSpec(shape, lambda c: tuple(0 for _ in shape))
    if single:
        u_in = proj
        u_spec = pl.BlockSpec((nb, GROUP_W), lambda c: (0, COL_S5))
        y_spec = pl.BlockSpec((nb, GROUP_W), lambda c: (0, 0))
        y_shape = jax.ShapeDtypeStruct((nb, GROUP_W), F32)
    else:
        u_in = proj.reshape(nb, seq, proj.shape[1])
        u_spec = pl.BlockSpec((nb, tc, GROUP_W), lambda c: (0, c, COL_S5))
        y_spec = pl.BlockSpec((nb, tc, GROUP_W), lambda c: (0, c, 0))
        y_shape = jax.ShapeDtypeStruct((nb, seq, GROUP_W), F32)
    y, hre, him = pl.pallas_call(
        functools.partial(_s5_kernel, nb=nb, tc=tc),
        grid=(nchunk,),
        in_specs=[u_spec, full((nb, S5_STATE_W)), full((nb, S5_STATE_W)), full((1, S5_STATE_W)),
                  full((1, S5_STATE_W)), full((4, LANES, 2 * GROUP_W)), full((4, 2 * GROUP_W, LANES)),
                  full((1, GROUP_W)), full((GROUP_W, GROUP_W)), full((1, GROUP_W))],
        out_specs=[y_spec, full((nb, S5_STATE_W)), full((nb, S5_STATE_W))],
        out_shape=[y_shape, jax.ShapeDtypeStruct((nb, S5_STATE_W), F32),
                   jax.ShapeDtypeStruct((nb, S5_STATE_W), F32)],
        scratch_shapes=[pltpu.VMEM((S5_STATE_W // LANES, nb * tc, LANES), F32)] * 2,
        compiler_params=_cparams(("arbitrary",), 56),
        name="s5",
    )(u_in, h0re, h0im, lre, lim, bw, cw, d.reshape(1, GROUP_W), glu_w, glu_b.reshape(1, GROUP_W))
    return y.reshape(nb * seq, GROUP_W), hre, him


def _softplus(x):
    return jnp.maximum(x, 0.0) + jnp.log1p(jnp.exp(-jnp.abs(x)))


def _lru_gates(conv, wa_ref, wx_ref, ba_ref, bx_ref, lam_ref):
    cb = conv.astype(BF16)
    r = jax.nn.sigmoid(_dot(cb, wa_ref[...]) + ba_ref[...])
    i = jax.nn.sigmoid(_dot(cb, wx_ref[...]) + bx_ref[...])
    log_a = -LRU_C * r * _softplus(-lam_ref[...])
    a = jnp.exp(log_a)
    return a, jnp.sqrt(-jnp.tanh(log_a) * (a * a + 1.0)) * (i * conv)


def _lru_kernel(x_ref, g_ref, h0_ref, buf0_ref, cw_ref, cb_ref, wa_ref, ba_ref, wx_ref, bx_ref, lam_ref,
                y_ref, h_ref, buf_ref, xe_ref, sa_ref, sx_ref, *, nb, tc):
    nq = GROUP_W // LANES

    @pl.when(pl.program_id(0) == 0)
    def _():
        h_ref[...] = h0_ref[...]
        for b in range(nb):
            xe_ref[b, 5:8, :] = buf0_ref[b]

    for b in range(nb):
        xe_ref[b, 8:8 + tc, :] = x_ref[b]
    for b in range(nb):
        conv = cb_ref[...] + sum(xe_ref[b, 5 + j:5 + j + tc, :] * cw_ref[j:j + 1, :] for j in range(CONV_W))
        a, xin = _lru_gates(conv, wa_ref, wx_ref, ba_ref, bx_ref, lam_ref)
        for q in range(nq):
            sa_ref[q, b * tc:(b + 1) * tc, :] = a[:, q * LANES:(q + 1) * LANES]
            sx_ref[q, b * tc:(b + 1) * tc, :] = xin[:, q * LANES:(q + 1) * LANES]

    def body(t, hs):
        idx = pl.ds(t, nb, stride=tc)
        new = []
        for q in range(nq):
            h = sa_ref[q, idx, :] * hs[q] + sx_ref[q, idx, :]
            sx_ref[q, idx, :] = h
            new.append(h)
        return tuple(new)

    fin = lax.fori_loop(0, tc, body, tuple(h_ref[:, q * LANES:(q + 1) * LANES] for q in range(nq)))
    for q in range(nq):
        h_ref[:, q * LANES:(q + 1) * LANES] = fin[q]
    for b in range(nb):
        hseq = jnp.concatenate([sx_ref[q, b * tc:(b + 1) * tc, :] for q in range(nq)], axis=1)
        y_ref[b] = hseq * _gelu(g_ref[b])
        tail = xe_ref[b, tc + 5:tc + 8, :]
        xe_ref[b, 5:8, :] = tail
        buf_ref[b] = tail


def _lru_step_kernel(x_ref, g_ref, h0_ref, buf0_ref, cw_ref, cb_ref, wa_ref, ba_ref, wx_ref, bx_ref, lam_ref,
                     y_ref, h_ref, buf_ref):
    x = x_ref[...]
    conv = cb_ref[...] + x * cw_ref[3:4, :] + sum(buf0_ref[j] * cw_ref[j:j + 1, :] for j in range(CONV_W - 1))
    a, xin = _lru_gates(conv, wa_ref, wx_ref, ba_ref, bx_ref, lam_ref)
    h = a * h0_ref[...] + xin
    h_ref[...] = h
    y_ref[...] = h * _gelu(g_ref[...])
    buf_ref[0] = buf0_ref[1]
    buf_ref[1] = buf0_ref[2]
    buf_ref[2] = x


def _block_diag(w):
    return jnp.einsum("hij,hg->higj", w, jnp.eye(8, dtype=w.dtype)).reshape(GROUP_W, GROUP_W)


def _lru(proj, h0, buf0, conv_w, conv_b, wa, ba, wx, bx, lam, *, nb, seq):
    single = seq == 1
    row = lambda v: v.reshape(1, GROUP_W)
    wts = (conv_w, row(conv_b), _block_diag(wa).astype(BF16), row(ba), _block_diag(wx).astype(BF16), row(bx), row(lam))
    full = lambda shape: pl.BlockSpec(shape, lambda c: tuple(0 for _ in shape))
    wspecs = [full((CONV_W, GROUP_W)), full((1, GROUP_W)), full((GROUP_W, GROUP_W)), full((1, GROUP_W)),
              full((GROUP_W, GROUP_W)), full((1, GROUP_W)), full((1, GROUP_W))]
    if single:
        y, h, buf = pl.pallas_call(
            _lru_step_kernel,
            grid=(1,),
            in_specs=[pl.BlockSpec((nb, GROUP_W), lambda c: (0, COL_LX)), pl.BlockSpec((nb, GROUP_W), lambda c: (0, COL_LG)),
                      full((nb, GROUP_W)), full((CONV_W - 1, nb, GROUP_W))] + wspecs,
            out_specs=[full((nb, GROUP_W)), full((nb, GROUP_W)), full((CONV_W - 1, nb, GROUP_W))],
            out_shape=[jax.ShapeDtypeStruct((nb, GROUP_W), F32), jax.ShapeDtypeStruct((nb, GROUP_W), F32),
                       jax.ShapeDtypeStruct((CONV_W - 1, nb, GROUP_W), F32)],
            compiler_params=_cparams(("arbitrary",), 32),
            name="lru_step",
        )(proj, proj, h0, jnp.transpose(buf0, (1, 0, 2)), *wts)
        return y, h, jnp.transpose(buf, (1, 0, 2))
    tc = _row_tile(seq, 344)
    p3 = proj.reshape(nb, seq, proj.shape[1])
    y, h, buf = pl.pallas_call(
        functools.partial(_lru_kernel, nb=nb, tc=tc),
        grid=(seq // tc,),
        in_specs=[pl.BlockSpec((nb, tc, GROUP_W), lambda c: (0, c, COL_LX)),
                  pl.BlockSpec((nb, tc, GROUP_W), lambda c: (0, c, COL_LG)),
                  full((nb, GROUP_W)), full((nb, CONV_W - 1, GROUP_W))] + wspecs,
        out_specs=[pl.BlockSpec((nb, tc, GROUP_W), lambda c: (0, c, 0)), full((nb, GROUP_W)),
                   full((nb, CONV_W - 1, GROUP_W))],
        out_shape=[jax.ShapeDtypeStruct((nb, seq, GROUP_W), F32), jax.ShapeDtypeStruct((nb, GROUP_W), F32),
                   jax.ShapeDtypeStruct((nb, CONV_W - 1, GROUP_W), F32)],
        scratch_shapes=[pltpu.VMEM((nb, tc + 8, GROUP_W), F32), pltpu.VMEM((GROUP_W // LANES, nb * tc, LANES), F32),
                        pltpu.VMEM((GROUP_W // LANES, nb * tc, LANES), F32)],
        compiler_params=_cparams(("arbitrary",), 48),
        name="lru",
    )(p3, p3, h0, buf0, *wts)
    return y.reshape(nb * seq, GROUP_W), h, buf


def _ret_log_gamma(h):
    return math.log(1.0 - 2.0 ** (-5.0 - h))


def _rotary_tables(pos):
    half = HEAD_D // 2
    inv = 1.0 / (10000.0 ** (jnp.arange(half, dtype=F32) / half))
    ang = pos.astype(F32)[:, None] * inv[None, :]
    cos, sin = jnp.cos(ang), jnp.sin(ang)
    return jnp.concatenate([cos, cos], axis=1), jnp.concatenate([-sin, sin], axis=1)


def _rotate(x, cos2, sin2):
    return x * cos2 + pltpu.roll(x, HEAD_D // 2, 1) * sin2


def _ret_kernel(q_ref, k_ref, v_ref, g_ref, cos_ref, sin_ref, ng_ref, y_ref, s_ref, *, seq):
    c = pl.program_id(1)

    @pl.when(c == 0)
    def _():
        s_ref[...] = jnp.zeros_like(s_ref)

    nv = jnp.minimum(LANES, seq - c * LANES)
    nvf = nv.astype(F32)
    row = lax.broadcasted_iota(jnp.int32, (LANES, 1), 0)
    rowf = row.astype(F32)
    valid = row < nv
    diff = (lax.broadcasted_iota(jnp.int32, (LANES, LANES), 0)
            - lax.broadcasted_iota(jnp.int32, (LANES, LANES), 1)).astype(F32)
    cos2, sin2 = cos_ref[...], sin_ref[...]
    for h in range(N_HEADS):
        lg = _ret_log_gamma(h)
        cols = slice(h * HEAD_D, (h + 1) * HEAD_D)
        decay = jnp.exp(jnp.where(diff >= 0, diff * lg, NEG_INF))
        q_dec = jnp.exp((rowf + 1.0) * lg)
        k_dec = jnp.exp((nvf - 1.0 - rowf) * lg)
        c_dec = jnp.exp(jnp.full((1, 1), lg, F32) * nvf)
        q = jnp.where(valid, q_ref[:, cols], 0.0)
        k = jnp.where(valid, k_ref[:, cols], 0.0)
        v = jnp.where(valid, v_ref[:, cols], 0.0)
        qr = _rotate(q, cos2, sin2)
        kr = _rotate(k, cos2, sin2) * (HEAD_D ** -0.5)
        vb = v.astype(BF16)
        att = _dot_nt(qr.astype(BF16), kr.astype(BF16)) * decay
        s_old = s_ref[h]
        o = _dot(att.astype(BF16), vb) + _dot((qr * q_dec).astype(BF16), s_old.astype(BF16))
        s_ref[h] = s_old * c_dec + _dot_tn((kr * k_dec).astype(BF16), vb)
        o = o * lax.rsqrt(jnp.mean(o * o, axis=-1, keepdims=True) + EPS) * ng_ref[:, cols]
        g = jnp.where(valid, g_ref[:, cols], 0.0)
        y_ref[:, cols] = o * (g * jax.nn.sigmoid(g))


def _ret_prompt(proj, norm_g, *, nb, seq):
    p3 = proj.reshape(nb, seq, proj.shape[1])
    nchunk = pl.cdiv(seq, LANES)
    cos2, sin2 = _rotary_tables(jnp.arange(nchunk * LANES))
    blk = lambda col: pl.BlockSpec((None, LANES, GROUP_W), lambda b, c: (b, c, col))
    tab = pl.BlockSpec((LANES, HEAD_D), lambda b, c: (c, 0))
    y, s = pl.pallas_call(
        functools.partial(_ret_kernel, seq=seq),
        grid=(nb, nchunk),
        in_specs=[blk(COL_RQ), blk(COL_RK), blk(COL_RV), blk(COL_RG), tab, tab,
                  pl.BlockSpec((1, GROUP_W), lambda b, c: (0, 0))],
        out_specs=[pl.BlockSpec((None, LANES, GROUP_W), lambda b, c: (b, c, 0)),
                   pl.BlockSpec((None, N_HEADS, HEAD_D, HEAD_D), lambda b, c: (b, 0, 0, 0))],
        out_shape=[jax.ShapeDtypeStruct((nb, seq, GROUP_W), F32),
                   jax.ShapeDtypeStruct((nb, N_HEADS, HEAD_D, HEAD_D), F32)],
        compiler_params=_cparams(("arbitrary", "arbitrary"), 32),
        name="retention",
    )(p3, p3, p3, p3, cos2, sin2, norm_g.reshape(1, GROUP_W))
    return y.reshape(nb * seq, GROUP_W), s


def _ret_step_kernel(q_ref, k_ref, v_ref, g_ref, cos_ref, sin_ref, ng_ref, s0_ref, y_ref, s_ref):
    cos2, sin2 = cos_ref[...], sin_ref[...]
    eye = (lax.broadcasted_iota(jnp.int32, (HEAD_D, HEAD_D), 0)
           == lax.broadcasted_iota(jnp.int32, (HEAD_D, HEAD_D), 1))
    for h in range(N_HEADS):
        gamma = 1.0 - 2.0 ** (-5.0 - h)
        cols = slice(h * HEAD_D, (h + 1) * HEAD_D)
        qr = _rotate(q_ref[:, cols], cos2, sin2)
        kr = _rotate(k_ref[:, cols], cos2, sin2) * (HEAD_D ** -0.5)
        v = v_ref[:, cols]
        s0 = s0_ref[h]
        qs = _dot_f32(jnp.broadcast_to(qr, (SUBLANES, HEAD_D)), s0)[0:1]
        o = jnp.sum(qr * kr, axis=-1, keepdims=True) * v + gamma * qs
        kcol = jnp.sum(jnp.where(eye, jnp.broadcast_to(kr, (HEAD_D, HEAD_D)), 0.0), axis=1, keepdims=True)
        s_ref[h] = gamma * s0 + kcol * v
        o = o * lax.rsqrt(jnp.mean(o * o, axis=-1, keepdims=True) + EPS) * ng_ref[:, cols]
        g = g_ref[:, cols]
        y_ref[:, cols] = o * (g * jax.nn.sigmoid(g))


def _ret_step(proj, norm_g, s0, *, nb):
    p3 = proj.reshape(nb, 1, proj.shape[1])
    cos2, sin2 = _rotary_tables(jnp.full((1,), PAST_LEN))
    blk = lambda col: pl.BlockSpec((None, 1, GROUP_W), lambda b: (b, 0, col))
    one = lambda n: pl.BlockSpec((1, n), lambda b: (0, 0))
    st = pl.BlockSpec((None, N_HEADS, HEAD_D, HEAD_D), lambda b: (b, 0, 0, 0))
    y, s = pl.pallas_call(
        _ret_step_kernel,
        grid=(nb,),
        in_specs=[blk(COL_RQ), blk(COL_RK), blk(COL_RV), blk(COL_RG), one(HEAD_D), one(HEAD_D), one(GROUP_W), st],
        out_specs=[pl.BlockSpec((None, 1, GROUP_W), lambda b: (b, 0, 0)), st],
        out_shape=[jax.ShapeDtypeStruct((nb, 1, GROUP_W), F32),
                   jax.ShapeDtypeStruct((nb, N_HEADS, HEAD_D, HEAD_D), F32)],
        compiler_params=_cparams(("arbitrary",), 32),
        name="retention_step",
    )(p3, p3, p3, p3, cos2, sin2, norm_g.reshape(1, GROUP_W), s0)
    return y.reshape(nb, GROUP_W), s


def _log_sigmoid(x):
    return jnp.minimum(x, 0.0) - jnp.log1p(jnp.exp(-jnp.abs(x)))


def _fox_prep_kernel(ff_ref, bf_ref, logf_ref, c_ref):
    logf = _log_sigmoid(ff_ref[...] + bf_ref[...])
    logf_ref[...] = logf
    tri = (lax.broadcasted_iota(jnp.int32, (LANES, LANES), 0)
           <= lax.broadcasted_iota(jnp.int32, (LANES, LANES), 1)).astype(F32)
    carry = jnp.zeros((SUBLANES, 1), F32)
    for j in range(logf.shape[1] // LANES):
        blk = logf[:, j * LANES:(j + 1) * LANES]
        c_ref[:, j * LANES:(j + 1) * LANES] = carry + _dot_f32(blk, tri)
        carry = carry + jnp.sum(blk, axis=1, keepdims=True)


FOX_TK = 512


def _fox_prep(ff, bf, *, nb, seq):
    nblk = pl.cdiv(seq, FOX_TK)
    lp = nblk * FOX_TK
    fft = jnp.transpose(ff[:, :N_HEADS].reshape(nb, seq, N_HEADS), (0, 2, 1))
    fft = jnp.pad(fft, ((0, 0), (0, SUBLANES - N_HEADS), (0, lp - seq)))
    bfc = jnp.pad(bf, (0, SUBLANES - N_HEADS)).reshape(SUBLANES, 1)
    spec = pl.BlockSpec((None, SUBLANES, lp), lambda b: (b, 0, 0))
    logf_t, c_t = pl.pallas_call(
        _fox_prep_kernel,
        grid=(nb,),
        in_specs=[spec, pl.BlockSpec((SUBLANES, 1), lambda b: (0, 0))],
        out_specs=[spec, spec],
        out_shape=[jax.ShapeDtypeStruct((nb, SUBLANES, lp), F32)] * 2,
        compiler_params=_cparams(("arbitrary",), 56),
        name="fox_prep",
    )(fft, bfc)
    logf = jnp.transpose(logf_t[:, :N_HEADS, :seq], (0, 2, 1))
    c_blk = jnp.transpose(c_t.reshape(nb, SUBLANES, nblk, FOX_TK), (0, 2, 1, 3))
    return logf, c_blk


def _softmax_update(carry, s, vb):
    m, l, acc = carry
    mn = jnp.maximum(m, jnp.max(s, axis=-1, keepdims=True))
    p = jnp.exp(s - mn)
    al = jnp.exp(m - mn)
    return mn, al * l + jnp.sum(p, axis=-1, keepdims=True), al * acc + _dot(p.astype(BF16), vb)


def _fox_attn_kernel(q_ref, k_ref, v_ref, c_ref, o_ref, kb_ref, vb_ref, m_ref, l_ref, acc_ref, *, seq):
    i = pl.program_id(1)
    tk = FOX_TK
    nfull = seq // tk
    rem = seq - nfull * tk
    scale = HEAD_D ** -0.5

    @pl.when(i == 0)
    def _():
        kb_ref[...] = k_ref[...].astype(BF16)
        vb_ref[...] = v_ref[...].astype(BF16)

    m_ref[...] = jnp.full_like(m_ref, NEG_INF)
    l_ref[...] = jnp.zeros_like(l_ref)
    acc_ref[...] = jnp.zeros_like(acc_ref)
    qpos = i * LANES + lax.broadcasted_iota(jnp.int32, (LANES, 1), 0)
    head_cols = [slice(h * HEAD_D, (h + 1) * HEAD_D) for h in range(N_HEADS)]
    live = qpos < seq
    qbs = [jnp.where(live, q_ref[:, cols], 0.0).astype(BF16) for cols in head_cols]

    def tile(rows, cj, kpos):
        mask = kpos <= qpos
        heads = range(N_HEADS)
        ss = [jnp.where(mask, _dot_nt(qbs[h], kb_ref[rows, head_cols[h]]) * scale - cj[h:h + 1, :], NEG_INF)
              for h in heads]
        ms = [m_ref[h] for h in heads]
        mns = [jnp.maximum(ms[h], jnp.max(ss[h], axis=-1, keepdims=True)) for h in heads]
        ps = [jnp.exp(ss[h] - mns[h]) for h in heads]
        als = [jnp.exp(ms[h] - mns[h]) for h in heads]
        pvs = [_dot(ps[h].astype(BF16), vb_ref[rows, head_cols[h]]) for h in heads]
        for h in heads:
            m_ref[h] = mns[h]
            l_ref[h] = als[h] * l_ref[h] + jnp.sum(ps[h], axis=-1, keepdims=True)
            acc_ref[h] = als[h] * acc_ref[h] + pvs[h]

    def body(j, carry):
        r0 = pl.multiple_of(j * tk, tk)
        tile(pl.ds(r0, tk), c_ref[j], j * tk + lax.broadcasted_iota(jnp.int32, (1, tk), 1))
        return carry

    lax.fori_loop(0, jnp.minimum((i * LANES + LANES + tk - 1) // tk, nfull), body, 0)
    if rem:
        @pl.when(i * LANES + LANES > nfull * tk)
        def _():
            tile(slice(nfull * tk, seq), c_ref[nfull][:, 0:rem],
                 nfull * tk + lax.broadcasted_iota(jnp.int32, (1, rem), 1))
    for h, cols in enumerate(head_cols):
        o_ref[:, cols] = acc_ref[h] / l_ref[h]


def _fox_prompt(proj, c_blk, *, nb, seq):
    p3 = proj.reshape(nb, seq, proj.shape[1])
    nkt = c_blk.shape[1]
    kv = lambda col: pl.BlockSpec((None, seq, GROUP_W), lambda b, i: (b, 0, col))
    y = pl.pallas_call(
        functools.partial(_fox_attn_kernel, seq=seq),
        grid=(nb, pl.cdiv(seq, LANES)),
        in_specs=[pl.BlockSpec((None, LANES, GROUP_W), lambda b, i: (b, i, COL_FQ)), kv(COL_FK), kv(COL_FV),
                  pl.BlockSpec((None, nkt, SUBLANES, FOX_TK), lambda b, i: (b, 0, 0, 0))],
        out_specs=pl.BlockSpec((None, LANES, GROUP_W), lambda b, i: (b, i, 0)),
        out_shape=jax.ShapeDtypeStruct((nb, seq, GROUP_W), F32),
        scratch_shapes=[pltpu.VMEM((seq, GROUP_W), BF16), pltpu.VMEM((seq, GROUP_W), BF16),
                        pltpu.VMEM((N_HEADS, LANES, 1), F32), pltpu.VMEM((N_HEADS, LANES, 1), F32),
                        pltpu.VMEM((N_HEADS, LANES, HEAD_D), F32)],
        compiler_params=_cparams(("arbitrary", "arbitrary"), 48),
        name="fox_attn",
    )(p3, p3, p3, c_blk)
    return y.reshape(nb * seq, GROUP_W)


def _fox_step_kernel(pt_ref, q_ref, kn_ref, vn_ref, ff_ref, bf_ref, anchor_ref, *rest, pps):
    del anchor_ref
    k_refs, v_refs, lf_refs = rest[:pps], rest[pps:2 * pps], rest[2 * pps:3 * pps]
    o_ref, logf_ref, qbd_ref, m_ref, l_ref, acc_ref, carry_ref = rest[3 * pps:]
    j = pl.program_id(1)
    scale = HEAD_D ** -0.5
    head_of_col = lax.broadcasted_iota(jnp.int32, (SUBLANES, GROUP_W), 1) // HEAD_D
    bd = head_of_col == lax.broadcasted_iota(jnp.int32, (SUBLANES, GROUP_W), 0)

    @pl.when(j == 0)
    def _():
        qbd_ref[...] = jnp.where(bd, jnp.broadcast_to(q_ref[...], (SUBLANES, GROUP_W)), 0.0).astype(BF16)
        m_ref[...] = jnp.full_like(m_ref, NEG_INF)
        l_ref[...] = jnp.zeros_like(l_ref)
        acc_ref[...] = jnp.zeros_like(acc_ref)
        carry_ref[...] = jnp.zeros_like(carry_ref)

    later = (lax.broadcasted_iota(jnp.int32, (PAGE_SIZE, PAGE_SIZE), 0)
             > lax.broadcasted_iota(jnp.int32, (PAGE_SIZE, PAGE_SIZE), 1)).astype(F32)
    def page(ref):
        flat = ref.reshape(PAGE_SIZE * N_HEADS, HEAD_D)
        return jnp.concatenate([flat[pl.ds(h, PAGE_SIZE, stride=N_HEADS), :] for h in range(N_HEADS)],
                               axis=1).astype(BF16)

    qbd = qbd_ref[...]
    run = carry_ref[...]
    lps = [lf_refs[r][...] for r in range(pps)]
    suffix = _dot_f32(jnp.concatenate(lps, axis=0), later)
    scores = []
    for r in range(pps):
        bias = run + suffix[r * SUBLANES:(r + 1) * SUBLANES]
        run = run + jnp.sum(lps[r], axis=1, keepdims=True)
        scores.append(_dot_nt(qbd, page(k_refs[r])) * scale + bias)
    carry_ref[...] = run
    m = m_ref[...]
    mn = m
    for s in scores:
        mn = jnp.maximum(mn, jnp.max(s, axis=-1, keepdims=True))
    al = jnp.exp(m - mn)
    l = al * l_ref[...]
    acc = al * acc_ref[...]
    for r, s in enumerate(scores):
        p = jnp.exp(s - mn)
        l = l + jnp.sum(p, axis=-1, keepdims=True)
        acc = acc + _dot(p.astype(BF16), page(v_refs[r]))
    m_ref[...], l_ref[...], acc_ref[...] = mn, l, acc

    @pl.when(j == pl.num_programs(1) - 1)
    def _():
        logf = _log_sigmoid(ff_ref[...] + bf_ref[...])
        logf_ref[...] = jnp.broadcast_to(logf, (SUBLANES, LANES))
        qf = qbd_ref[...].astype(F32)
        kn = kn_ref[...].astype(BF16).astype(F32)
        s_new = jnp.sum(qf * kn, axis=1, keepdims=True) * scale - logf
        m, l, acc = m_ref[...], l_ref[...], acc_ref[...]
        mn = jnp.maximum(m, s_new)
        p = jnp.exp(s_new - mn)
        al = jnp.exp(m - mn)
        out = (al * acc + p * vn_ref[...]) / (al * l + p)
        o_ref[...] = jnp.sum(jnp.where(bd, out, 0.0), axis=0, keepdims=True)


def _fox_caches(cache_k, cache_v, cache_lf):
    ck, cv = cache_k, cache_v
    clf = jnp.pad(jnp.transpose(cache_lf, (0, 1, 3, 2)), ((0, 0), (0, 0), (0, SUBLANES - N_HEADS), (0, 0)))
    return ck, cv, clf


def _fox_step(proj, ff, bf, caches, layer, page_table, anchor, *, nb, pps=8):
    ck, cv, clf = caches
    npages = page_table.shape[1]
    p3 = proj.reshape(nb, 1, proj.shape[1])
    ffc = jnp.pad(ff[:, :N_HEADS], ((0, 0), (0, SUBLANES - N_HEADS))).reshape(nb, SUBLANES, 1)
    bfc = jnp.pad(bf, (0, SUBLANES - N_HEADS)).reshape(SUBLANES, 1)
    row = lambda col: pl.BlockSpec((None, 1, GROUP_W), lambda b, j, pt: (b, 0, col))

    def page(shape, r):
        return pl.BlockSpec((None, None) + shape,
                            lambda b, j, pt: (layer, pt[b, npages - 1 - (j * pps + r)]) + (0,) * len(shape))

    in_specs = ([row(COL_FQ), row(COL_FK), row(COL_FV),
                 pl.BlockSpec((None, SUBLANES, 1), lambda b, j, pt: (b, 0, 0)),
                 pl.BlockSpec((SUBLANES, 1), lambda b, j, pt: (0, 0)),
                 pl.BlockSpec((SUBLANES, LANES), lambda b, j, pt: (0, 0))]
                + [page((PAGE_SIZE, N_HEADS, HEAD_D), r) for r in range(pps)]
                + [page((PAGE_SIZE, N_HEADS, HEAD_D), r) for r in range(pps)]
                + [page((SUBLANES, PAGE_SIZE), r) for r in range(pps)])
    y, logf = pl.pallas_call(
        functools.partial(_fox_step_kernel, pps=pps),
        grid_spec=pltpu.PrefetchScalarGridSpec(
            num_scalar_prefetch=1,
            grid=(nb, npages // pps),
            in_specs=in_specs,
            out_specs=[pl.BlockSpec((None, 1, GROUP_W), lambda b, j, pt: (b, 0, 0)),
                       pl.BlockSpec((None, SUBLANES, LANES), lambda b, j, pt: (b, 0, 0))],
            scratch_shapes=[pltpu.VMEM((SUBLANES, GROUP_W), BF16), pltpu.VMEM((SUBLANES, 1), F32),
                            pltpu.VMEM((SUBLANES, 1), F32), pltpu.VMEM((SUBLANES, GROUP_W), F32),
                            pltpu.VMEM((SUBLANES, 1), F32)]),
        out_shape=[jax.ShapeDtypeStruct((nb, 1, GROUP_W), F32), jax.ShapeDtypeStruct((nb, SUBLANES, LANES), F32)],
        compiler_params=_cparams(("arbitrary", "arbitrary"), 32),
        name="fox_step",
    )(page_table, p3, p3, p3, ffc, bfc, anchor, *([ck] * pps), *([cv] * pps), *([clf] * pps))
    return y.reshape(nb, GROUP_W), logf[:, :N_HEADS, 0]


N_TOP = PEER_TOPK + 1


def _top_values(s):
    rows = lax.broadcasted_iota(jnp.int32, (24, 1), 0)

    def rnd(r, carry):
        work, top = carry
        m = jnp.max(work, axis=0, keepdims=True)
        return jnp.where(work == m, NEG_INF, work), jnp.where(rows == r, m, top)

    _, top = lax.fori_loop(0, N_TOP, rnd, (s, jnp.full((24, s.shape[1]), NEG_INF, F32)))
    return top


def _peer_route_kernel(q_ref, sk_ref, s1_ref, s2_ref, tau_ref, *, tmr, t_total):
    live = lax.broadcasted_iota(jnp.int32, (tmr, 1), 0) < t_total - pl.program_id(0) * tmr
    q = jnp.where(live, q_ref[...], 0.0).astype(BF16)
    s1 = _dot_nt(sk_ref[0], q[:, :PEER_NKEYS])
    s2 = _dot_nt(sk_ref[1], q[:, PEER_NKEYS:])
    a, b = _top_values(s1), _top_values(s2)
    r24 = lax.broadcasted_iota(jnp.int32, (24, 1), 0)
    r8 = lax.broadcasted_iota(jnp.int32, (8, 1), 0)
    a8, b8 = a[0:8], b[0:8]
    slabs = [a[0:1] + b, jnp.where(r24 >= 1, a + b[0:1], NEG_INF)]
    for i in range(1, 5):
        slabs.append(jnp.where((r8 >= 1) & (r8 < N_TOP // (i + 1)), a[i:i + 1] + b8, NEG_INF))
    slabs.append(jnp.where(r8 >= 5, a8 + b[1:2], NEG_INF))
    cand = jnp.concatenate(slabs, axis=0)

    def rnd(_, carry):
        work, _, cur = carry
        m = jnp.max(work, axis=0, keepdims=True)
        return jnp.where(work == m, NEG_INF, work), cur, m

    init = jnp.max(cand, axis=0, keepdims=True)
    _, t16, t17 = lax.fori_loop(0, N_TOP, rnd, (cand, init, init))
    m0 = a[0:1] + b[0:1]
    z = jnp.sum(jnp.where(cand >= t16, jnp.exp(cand - m0), 0.0), axis=0, keepdims=True)
    log2z = jnp.log2(z)
    s1n = (s1 - a[0:1]) * LOG2E
    s2n = (s2 - b[0:1]) * LOG2E - log2z
    taun = (0.5 * (t16 + t17) - m0) * LOG2E - log2z
    for q in range(tmr // LANES):
        lanes = slice(q * LANES, (q + 1) * LANES)
        s1_ref[q] = s1n[:, lanes]
        s2_ref[q] = s2n[:, lanes]
        tau_ref[q] = jnp.broadcast_to(taun[:, lanes], (SUBLANES, LANES))


def _peer_route(qp, subkeys):
    t = qp.shape[0]
    tmr = 2 * LANES if t > LANES else LANES
    ntile = pl.cdiv(t, tmr)
    k = tmr // LANES
    sblk = pl.BlockSpec((None, k, PEER_NKEYS, LANES), lambda i, h: (h, i, 0, 0))
    return pl.pallas_call(
        functools.partial(_peer_route_kernel, tmr=tmr, t_total=t),
        grid=(ntile, PEER_HEADS),
        in_specs=[pl.BlockSpec((tmr, 2 * PEER_NKEYS), lambda i, h: (i, h)),
                  pl.BlockSpec((None, 2, PEER_NKEYS, PEER_NKEYS), lambda i, h: (h, 0, 0, 0))],
        out_specs=[sblk, sblk, pl.BlockSpec((None, k, SUBLANES, LANES), lambda i, h: (h, i, 0, 0))],
        out_shape=[jax.ShapeDtypeStruct((PEER_HEADS, ntile * k, PEER_NKEYS, LANES), F32)] * 2
        + [jax.ShapeDtypeStruct((PEER_HEADS, ntile * k, SUBLANES, LANES), F32)],
        compiler_params=_cparams(("arbitrary", "arbitrary"), 32),
        name="peer_route",
    )(qp, subkeys)


def _peer_dense_kernel(x_ref, h_ref, u_ref, v_ref, s1_ref, s2_ref, tau_ref, o_ref, xs_ref, xu_ref, ht_ref,
                       *, tm, et, t_total):
    j = pl.program_id(1)

    @pl.when(j == 0)
    def _():
        live = lax.broadcasted_iota(jnp.int32, (tm, 1), 0) < t_total - pl.program_id(0) * tm
        o_ref[...] = jnp.where(live, h_ref[...], 0.0)
        xs_ref[...] = jnp.where(live, x_ref[...], jnp.zeros((), BF16))

    xu_ref[...] = _dot_nt(u_ref[...], xs_ref[...])
    na = et // PEER_NKEYS
    for ap in range(na):
        a_glob = j * na + ap
        rows = slice(ap * PEER_NKEYS, (ap + 1) * PEER_NKEYS)
        for lt in range(tm // LANES):
            lanes = slice(lt * LANES, (lt + 1) * LANES)
            gate = jnp.zeros((PEER_NKEYS, LANES), F32)
            for h in range(PEER_HEADS):
                v2 = s2_ref[h, lt] + s1_ref[h, lt, pl.ds(a_glob, 1), :]
                gate = gate + jnp.exp2(jnp.where(v2 >= tau_ref[h, lt, 0:1, :], v2, NEG_INF))
            ht_ref[rows, lanes] = (_gelu(xu_ref[rows, lanes]) * gate).astype(BF16)
    o_ref[...] += _dot_tn(ht_ref[...], v_ref[...])


def _peer_dense(xn, h, u_tab, v_tab, layer, s1n, s2n, tau):
    t, d = h.shape
    ne = u_tab.shape[1]
    tm = 5 * LANES if t > 5 * LANES else LANES
    et = 4 * PEER_NKEYS
    k = tm // LANES
    rblk = lambda rows: pl.BlockSpec((PEER_HEADS, k, rows, LANES), lambda i, j: (0, i, 0, 0))
    tab = pl.BlockSpec((None, et, d), lambda i, j: (layer, j, 0))
    return pl.pallas_call(
        functools.partial(_peer_dense_kernel, tm=tm, et=et, t_total=t),
        grid=(pl.cdiv(t, tm), ne // et),
        in_specs=[pl.BlockSpec((tm, d), lambda i, j: (i, 0)), pl.BlockSpec((tm, d), lambda i, j: (i, 0)),
                  tab, tab, rblk(PEER_NKEYS), rblk(PEER_NKEYS), rblk(SUBLANES)],
        out_specs=pl.BlockSpec((tm, d), lambda i, j: (i, 0)),
        out_shape=jax.ShapeDtypeStruct((t, d), F32),
        scratch_shapes=[pltpu.VMEM((tm, d), BF16), pltpu.VMEM((et, tm), F32), pltpu.VMEM((et, tm), BF16)],
        compiler_params=_cparams(("arbitrary", "arbitrary"), 56),
        name="peer_dense",
    )(xn, h, u_tab, v_tab, s1n, s2n, tau)


def _peer(h, norm_g, wq, subkeys, u_tab, v_tab, layer):
    t = h.shape[0]
    if t < LANES:
        h = jnp.pad(h, ((0, LANES - t), (0, 0)))
    qp, xn = _q_proj(h, norm_g, wq)
    s1n, s2n, tau = _peer_route(qp, subkeys)
    out = _peer_dense(xn, h, u_tab, v_tab, layer, s1n, s2n, tau)
    return out[:t]


def kernel(x_prompt, x_sample, cache_fox_k, cache_fox_v, cache_fox_logf, page_table, state_s5_re, state_s5_im, state_ret, state_lru, state_conv, meta_tokens, norm_mix_g, w_in, w_out, s5_lam_re, s5_lam_im, s5_log_dt, s5_b_re, s5_b_im, s5_c_re, s5_c_im, s5_d, s5_glu_w, s5_glu_b, fox_bf, ret_norm_g, lru_conv_w, lru_conv_b, lru_wa, lru_ba, lru_wx, lru_bx, lru_lam, norm_ffn_g, peer_wq, peer_subkeys, peer_u, peer_v, norm_final_g):
    bp, seq_x, d = x_prompt.shape
    bs = x_sample.shape[0]
    depth = w_in.shape[0]
    seq = seq_x + N_META
    meta = jnp.broadcast_to(meta_tokens[None], (bp, N_META, d))
    h_p = jnp.concatenate([meta, x_prompt], axis=1).reshape(bp * seq, d)
    h_s = x_sample.reshape(bs, d)
    zeros = lambda *shape: jnp.zeros(shape, F32)
    u_tab, v_tab = peer_u.astype(BF16), peer_v.astype(BF16)
    caches = _fox_caches(cache_fox_k, cache_fox_v, cache_fox_logf)
    w_main, w_ff = _repack_w_in(w_in)
    outs_p, outs_s = [], []
    for l in range(depth):
        w_o = w_out[l].astype(BF16).reshape(4, GROUP_W, d)
        s5w = _s5_weights(s5_lam_re[l], s5_lam_im[l], s5_log_dt[l], s5_b_re[l], s5_b_im[l], s5_c_re[l], s5_c_im[l])
        glu_w = s5_glu_w[l].astype(BF16)
        lru_w = (lru_conv_w[l], lru_conv_b[l], lru_wa[l], lru_ba[l], lru_wx[l], lru_bx[l], lru_lam[l])
        wq = peer_wq[l].astype(BF16)
        subkeys = peer_subkeys[l].astype(BF16)

        proj, ff = _in_proj(h_p, norm_mix_g[l], w_main, w_ff, l)
        y_s5, s5re, s5im = _s5(proj, zeros(bp, S5_STATE_W), zeros(bp, S5_STATE_W), s5w, s5_d[l], glu_w, s5_glu_b[l],
                               nb=bp, seq=seq)
        logf, c_blk = _fox_prep(ff, fox_bf[l], nb=bp, seq=seq)
        y_fox = _fox_prompt(proj, c_blk, nb=bp, seq=seq)
        y_ret, ret_s = _ret_prompt(proj, ret_norm_g[l], nb=bp, seq=seq)
        y_lru, lru_h, conv_buf = _lru(proj, zeros(bp, GROUP_W), zeros(bp, CONV_W - 1, GROUP_W), *lru_w, nb=bp, seq=seq)
        h_p = _out_proj(h_p, (y_s5, y_fox, y_ret, y_lru), w_o)
        h_p = _peer(h_p, norm_ffn_g[l], wq, subkeys, u_tab, v_tab, l)
        p3 = proj.reshape(bp, seq, -1)
        outs_p.append((p3[:, :, COL_FK * GROUP_W:(COL_FK + 1) * GROUP_W].reshape(bp, seq, N_HEADS, HEAD_D),
                       p3[:, :, COL_FV * GROUP_W:(COL_FV + 1) * GROUP_W].reshape(bp, seq, N_HEADS, HEAD_D),
                       logf, s5re.reshape(bp, -1, 64), s5im.reshape(bp, -1, 64), ret_s, lru_h, conv_buf))

        proj, ff = _in_proj(h_s, norm_mix_g[l], w_main, w_ff, l)
        y_s5, s5re, s5im = _s5(proj, state_s5_re[l].reshape(bs, S5_STATE_W), state_s5_im[l].reshape(bs, S5_STATE_W),
                               s5w, s5_d[l], glu_w, s5_glu_b[l], nb=bs, seq=1)
        y_fox, logf = _fox_step(proj, ff, fox_bf[l], caches, l, page_table, h_p, nb=bs)
        y_ret, ret_s = _ret_step(proj, ret_norm_g[l], state_ret[l], nb=bs)
        y_lru, lru_h, conv_buf = _lru(proj, state_lru[l], state_conv[l], *lru_w, nb=bs, seq=1)
        h_s = _out_proj(h_s, (y_s5, y_fox, y_ret, y_lru), w_o)
        h_s = _peer(h_s, norm_ffn_g[l], wq, subkeys, u_tab, v_tab, l)
        outs_s.append((proj[:, COL_FK * GROUP_W:(COL_FK + 1) * GROUP_W].reshape(bs, 1, N_HEADS, HEAD_D),
                       proj[:, COL_FV * GROUP_W:(COL_FV + 1) * GROUP_W].reshape(bs, 1, N_HEADS, HEAD_D),
                       logf.reshape(bs, 1, N_HEADS), s5re.reshape(bs, -1, 64), s5im.reshape(bs, -1, 64),
                       ret_s, lru_h, conv_buf))

    y_prompt = _rmsnorm(h_p, norm_final_g).reshape(bp, seq, d)[:, N_META:]
    y_sample = _rmsnorm(h_s, norm_final_g).reshape(bs, 1, d)
    stk = lambda outs, j: jnp.stack([o[j] for o in outs], axis=0)
    return ((y_prompt, y_sample) + tuple(stk(outs_p, j) for j in range(8)) + tuple(stk(outs_s, j) for j in range(8)))
```

```python
import functools
import math

import jax
import jax.numpy as jnp
from jax import lax
from jax.experimental import pallas as pl
from jax.experimental.pallas import tpu as pltpu

F32 = jnp.float32
BF16 = jnp.bfloat16
NEG_INF = float("-inf")

EPS = 1e-6
N_META = 16
PAST_LEN = 16384
PAGE_SIZE = 128
GROUP_W = 512
HEAD_D = 128
N_HEADS = 4
S5_STATE_W = 2048
LRU_C = 8.0
CONV_W = 4
PEER_HEADS = 8
PEER_NKEYS = 128
PEER_TOPK = 16
LANES = 128
SUBLANES = 8
MXU_W = 256
LOG2E = 1.4426950408889634
MIB = 1024 * 1024

COL_S5, COL_FQ, COL_FK, COL_FV, COL_RQ, COL_RK, COL_RV, COL_RG, COL_LX, COL_LG = range(10)


def _cparams(sem, vmem_mib, **kw):
    return pltpu.CompilerParams(dimension_semantics=sem, vmem_limit_bytes=vmem_mib * MIB, **kw)


def _dot(a, b):
    return jnp.dot(a, b, preferred_element_type=F32)


def _dot_nt(a, b):
    return lax.dot_general(a, b, (((1,), (1,)), ((), ())), preferred_element_type=F32)


def _dot_tn(a, b):
    return lax.dot_general(a, b, (((0,), (0,)), ((), ())), preferred_element_type=F32)


def _dot_f32(a, b):
    return jnp.dot(a, b, preferred_element_type=F32, precision=lax.Precision.HIGHEST)


def _gelu(x):
    return 0.5 * x * (1.0 + jnp.tanh(0.7978845608028654 * (x + 0.044715 * (x * x * x))))


def _row_tile(n, target):
    if n <= target:
        return n
    best = None
    for t in range(SUBLANES, target + 1, SUBLANES):
        if n % t == 0:
            best = t
    assert best is not None, n
    return best


def _in_proj_kernel(x_ref, g_ref, w_ref, wff_ref, o_ref, off_ref, xn_ref):
    @pl.when(pl.program_id(1) == 0)
    def _():
        x = x_ref[...]
        ms = jnp.mean(x * x, axis=-1, keepdims=True)
        xn = (x * lax.rsqrt(ms + EPS) * g_ref[...]).astype(BF16)
        xn_ref[...] = xn
        off_ref[...] = _dot(xn, wff_ref[...])
    o_ref[...] = _dot(xn_ref[...], w_ref[...])


def _repack_w_in_kernel(w_ref, main_ref, ff_ref):
    g0 = 4 * GROUP_W
    w = w_ref[...]
    main_ref[:, :g0] = w[:, :g0].astype(BF16)
    main_ref[:, g0:] = w[:, g0 + N_HEADS:].astype(BF16)
    lane = lax.broadcasted_iota(jnp.int32, (w.shape[0], LANES), 1)
    ff_ref[...] = jnp.where(lane < N_HEADS, w[:, g0:g0 + LANES], 0.0).astype(BF16)


def _repack_w_in(w_in):
    depth, d, n = w_in.shape
    rows = 256
    return pl.pallas_call(
        _repack_w_in_kernel,
        grid=(depth, d // rows),
        in_specs=[pl.BlockSpec((None, rows, n), lambda l, r: (l, r, 0))],
        out_specs=[pl.BlockSpec((None, rows, n - N_HEADS), lambda l, r: (l, r, 0)),
                   pl.BlockSpec((None, rows, LANES), lambda l, r: (l, r, 0))],
        out_shape=[jax.ShapeDtypeStruct((depth, d, n - N_HEADS), BF16), jax.ShapeDtypeStruct((depth, d, LANES), BF16)],
        compiler_params=_cparams(("arbitrary", "arbitrary"), 32),
        name="repack_w_in",
    )(w_in)


def _in_proj(x, g, w_main, w_ff, layer):
    t, d = x.shape
    n = w_main.shape[2]
    tm, tn = _row_tile(t, 1032), 512
    return pl.pallas_call(
        _in_proj_kernel,
        grid=(t // tm, n // tn),
        in_specs=[pl.BlockSpec((tm, d), lambda i, j: (i, 0)),
                  pl.BlockSpec((1, d), lambda i, j: (0, 0)),
                  pl.BlockSpec((None, d, tn), lambda i, j: (layer, 0, j)),
                  pl.BlockSpec((None, d, LANES), lambda i, j: (layer, 0, 0))],
        out_specs=[pl.BlockSpec((tm, tn), lambda i, j: (i, j)),
                   pl.BlockSpec((tm, LANES), lambda i, j: (i, 0))],
        out_shape=[jax.ShapeDtypeStruct((t, n), F32), jax.ShapeDtypeStruct((t, LANES), F32)],
        scratch_shapes=[pltpu.VMEM((tm, d), BF16)],
        compiler_params=_cparams(("arbitrary", "arbitrary"), 48),
        name="in_proj",
    )(x, g.reshape(1, d), w_main, w_ff)


def _q_proj_kernel(x_ref, g_ref, w_ref, o_ref, xn_ref):
    @pl.when(pl.program_id(1) == 0)
    def _():
        x = x_ref[...]
        ms = jnp.mean(x * x, axis=-1, keepdims=True)
        xn_ref[...] = (x * lax.rsqrt(ms + EPS) * g_ref[...]).astype(BF16)
    o_ref[...] = _dot(xn_ref[...], w_ref[...])


def _q_proj(x, g, w):
    t, d = x.shape
    n = w.shape[1]
    tm, tn = _row_tile(t, 1032), 512
    return pl.pallas_call(
        _q_proj_kernel,
        grid=(t // tm, n // tn),
        in_specs=[pl.BlockSpec((tm, d), lambda i, j: (i, 0)),
                  pl.BlockSpec((1, d), lambda i, j: (0, 0)),
                  pl.BlockSpec((d, tn), lambda i, j: (0, j))],
        out_specs=[pl.BlockSpec((tm, tn), lambda i, j: (i, j)),
                   pl.BlockSpec((tm, d), lambda i, j: (i, 0))],
        out_shape=[jax.ShapeDtypeStruct((t, n), F32), jax.ShapeDtypeStruct((t, d), BF16)],
        compiler_params=_cparams(("arbitrary", "arbitrary"), 48),
        name="q_proj",
    )(x, g.reshape(1, d), w)


def _out_proj_kernel(h_ref, y0_ref, y1_ref, y2_ref, y3_ref, w_ref, o_ref):
    acc = h_ref[...]
    for gi, y_ref in enumerate((y0_ref, y1_ref, y2_ref, y3_ref)):
        acc = acc + _dot(y_ref[...].astype(BF16), w_ref[gi])
    o_ref[...] = acc


def _out_proj(h, ys, w):
    t, d = h.shape
    tm, tn = _row_tile(t, 1032), 512
    yspec = pl.BlockSpec((tm, GROUP_W), lambda i, j: (i, 0))
    return pl.pallas_call(
        _out_proj_kernel,
        grid=(t // tm, d // tn),
        in_specs=[pl.BlockSpec((tm, tn), lambda i, j: (i, j)), yspec, yspec, yspec, yspec,
                  pl.BlockSpec((4, GROUP_W, tn), lambda i, j: (0, 0, j))],
        out_specs=pl.BlockSpec((tm, tn), lambda i, j: (i, j)),
        out_shape=jax.ShapeDtypeStruct((t, d), F32),
        compiler_params=_cparams(("arbitrary", "arbitrary"), 48),
        name="out_proj",
    )(h, *ys, w)


def _rmsnorm_kernel(x_ref, g_ref, o_ref):
    x = x_ref[...]
    ms = jnp.mean(x * x, axis=-1, keepdims=True)
    o_ref[...] = x * lax.rsqrt(ms + EPS) * g_ref[...]


def _rmsnorm(x, g):
    t, d = x.shape
    tm = _row_tile(t, 1032)
    return pl.pallas_call(
        _rmsnorm_kernel,
        grid=(t // tm,),
        in_specs=[pl.BlockSpec((tm, d), lambda i: (i, 0)), pl.BlockSpec((1, d), lambda i: (0, 0))],
        out_specs=pl.BlockSpec((tm, d), lambda i: (i, 0)),
        out_shape=jax.ShapeDtypeStruct((t, d), F32),
        compiler_params=_cparams(("arbitrary",), 48),
        name="final_norm",
    )(x, g.reshape(1, d))


def _s5_kernel(u_ref, h0re_ref, h0im_ref, lre_ref, lim_ref, bw_ref, cw_ref, d_ref, gw_ref, gb_ref,
               y_ref, hre_ref, him_ref, sre_ref, sim_ref, *, nb, tc):
    single = tc == 1

    @pl.when(pl.program_id(0) == 0)
    def _():
        hre_ref[...] = h0re_ref[...]
        him_ref[...] = h0im_ref[...]

    def get_u(b):
        return u_ref[...] if single else u_ref[b]

    seqs = (0,) if single else tuple(range(nb))
    nrow = nb if single else tc
    for b in seqs:
        ub = get_u(b).astype(BF16)
        for r in range(4):
            bu = _dot(ub[:, r * LANES:(r + 1) * LANES], bw_ref[r])
            for q in range(4):
                sre_ref[4 * r + q, b * nrow:(b + 1) * nrow, :] = bu[:, q * LANES:(q + 1) * LANES]
                sim_ref[4 * r + q, b * nrow:(b + 1) * nrow, :] = bu[:, GROUP_W + q * LANES:GROUP_W + (q + 1) * LANES]

    for r in range(4):
        tiles = tuple(range(4 * r, 4 * r + 4))
        lanes = [slice(lt * LANES, (lt + 1) * LANES) for lt in tiles]
        lr = [jnp.broadcast_to(lre_ref[:, ln], (nb, LANES)) for ln in lanes]
        li = [jnp.broadcast_to(lim_ref[:, ln], (nb, LANES)) for ln in lanes]

        def body(t, carry, tiles=tiles, lr=lr, li=li):
            idx = pl.ds(t, nb, stride=tc)
            new = []
            for q, lt in enumerate(tiles):
                hr, hi = carry[q]
                nr = lr[q] * hr - li[q] * hi + sre_ref[lt, idx, :]
                ni = lr[q] * hi + li[q] * hr + sim_ref[lt, idx, :]
                sre_ref[lt, idx, :] = nr
                sim_ref[lt, idx, :] = ni
                new.append((nr, ni))
            return tuple(new)

        fin = lax.fori_loop(0, tc, body, tuple((hre_ref[:, ln], him_ref[:, ln]) for ln in lanes))
        for q, ln in enumerate(lanes):
            hre_ref[:, ln] = fin[q][0]
            him_ref[:, ln] = fin[q][1]

    for b in seqs:
        rows = slice(b * nrow, (b + 1) * nrow)
        ys = []
        for r in range(4):
            hre = jnp.concatenate([sre_ref[4 * r + q, rows, :] for q in range(4)], axis=1).astype(BF16)
            him = jnp.concatenate([sim_ref[4 * r + q, rows, :] for q in range(4)], axis=1).astype(BF16)
            ys.append(_dot(hre, cw_ref[r, :GROUP_W, :]) + _dot(him, cw_ref[r, GROUP_W:, :]))
        y = jnp.concatenate(ys, axis=1) + d_ref[...] * get_u(b)
        y = _gelu(y)
        z = _dot(y.astype(BF16), gw_ref[...]) + gb_ref[...]
        out = y * jax.nn.sigmoid(z)
        if single:
            y_ref[...] = out
        else:
            y_ref[b] = out


def _s5_weights(lam_re, lam_im, log_dt, b_re, b_im, c_re, c_im):
    lr, li = jnp.minimum(lam_re, -1e-4), lam_im
    dt = jnp.exp(log_dt)[:, None]
    mag = jnp.exp(lr * dt)
    bar_re, bar_im = mag * jnp.cos(li * dt), mag * jnp.sin(li * dt)
    den = lr * lr + li * li
    f_re = ((bar_re - 1.0) * lr + bar_im * li) / den
    f_im = (bar_im * lr - (bar_re - 1.0) * li) / den
    bb_re = f_re[:, :, None] * b_re - f_im[:, :, None] * b_im
    bb_im = f_re[:, :, None] * b_im + f_im[:, :, None] * b_re
    eye = jnp.eye(8, dtype=F32)

    def pack_b(m):
        m = m.reshape(4, 8, 64, 16)
        return jnp.einsum("rgpc,gh->rgchp", m, eye).reshape(4, LANES, GROUP_W)

    def pack_c(m):
        m = m.reshape(4, 8, 16, 64)
        return jnp.einsum("rgcp,hg->rhpgc", m, eye).reshape(4, GROUP_W, LANES)

    bw = jnp.concatenate([pack_b(bb_re), pack_b(bb_im)], axis=2).astype(BF16)
    cw = jnp.concatenate([pack_c(c_re), -pack_c(c_im)], axis=1).astype(BF16)
    return (bar_re.reshape(1, S5_STATE_W), bar_im.reshape(1, S5_STATE_W), bw, cw)


def _s5(proj, h0re, h0im, wts, d, glu_w, glu_b, *, nb, seq):
    lre, lim, bw, cw = wts
    single = seq == 1
    tc = 1 if single else _row_tile(seq, 344)
    nchunk = seq // tc
    full = lambda shape: pl.BlockSpec(shape, lambda c: tuple(0 for _ in shape))
    if single:
        u_in = proj
        u_spec = pl.BlockSpec((nb, GROUP_W), lambda c: (0, COL_S5))
        y_spec = pl.BlockSpec((nb, GROUP_W), lambda c: (0, 0))
        y_shape = jax.ShapeDtypeStruct((nb, GROUP_W), F32)
    else:
        u_in = proj.reshape(nb, seq, proj.shape[1])
        u_spec = pl.BlockSpec((nb, tc, GROUP_W), lambda c: (0, c, COL_S5))
        y_spec = pl.BlockSpec((nb, tc, GROUP_W), lambda c: (0, c, 0))
        y_shape = jax.ShapeDtypeStruct((nb, seq, GROUP_W), F32)
    y, hre, him = pl.pallas_call(
        functools.partial(_s5_kernel, nb=nb, tc=tc),
        grid=(nchunk,),
        in_specs=[u_spec, full((nb, S5_STATE_W)), full((nb, S5_STATE_W)), full((1, S5_STATE_W)),
                  full((1, S5_STATE_W)), full((4, LANES, 2 * GROUP_W)), full((4, 2 * GROUP_W, LANES)),
                  full((1, GROUP_W)), full((GROUP_W, GROUP_W)), full((1, GROUP_W))],
        out_specs=[y_spec, full((nb, S5_STATE_W)), full((nb, S5_STATE_W))],
        out_shape=[y_shape, jax.ShapeDtypeStruct((nb, S5_STATE_W), F32),
                   jax.ShapeDtypeStruct((nb, S5_STATE_W), F32)],
        scratch_shapes=[pltpu.VMEM((S5_STATE_W // LANES, nb * tc, LANES), F32)] * 2,
        compiler_params=_cparams(("arbitrary",), 56),
        name="s5",
    )(u_in, h0re, h0im, lre, lim, bw, cw, d.reshape(1, GROUP_W), glu_w, glu_b.reshape(1, GROUP_W))
    return y.reshape(nb * seq, GROUP_W), hre, him


def _softplus(x):
    return jnp.maximum(x, 0.0) + jnp.log1p(jnp.exp(-jnp.abs(x)))


def _lru_gates(conv, wa_ref, wx_ref, ba_ref, bx_ref, lam_ref):
    cb = conv.astype(BF16)
    r = jax.nn.sigmoid(_dot(cb, wa_ref[...]) + ba_ref[...])
    i = jax.nn.sigmoid(_dot(cb, wx_ref[...]) + bx_ref[...])
    log_a = -LRU_C * r * _softplus(-lam_ref[...])
    a = jnp.exp(log_a)
    return a, jnp.sqrt(-jnp.tanh(log_a) * (a * a + 1.0)) * (i * conv)


def _lru_kernel(x_ref, g_ref, h0_ref, buf0_ref, cw_ref, cb_ref, wa_ref, ba_ref, wx_ref, bx_ref, lam_ref,
                y_ref, h_ref, buf_ref, xe_ref, sa_ref, sx_ref, *, nb, tc):
    nq = GROUP_W // LANES

    @pl.when(pl.program_id(0) == 0)
    def _():
        h_ref[...] = h0_ref[...]
        for b in range(nb):
            xe_ref[b, 5:8, :] = buf0_ref[b]

    for b in range(nb):
        xe_ref[b, 8:8 + tc, :] = x_ref[b]
    for b in range(nb):
        conv = cb_ref[...] + sum(xe_ref[b, 5 + j:5 + j + tc, :] * cw_ref[j:j + 1, :] for j in range(CONV_W))
        a, xin = _lru_gates(conv, wa_ref, wx_ref, ba_ref, bx_ref, lam_ref)
        for q in range(nq):
            sa_ref[q, b * tc:(b + 1) * tc, :] = a[:, q * LANES:(q + 1) * LANES]
            sx_ref[q, b * tc:(b + 1) * tc, :] = xin[:, q * LANES:(q + 1) * LANES]

    def body(t, hs):
        idx = pl.ds(t, nb, stride=tc)
        new = []
        for q in range(nq):
            h = sa_ref[q, idx, :] * hs[q] + sx_ref[q, idx, :]
            sx_ref[q, idx, :] = h
            new.append(h)
        return tuple(new)

    fin = lax.fori_loop(0, tc, body, tuple(h_ref[:, q * LANES:(q + 1) * LANES] for q in range(nq)))
    for q in range(nq):
        h_ref[:, q * LANES:(q + 1) * LANES] = fin[q]
    for b in range(nb):
        hseq = jnp.concatenate([sx_ref[q, b * tc:(b + 1) * tc, :] for q in range(nq)], axis=1)
        y_ref[b] = hseq * _gelu(g_ref[b])
        tail = xe_ref[b, tc + 5:tc + 8, :]
        xe_ref[b, 5:8, :] = tail
        buf_ref[b] = tail


def _lru_step_kernel(x_ref, g_ref, h0_ref, buf0_ref, cw_ref, cb_ref, wa_ref, ba_ref, wx_ref, bx_ref, lam_ref,
                     y_ref, h_ref, buf_ref):
    x = x_ref[...]
    conv = cb_ref[...] + x * cw_ref[3:4, :] + sum(buf0_ref[j] * cw_ref[j:j + 1, :] for j in range(CONV_W - 1))
    a, xin = _lru_gates(conv, wa_ref, wx_ref, ba_ref, bx_ref, lam_ref)
    h = a * h0_ref[...] + xin
    h_ref[...] = h
    y_ref[...] = h * _gelu(g_ref[...])
    buf_ref[0] = buf0_ref[1]
    buf_ref[1] = buf0_ref[2]
    buf_ref[2] = x


def _block_diag(w):
    return jnp.einsum("hij,hg->higj", w, jnp.eye(8, dtype=w.dtype)).reshape(GROUP_W, GROUP_W)


def _lru(proj, h0, buf0, conv_w, conv_b, wa, ba, wx, bx, lam, *, nb, seq):
    single = seq == 1
    row = lambda v: v.reshape(1, GROUP_W)
    wts = (conv_w, row(conv_b), _block_diag(wa).astype(BF16), row(ba), _block_diag(wx).astype(BF16), row(bx), row(lam))
    full = lambda shape: pl.BlockSpec(shape, lambda c: tuple(0 for _ in shape))
    wspecs = [full((CONV_W, GROUP_W)), full((1, GROUP_W)), full((GROUP_W, GROUP_W)), full((1, GROUP_W)),
              full((GROUP_W, GROUP_W)), full((1, GROUP_W)), full((1, GROUP_W))]
    if single:
        y, h, buf = pl.pallas_call(
            _lru_step_kernel,
            grid=(1,),
            in_specs=[pl.BlockSpec((nb, GROUP_W), lambda c: (0, COL_LX)), pl.BlockSpec((nb, GROUP_W), lambda c: (0, COL_LG)),
                      full((nb, GROUP_W)), full((CONV_W - 1, nb, GROUP_W))] + wspecs,
            out_specs=[full((nb, GROUP_W)), full((nb, GROUP_W)), full((CONV_W - 1, nb, GROUP_W))],
            out_shape=[jax.ShapeDtypeStruct((nb, GROUP_W), F32), jax.ShapeDtypeStruct((nb, GROUP_W), F32),
                       jax.ShapeDtypeStruct((CONV_W - 1, nb, GROUP_W), F32)],
            compiler_params=_cparams(("arbitrary",), 32),
            name="lru_step",
        )(proj, proj, h0, jnp.transpose(buf0, (1, 0, 2)), *wts)
        return y, h, jnp.transpose(buf, (1, 0, 2))
    tc = _row_tile(seq, 344)
    p3 = proj.reshape(nb, seq, proj.shape[1])
    y, h, buf = pl.pallas_call(
        functools.partial(_lru_kernel, nb=nb, tc=tc),
        grid=(seq // tc,),
        in_specs=[pl.BlockSpec((nb, tc, GROUP_W), lambda c: (0, c, COL_LX)),
                  pl.BlockSpec((nb, tc, GROUP_W), lambda c: (0, c, COL_LG)),
                  full((nb, GROUP_W)), full((nb, CONV_W - 1, GROUP_W))] + wspecs,
        out_specs=[pl.BlockSpec((nb, tc, GROUP_W), lambda c: (0, c, 0)), full((nb, GROUP_W)),
                   full((nb, CONV_W - 1, GROUP_W))],
        out_shape=[jax.ShapeDtypeStruct((nb, seq, GROUP_W), F32), jax.ShapeDtypeStruct((nb, GROUP_W), F32),
                   jax.ShapeDtypeStruct((nb, CONV_W - 1, GROUP_W), F32)],
        scratch_shapes=[pltpu.VMEM((nb, tc + 8, GROUP_W), F32), pltpu.VMEM((GROUP_W // LANES, nb * tc, LANES), F32),
                        pltpu.VMEM((GROUP_W // LANES, nb * tc, LANES), F32)],
        compiler_params=_cparams(("arbitrary",), 48),
        name="lru",
    )(p3, p3, h0, buf0, *wts)
    return y.reshape(nb * seq, GROUP_W), h, buf


def _ret_log_gamma(h):
    return math.log(1.0 - 2.0 ** (-5.0 - h))


def _rotary_tables(pos):
    half = HEAD_D // 2
    inv = 1.0 / (10000.0 ** (jnp.arange(half, dtype=F32) / half))
    ang = pos.astype(F32)[:, None] * inv[None, :]
    cos, sin = jnp.cos(ang), jnp.sin(ang)
    return jnp.concatenate([cos, cos], axis=1), jnp.concatenate([-sin, sin], axis=1)


def _rotate(x, cos2, sin2):
    return x * cos2 + pltpu.roll(x, HEAD_D // 2, 1) * sin2


def _ret_kernel(q_ref, k_ref, v_ref, g_ref, cos_ref, sin_ref, ng_ref, y_ref, s_ref, *, seq):
    c = pl.program_id(1)

    @pl.when(c == 0)
    def _():
        s_ref[...] = jnp.zeros_like(s_ref)

    nv = jnp.minimum(LANES, seq - c * LANES)
    nvf = nv.astype(F32)
    row = lax.broadcasted_iota(jnp.int32, (LANES, 1), 0)
    rowf = row.astype(F32)
    valid = row < nv
    diff = (lax.broadcasted_iota(jnp.int32, (LANES, LANES), 0)
            - lax.broadcasted_iota(jnp.int32, (LANES, LANES), 1)).astype(F32)
    cos2, sin2 = cos_ref[...], sin_ref[...]
    for h in range(N_HEADS):
        lg = _ret_log_gamma(h)
        cols = slice(h * HEAD_D, (h + 1) * HEAD_D)
        decay = jnp.exp(jnp.where(diff >= 0, diff * lg, NEG_INF))
        q_dec = jnp.exp((rowf + 1.0) * lg)
        k_dec = jnp.exp((nvf - 1.0 - rowf) * lg)
        c_dec = jnp.exp(jnp.full((1, 1), lg, F32) * nvf)
        q = jnp.where(valid, q_ref[:, cols], 0.0)
        k = jnp.where(valid, k_ref[:, cols], 0.0)
        v = jnp.where(valid, v_ref[:, cols], 0.0)
        qr = _rotate(q, cos2, sin2)
        kr = _rotate(k, cos2, sin2) * (HEAD_D ** -0.5)
        vb = v.astype(BF16)
        att = _dot_nt(qr.astype(BF16), kr.astype(BF16)) * decay
        s_old = s_ref[h]
        o = _dot(att.astype(BF16), vb) + _dot((qr * q_dec).astype(BF16), s_old.astype(BF16))
        s_ref[h] = s_old * c_dec + _dot_tn((kr * k_dec).astype(BF16), vb)
        o = o * lax.rsqrt(jnp.mean(o * o, axis=-1, keepdims=True) + EPS) * ng_ref[:, cols]
        g = jnp.where(valid, g_ref[:, cols], 0.0)
        y_ref[:, cols] = o * (g * jax.nn.sigmoid(g))


def _ret_prompt(proj, norm_g, *, nb, seq):
    p3 = proj.reshape(nb, seq, proj.shape[1])
    nchunk = pl.cdiv(seq, LANES)
    cos2, sin2 = _rotary_tables(jnp.arange(nchunk * LANES))
    blk = lambda col: pl.BlockSpec((None, LANES, GROUP_W), lambda b, c: (b, c, col))
    tab = pl.BlockSpec((LANES, HEAD_D), lambda b, c: (c, 0))
    y, s = pl.pallas_call(
        functools.partial(_ret_kernel, seq=seq),
        grid=(nb, nchunk),
        in_specs=[blk(COL_RQ), blk(COL_RK), blk(COL_RV), blk(COL_RG), tab, tab,
                  pl.BlockSpec((1, GROUP_W), lambda b, c: (0, 0))],
        out_specs=[pl.BlockSpec((None, LANES, GROUP_W), lambda b, c: (b, c, 0)),
                   pl.BlockSpec((None, N_HEADS, HEAD_D, HEAD_D), lambda b, c: (b, 0, 0, 0))],
        out_shape=[jax.ShapeDtypeStruct((nb, seq, GROUP_W), F32),
                   jax.ShapeDtypeStruct((nb, N_HEADS, HEAD_D, HEAD_D), F32)],
        compiler_params=_cparams(("arbitrary", "arbitrary"), 32),
        name="retention",
    )(p3, p3, p3, p3, cos2, sin2, norm_g.reshape(1, GROUP_W))
    return y.reshape(nb * seq, GROUP_W), s


def _ret_step_kernel(q_ref, k_ref, v_ref, g_ref, cos_ref, sin_ref, ng_ref, s0_ref, y_ref, s_ref):
    cos2, sin2 = cos_ref[...], sin_ref[...]
    eye = (lax.broadcasted_iota(jnp.int32, (HEAD_D, HEAD_D), 0)
           == lax.broadcasted_iota(jnp.int32, (HEAD_D, HEAD_D), 1))
    for h in range(N_HEADS):
        gamma = 1.0 - 2.0 ** (-5.0 - h)
        cols = slice(h * HEAD_D, (h + 1) * HEAD_D)
        qr = _rotate(q_ref[:, cols], cos2, sin2)
        kr = _rotate(k_ref[:, cols], cos2, sin2) * (HEAD_D ** -0.5)
        v = v_ref[:, cols]
        s0 = s0_ref[h]
        qs = _dot_f32(jnp.broadcast_to(qr, (SUBLANES, HEAD_D)), s0)[0:1]
        o = jnp.sum(qr * kr, axis=-1, keepdims=True) * v + gamma * qs
        kcol = jnp.sum(jnp.where(eye, jnp.broadcast_to(kr, (HEAD_D, HEAD_D)), 0.0), axis=1, keepdims=True)
        s_ref[h] = gamma * s0 + kcol * v
        o = o * lax.rsqrt(jnp.mean(o * o, axis=-1, keepdims=True) + EPS) * ng_ref[:, cols]
        g = g_ref[:, cols]
        y_ref[:, cols] = o * (g * jax.nn.sigmoid(g))


def _ret_step(proj, norm_g, s0, *, nb):
    p3 = proj.reshape(nb, 1, proj.shape[1])
    cos2, sin2 = _rotary_tables(jnp.full((1,), PAST_LEN))
    blk = lambda col: pl.BlockSpec((None, 1, GROUP_W), lambda b: (b, 0, col))
    one = lambda n: pl.BlockSpec((1, n), lambda b: (0, 0))
    st = pl.BlockSpec((None, N_HEADS, HEAD_D, HEAD_D), lambda b: (b, 0, 0, 0))
    y, s = pl.pallas_call(
        _ret_step_kernel,
        grid=(nb,),
        in_specs=[blk(COL_RQ), blk(COL_RK), blk(COL_RV), blk(COL_RG), one(HEAD_D), one(HEAD_D), one(GROUP_W), st],
        out_specs=[pl.BlockSpec((None, 1, GROUP_W), lambda b: (b, 0, 0)), st],
        out_shape=[jax.ShapeDtypeStruct((nb, 1, GROUP_W), F32),
                   jax.ShapeDtypeStruct((nb, N_HEADS, HEAD_D, HEAD_D), F32)],
        compiler_params=_cparams(("arbitrary",), 32),
        name="retention_step",
    )(p3, p3, p3, p3, cos2, sin2, norm_g.reshape(1, GROUP_W), s0)
    return y.reshape(nb, GROUP_W), s


def _log_sigmoid(x):
    return jnp.minimum(x, 0.0) - jnp.log1p(jnp.exp(-jnp.abs(x)))


def _fox_prep_kernel(ff_ref, bf_ref, logf_ref, c_ref):
    logf = _log_sigmoid(ff_ref[...] + bf_ref[...])
    logf_ref[...] = logf
    tri = (lax.broadcasted_iota(jnp.int32, (LANES, LANES), 0)
           <= lax.broadcasted_iota(jnp.int32, (LANES, LANES), 1)).astype(F32)
    carry = jnp.zeros((SUBLANES, 1), F32)
    for j in range(logf.shape[1] // LANES):
        blk = logf[:, j * LANES:(j + 1) * LANES]
        c_ref[:, j * LANES:(j + 1) * LANES] = carry + _dot_f32(blk, tri)
        carry = carry + jnp.sum(blk, axis=1, keepdims=True)


FOX_TK = 512
FOX_TQ = 344


def _fox_prep(ff, bf, *, nb, seq):
    nblk = pl.cdiv(seq, FOX_TK)
    lp = nblk * FOX_TK
    fft = jnp.transpose(ff[:, :N_HEADS].reshape(nb, seq, N_HEADS), (0, 2, 1))
    fft = jnp.pad(fft, ((0, 0), (0, SUBLANES - N_HEADS), (0, lp - seq)))
    bfc = jnp.pad(bf, (0, SUBLANES - N_HEADS)).reshape(SUBLANES, 1)
    spec = pl.BlockSpec((None, SUBLANES, lp), lambda b: (b, 0, 0))
    logf_t, c_t = pl.pallas_call(
        _fox_prep_kernel,
        grid=(nb,),
        in_specs=[spec, pl.BlockSpec((SUBLANES, 1), lambda b: (0, 0))],
        out_specs=[spec, spec],
        out_shape=[jax.ShapeDtypeStruct((nb, SUBLANES, lp), F32)] * 2,
        compiler_params=_cparams(("arbitrary",), 56),
        name="fox_prep",
    )(fft, bfc)
    logf = jnp.transpose(logf_t[:, :N_HEADS, :seq], (0, 2, 1))
    c_blk = jnp.transpose(c_t.reshape(nb, SUBLANES, nblk, FOX_TK), (0, 2, 1, 3))
    return logf, c_blk


def _softmax_update(carry, s, vb):
    m, l, acc = carry
    mn = jnp.maximum(m, jnp.max(s, axis=-1, keepdims=True))
    p = jnp.exp(s - mn)
    al = jnp.exp(m - mn)
    return mn, al * l + jnp.sum(p, axis=-1, keepdims=True), al * acc + _dot(p.astype(BF16), vb)


def _fox_attn_kernel(q_ref, k_ref, v_ref, c_ref, o_ref, kb_ref, vb_ref, m_ref, l_ref, acc_ref, *, seq):
    i = pl.program_id(1)
    tk = FOX_TK
    nfull = seq // tk
    rem = seq - nfull * tk
    scale = HEAD_D ** -0.5

    @pl.when(i == 0)
    def _():
        kb_ref[...] = k_ref[...].astype(BF16)
        vb_ref[...] = v_ref[...].astype(BF16)

    m_ref[...] = jnp.full_like(m_ref, NEG_INF)
    l_ref[...] = jnp.zeros_like(l_ref)
    acc_ref[...] = jnp.zeros_like(acc_ref)
    tq = o_ref.shape[0]
    qpos = i * tq + lax.broadcasted_iota(jnp.int32, (tq, 1), 0)
    head_cols = [slice(h * HEAD_D, (h + 1) * HEAD_D) for h in range(N_HEADS)]
    live = qpos < seq
    qbs = [jnp.where(live, q_ref[:, cols], 0.0).astype(BF16) for cols in head_cols]

    def tile(rows, cj, kpos):
        mask = kpos <= qpos
        heads = range(N_HEADS)
        ss = [jnp.where(mask, _dot_nt(qbs[h], kb_ref[rows, head_cols[h]]) * scale - cj[h:h + 1, :], NEG_INF)
              for h in heads]
        ms = [m_ref[h] for h in heads]
        mns = [jnp.maximum(ms[h], jnp.max(ss[h], axis=-1, keepdims=True)) for h in heads]
        ps = [jnp.exp(ss[h] - mns[h]) for h in heads]
        als = [jnp.exp(ms[h] - mns[h]) for h in heads]
        pvs = [_dot(ps[h].astype(BF16), vb_ref[rows, head_cols[h]]) for h in heads]
        for h in heads:
            m_ref[h] = mns[h]
            l_ref[h] = als[h] * l_ref[h] + jnp.sum(ps[h], axis=-1, keepdims=True)
            acc_ref[h] = als[h] * acc_ref[h] + pvs[h]

    def body(j, carry):
        r0 = pl.multiple_of(j * tk, tk)
        tile(pl.ds(r0, tk), c_ref[j], j * tk + lax.broadcasted_iota(jnp.int32, (1, tk), 1))
        return carry

    lax.fori_loop(0, jnp.minimum((i * tq + tq + tk - 1) // tk, nfull), body, 0)
    if rem:
        @pl.when(i * tq + tq > nfull * tk)
        def _():
            tile(slice(nfull * tk, seq), c_ref[nfull][:, 0:rem],
                 nfull * tk + lax.broadcasted_iota(jnp.int32, (1, rem), 1))
    for h, cols in enumerate(head_cols):
        o_ref[:, cols] = acc_ref[h] / l_ref[h]


def _fox_prompt(proj, c_blk, *, nb, seq):
    p3 = proj.reshape(nb, seq, proj.shape[1])
    nkt = c_blk.shape[1]
    tq = _row_tile(seq, FOX_TQ)
    kv = lambda col: pl.BlockSpec((None, seq, GROUP_W), lambda b, i: (b, 0, col))
    y = pl.pallas_call(
        functools.partial(_fox_attn_kernel, seq=seq),
        grid=(nb, seq // tq),
        in_specs=[pl.BlockSpec((None, tq, GROUP_W), lambda b, i: (b, i, COL_FQ)), kv(COL_FK), kv(COL_FV),
                  pl.BlockSpec((None, nkt, SUBLANES, FOX_TK), lambda b, i: (b, 0, 0, 0))],
        out_specs=pl.BlockSpec((None, tq, GROUP_W), lambda b, i: (b, i, 0)),
        out_shape=jax.ShapeDtypeStruct((nb, seq, GROUP_W), F32),
        scratch_shapes=[pltpu.VMEM((seq, GROUP_W), BF16), pltpu.VMEM((seq, GROUP_W), BF16),
                        pltpu.VMEM((N_HEADS, tq, 1), F32), pltpu.VMEM((N_HEADS, tq, 1), F32),
                        pltpu.VMEM((N_HEADS, tq, HEAD_D), F32)],
        compiler_params=_cparams(("arbitrary", "arbitrary"), 48),
        name="fox_attn",
    )(p3, p3, p3, c_blk)
    return y.reshape(nb * seq, GROUP_W)


def _fox_step_kernel(pt_ref, q_ref, kn_ref, vn_ref, ff_ref, bf_ref, anchor_ref, *rest, pps):
    del anchor_ref
    k_refs, v_refs, lf_refs = rest[:pps], rest[pps:2 * pps], rest[2 * pps:3 * pps]
    o_ref, logf_ref, qbd_ref, m_ref, l_ref, acc_ref, carry_ref = rest[3 * pps:]
    j = pl.program_id(1)
    scale = HEAD_D ** -0.5
    head_of_col = lax.broadcasted_iota(jnp.int32, (SUBLANES, GROUP_W), 1) // HEAD_D
    bd = head_of_col == lax.broadcasted_iota(jnp.int32, (SUBLANES, GROUP_W), 0)

    @pl.when(j == 0)
    def _():
        qbd_ref[...] = jnp.where(bd, jnp.broadcast_to(q_ref[...], (SUBLANES, GROUP_W)), 0.0).astype(BF16)
        m_ref[...] = jnp.full_like(m_ref, NEG_INF)
        l_ref[...] = jnp.zeros_like(l_ref)
        acc_ref[...] = jnp.zeros_like(acc_ref)
        carry_ref[...] = jnp.zeros_like(carry_ref)

    later = (lax.broadcasted_iota(jnp.int32, (PAGE_SIZE, PAGE_SIZE), 0)
             > lax.broadcasted_iota(jnp.int32, (PAGE_SIZE, PAGE_SIZE), 1)).astype(F32)
    def page(ref):
        flat = ref.reshape(PAGE_SIZE * N_HEADS, HEAD_D)
        return jnp.concatenate([flat[pl.ds(h, PAGE_SIZE, stride=N_HEADS), :] for h in range(N_HEADS)],
                               axis=1).astype(BF16)

    qbd = qbd_ref[...]
    run = carry_ref[...]
    lps = [lf_refs[r][...] for r in range(pps)]
    suffix = _dot_f32(jnp.concatenate(lps, axis=0), later)
    scores = []
    for r in range(pps):
        bias = run + suffix[r * SUBLANES:(r + 1) * SUBLANES]
        run = run + jnp.sum(lps[r], axis=1, keepdims=True)
        scores.append(_dot_nt(qbd, page(k_refs[r])) * scale + bias)
    carry_ref[...] = run
    m = m_ref[...]
    mn = m
    for s in scores:
        mn = jnp.maximum(mn, jnp.max(s, axis=-1, keepdims=True))
    al = jnp.exp(m - mn)
    l = al * l_ref[...]
    acc = al * acc_ref[...]
    for r, s in enumerate(scores):
        p = jnp.exp(s - mn)
        l = l + jnp.sum(p, axis=-1, keepdims=True)
        acc = acc + _dot(p.astype(BF16), page(v_refs[r]))
    m_ref[...], l_ref[...], acc_ref[...] = mn, l, acc

    @pl.when(j == pl.num_programs(1) - 1)
    def _():
        logf = _log_sigmoid(ff_ref[...] + bf_ref[...])
        logf_ref[...] = jnp.broadcast_to(logf, (SUBLANES, LANES))
        qf = qbd_ref[...].astype(F32)
        kn = kn_ref[...].astype(BF16).astype(F32)
        s_new = jnp.sum(qf * kn, axis=1, keepdims=True) * scale - logf
        m, l, acc = m_ref[...], l_ref[...], acc_ref[...]
        mn = jnp.maximum(m, s_new)
        p = jnp.exp(s_new - mn)
        al = jnp.exp(m - mn)
        out = (al * acc + p * vn_ref[...]) / (al * l + p)
        o_ref[...] = jnp.sum(jnp.where(bd, out, 0.0), axis=0, keepdims=True)


def _fox_caches(cache_k, cache_v, cache_lf):
    ck, cv = cache_k, cache_v
    clf = jnp.pad(jnp.transpose(cache_lf, (0, 1, 3, 2)), ((0, 0), (0, 0), (0, SUBLANES - N_HEADS), (0, 0)))
    return ck, cv, clf


def _fox_step(proj, ff, bf, caches, layer, page_table, anchor, *, nb, pps=8):
    ck, cv, clf = caches
    npages = page_table.shape[1]
    p3 = proj.reshape(nb, 1, proj.shape[1])
    ffc = jnp.pad(ff[:, :N_HEADS], ((0, 0), (0, SUBLANES - N_HEADS))).reshape(nb, SUBLANES, 1)
    bfc = jnp.pad(bf, (0, SUBLANES - N_HEADS)).reshape(SUBLANES, 1)
    row = lambda col: pl.BlockSpec((None, 1, GROUP_W), lambda b, j, pt: (b, 0, col))

    def page(shape, r):
        return pl.BlockSpec((None, None) + shape,
                            lambda b, j, pt: (layer, pt[b, npages - 1 - (j * pps + r)]) + (0,) * len(shape))

    in_specs = ([row(COL_FQ), row(COL_FK), row(COL_FV),
                 pl.BlockSpec((None, SUBLANES, 1), lambda b, j, pt: (b, 0, 0)),
                 pl.BlockSpec((SUBLANES, 1), lambda b, j, pt: (0, 0)),
                 pl.BlockSpec((SUBLANES, LANES), lambda b, j, pt: (0, 0))]
                + [page((PAGE_SIZE, N_HEADS, HEAD_D), r) for r in range(pps)]
                + [page((PAGE_SIZE, N_HEADS, HEAD_D), r) for r in range(pps)]
                + [page((SUBLANES, PAGE_SIZE), r) for r in range(pps)])
    y, logf = pl.pallas_call(
        functools.partial(_fox_step_kernel, pps=pps),
        grid_spec=pltpu.PrefetchScalarGridSpec(
            num_scalar_prefetch=1,
            grid=(nb, npages // pps),
            in_specs=in_specs,
            out_specs=[pl.BlockSpec((None, 1, GROUP_W), lambda b, j, pt: (b, 0, 0)),
                       pl.BlockSpec((None, SUBLANES, LANES), lambda b, j, pt: (b, 0, 0))],
            scratch_shapes=[pltpu.VMEM((SUBLANES, GROUP_W), BF16), pltpu.VMEM((SUBLANES, 1), F32),
                            pltpu.VMEM((SUBLANES, 1), F32), pltpu.VMEM((SUBLANES, GROUP_W), F32),
                            pltpu.VMEM((SUBLANES, 1), F32)]),
        out_shape=[jax.ShapeDtypeStruct((nb, 1, GROUP_W), F32), jax.ShapeDtypeStruct((nb, SUBLANES, LANES), F32)],
        compiler_params=_cparams(("arbitrary", "arbitrary"), 32),
        name="fox_step",
    )(page_table, p3, p3, p3, ffc, bfc, anchor, *([ck] * pps), *([cv] * pps), *([clf] * pps))
    return y.reshape(nb, GROUP_W), logf[:, :N_HEADS, 0]


N_TOP = PEER_TOPK + 1


def _top_values(s):
    rows = lax.broadcasted_iota(jnp.int32, (24, 1), 0)

    def rnd(r, carry):
        work, top = carry
        m = jnp.max(work, axis=0, keepdims=True)
        return jnp.where(work == m, NEG_INF, work), jnp.where(rows == r, m, top)

    _, top = lax.fori_loop(0, N_TOP, rnd, (s, jnp.full((24, s.shape[1]), NEG_INF, F32)))
    return top


def _peer_route_kernel(q_ref, sk_ref, s1_ref, s2_ref, tau_ref, *, tmr, t_total):
    live = lax.broadcasted_iota(jnp.int32, (tmr, 1), 0) < t_total - pl.program_id(0) * tmr
    for h in range(PEER_HEADS):
        q = jnp.where(live, q_ref[:, h * 2 * PEER_NKEYS:(h + 1) * 2 * PEER_NKEYS], 0.0).astype(BF16)
        _peer_route_head(q, sk_ref.at[h], s1_ref.at[h], s2_ref.at[h], tau_ref.at[h], tmr)


def _peer_route_head(q, sk_ref, s1_ref, s2_ref, tau_ref, tmr):
    s1 = _dot_nt(sk_ref[0], q[:, :PEER_NKEYS])
    s2 = _dot_nt(sk_ref[1], q[:, PEER_NKEYS:])
    a, b = _top_values(s1), _top_values(s2)
    r24 = lax.broadcasted_iota(jnp.int32, (24, 1), 0)
    r8 = lax.broadcasted_iota(jnp.int32, (8, 1), 0)
    a8, b8 = a[0:8], b[0:8]
    slabs = [a[0:1] + b, jnp.where(r24 >= 1, a + b[0:1], NEG_INF)]
    for i in range(1, 5):
        slabs.append(jnp.where((r8 >= 1) & (r8 < N_TOP // (i + 1)), a[i:i + 1] + b8, NEG_INF))
    slabs.append(jnp.where(r8 >= 5, a8 + b[1:2], NEG_INF))
    cand = jnp.concatenate(slabs, axis=0)

    def rnd(_, carry):
        work, _, cur = carry
        m = jnp.max(work, axis=0, keepdims=True)
        return jnp.where(work == m, NEG_INF, work), cur, m

    init = jnp.max(cand, axis=0, keepdims=True)
    _, t16, t17 = lax.fori_loop(0, N_TOP, rnd, (cand, init, init))
    m0 = a[0:1] + b[0:1]
    z = jnp.sum(jnp.where(cand >= t16, jnp.exp(cand - m0), 0.0), axis=0, keepdims=True)
    log2z = jnp.log2(z)
    s1n = (s1 - a[0:1]) * LOG2E
    s2n = (s2 - b[0:1]) * LOG2E - log2z
    taun = (0.5 * (t16 + t17) - m0) * LOG2E - log2z
    for q in range(tmr // LANES):
        lanes = slice(q * LANES, (q + 1) * LANES)
        s1_ref[q] = s1n[:, lanes]
        s2_ref[q] = s2n[:, lanes]
        tau_ref[q] = jnp.broadcast_to(taun[:, lanes], (SUBLANES, LANES))


def _peer_route(qp, subkeys):
    t = qp.shape[0]
    tmr = 2 * LANES if t > LANES else LANES
    ntile = pl.cdiv(t, tmr)
    k = tmr // LANES
    sblk = pl.BlockSpec((PEER_HEADS, k, PEER_NKEYS, LANES), lambda i: (0, i, 0, 0))
    return pl.pallas_call(
        functools.partial(_peer_route_kernel, tmr=tmr, t_total=t),
        grid=(ntile,),
        in_specs=[pl.BlockSpec((tmr, PEER_HEADS * 2 * PEER_NKEYS), lambda i: (i, 0)),
                  pl.BlockSpec((PEER_HEADS, 2, PEER_NKEYS, PEER_NKEYS), lambda i: (0, 0, 0, 0))],
        out_specs=[sblk, sblk, pl.BlockSpec((PEER_HEADS, k, SUBLANES, LANES), lambda i: (0, i, 0, 0))],
        out_shape=[jax.ShapeDtypeStruct((PEER_HEADS, ntile * k, PEER_NKEYS, LANES), F32)] * 2
        + [jax.ShapeDtypeStruct((PEER_HEADS, ntile * k, SUBLANES, LANES), F32)],
        compiler_params=_cparams(("arbitrary",), 32),
        name="peer_route",
    )(qp, subkeys)


def _peer_dense_kernel(x_ref, h_ref, u_ref, v_ref, s1_ref, s2_ref, tau_ref, o_ref, xs_ref, xu_ref, ht_ref,
                       *, tm, et, t_total):
    j = pl.program_id(1)

    @pl.when(j == 0)
    def _():
        live = lax.broadcasted_iota(jnp.int32, (tm, 1), 0) < t_total - pl.program_id(0) * tm
        o_ref[...] = jnp.where(live, h_ref[...], 0.0)
        xs_ref[...] = jnp.where(live, x_ref[...], jnp.zeros((), BF16))

    xu_ref[...] = _dot_nt(u_ref[...], xs_ref[...])
    na = et // PEER_NKEYS
    for ap in range(na):
        a_glob = j * na + ap
        rows = slice(ap * PEER_NKEYS, (ap + 1) * PEER_NKEYS)
        for lt in range(tm // LANES):
            lanes = slice(lt * LANES, (lt + 1) * LANES)
            gate = jnp.zeros((PEER_NKEYS, LANES), F32)
            for h in range(PEER_HEADS):
                v2 = s2_ref[h, lt] + s1_ref[h, lt, pl.ds(a_glob, 1), :]
                gate = gate + jnp.exp2(jnp.where(v2 >= tau_ref[h, lt, 0:1, :], v2, NEG_INF))
            ht_ref[rows, lanes] = (_gelu(xu_ref[rows, lanes]) * gate).astype(BF16)
    o_ref[...] += _dot_tn(ht_ref[...], v_ref[...])


def _peer_dense(xn, h, u_tab, v_tab, layer, s1n, s2n, tau):
    t, d = h.shape
    ne = u_tab.shape[1]
    tm = 5 * LANES if t > 5 * LANES else LANES
    et = 4 * PEER_NKEYS
    k = tm // LANES
    rblk = lambda rows: pl.BlockSpec((PEER_HEADS, k, rows, LANES), lambda i, j: (0, i, 0, 0))
    tab = pl.BlockSpec((None, et, d), lambda i, j: (layer, j, 0))
    return pl.pallas_call(
        functools.partial(_peer_dense_kernel, tm=tm, et=et, t_total=t),
        grid=(pl.cdiv(t, tm), ne // et),
        in_specs=[pl.BlockSpec((tm, d), lambda i, j: (i, 0)), pl.BlockSpec((tm, d), lambda i, j: (i, 0)),
                  tab, tab, rblk(PEER_NKEYS), rblk(PEER_NKEYS), rblk(SUBLANES)],
        out_specs=pl.BlockSpec((tm, d), lambda i, j: (i, 0)),
        out_shape=jax.ShapeDtypeStruct((t, d), F32),
        scratch_shapes=[pltpu.VMEM((tm, d), BF16), pltpu.VMEM((et, tm), F32), pltpu.VMEM((et, tm), BF16)],
        compiler_params=_cparams(("arbitrary", "arbitrary"), 56),
        name="peer_dense",
    )(xn, h, u_tab, v_tab, s1n, s2n, tau)


def _peer(h, norm_g, wq, subkeys, u_tab, v_tab, layer):
    t = h.shape[0]
    if t < LANES:
        h = jnp.pad(h, ((0, LANES - t), (0, 0)))
    qp, xn = _q_proj(h, norm_g, wq)
    s1n, s2n, tau = _peer_route(qp, subkeys)
    out = _peer_dense(xn, h, u_tab, v_tab, layer, s1n, s2n, tau)
    return out[:t]


def kernel(x_prompt, x_sample, cache_fox_k, cache_fox_v, cache_fox_logf, page_table, state_s5_re, state_s5_im, state_ret, state_lru, state_conv, meta_tokens, norm_mix_g, w_in, w_out, s5_lam_re, s5_lam_im, s5_log_dt, s5_b_re, s5_b_im, s5_c_re, s5_c_im, s5_d, s5_glu_w, s5_glu_b, fox_bf, ret_norm_g, lru_conv_w, lru_conv_b, lru_wa, lru_ba, lru_wx, lru_bx, lru_lam, norm_ffn_g, peer_wq, peer_subkeys, peer_u, peer_v, norm_final_g):
    bp, seq_x, d = x_prompt.shape
    bs = x_sample.shape[0]
    depth = w_in.shape[0]
    seq = seq_x + N_META
    meta = jnp.broadcast_to(meta_tokens[None], (bp, N_META, d))
    h_p = jnp.concatenate([meta, x_prompt], axis=1).reshape(bp * seq, d)
    h_s = x_sample.reshape(bs, d)
    zeros = lambda *shape: jnp.zeros(shape, F32)
    u_tab, v_tab = peer_u.astype(BF16), peer_v.astype(BF16)
    caches = _fox_caches(cache_fox_k, cache_fox_v, cache_fox_logf)
    w_main, w_ff = _repack_w_in(w_in)
    outs_p, outs_s = [], []
    for l in range(depth):
        w_o = w_out[l].astype(BF16).reshape(4, GROUP_W, d)
        s5w = _s5_weights(s5_lam_re[l], s5_lam_im[l], s5_log_dt[l], s5_b_re[l], s5_b_im[l], s5_c_re[l], s5_c_im[l])
        glu_w = s5_glu_w[l].astype(BF16)
        lru_w = (lru_conv_w[l], lru_conv_b[l], lru_wa[l], lru_ba[l], lru_wx[l], lru_bx[l], lru_lam[l])
        wq = peer_wq[l].astype(BF16)
        subkeys = peer_subkeys[l].astype(BF16)

        proj, ff = _in_proj(h_p, norm_mix_g[l], w_main, w_ff, l)
        y_s5, s5re, s5im = _s5(proj, zeros(bp, S5_STATE_W), zeros(bp, S5_STATE_W), s5w, s5_d[l], glu_w, s5_glu_b[l],
                               nb=bp, seq=seq)
        logf, c_blk = _fox_prep(ff, fox_bf[l], nb=bp, seq=seq)
        y_fox = _fox_prompt(proj, c_blk, nb=bp, seq=seq)
        y_ret, ret_s = _ret_prompt(proj, ret_norm_g[l], nb=bp, seq=seq)
        y_lru, lru_h, conv_buf = _lru(proj, zeros(bp, GROUP_W), zeros(bp, CONV_W - 1, GROUP_W), *lru_w, nb=bp, seq=seq)
        h_p = _out_proj(h_p, (y_s5, y_fox, y_ret, y_lru), w_o)
        h_p = _peer(h_p, norm_ffn_g[l], wq, subkeys, u_tab, v_tab, l)
        p3 = proj.reshape(bp, seq, -1)
        outs_p.append((p3[:, :, COL_FK * GROUP_W:(COL_FK + 1) * GROUP_W].reshape(bp, seq, N_HEADS, HEAD_D),
                       p3[:, :, COL_FV * GROUP_W:(COL_FV + 1) * GROUP_W].reshape(bp, seq, N_HEADS, HEAD_D),
                       logf, s5re.reshape(bp, -1, 64), s5im.reshape(bp, -1, 64), ret_s, lru_h, conv_buf))

        proj, ff = _in_proj(h_s, norm_mix_g[l], w_main, w_ff, l)
        y_s5, s5re, s5im = _s5(proj, state_s5_re[l].reshape(bs, S5_STATE_W), state_s5_im[l].reshape(bs, S5_STATE_W),
                               s5w, s5_d[l], glu_w, s5_glu_b[l], nb=bs, seq=1)
        y_fox, logf = _fox_step(proj, ff, fox_bf[l], caches, l, page_table, h_p, nb=bs)
        y_ret, ret_s = _ret_step(proj, ret_norm_g[l], state_ret[l], nb=bs)
        y_lru, lru_h, conv_buf = _lru(proj, state_lru[l], state_conv[l], *lru_w, nb=bs, seq=1)
        h_s = _out_proj(h_s, (y_s5, y_fox, y_ret, y_lru), w_o)
        h_s = _peer(h_s, norm_ffn_g[l], wq, subkeys, u_tab, v_tab, l)
        outs_s.append((proj[:, COL_FK * GROUP_W:(COL_FK + 1) * GROUP_W].reshape(bs, 1, N_HEADS, HEAD_D),
                       proj[:, COL_FV * GROUP_W:(COL_FV + 1) * GROUP_W].reshape(bs, 1, N_HEADS, HEAD_D),
                       logf.reshape(bs, 1, N_HEADS), s5re.reshape(bs, -1, 64), s5im.reshape(bs, -1, 64),
                       ret_s, lru_h, conv_buf))

    y_prompt = _rmsnorm(h_p, norm_final_g).reshape(bp, seq, d)[:, N_META:]
    y_sample = _rmsnorm(h_s, norm_final_g).reshape(bs, 1, d)
    stk = lambda outs, j: jnp.stack([o[j] for o in outs], axis=0)
    return ((y_prompt, y_sample) + tuple(stk(outs_p, j) for j in range(8)) + tuple(stk(outs_s, j) for j in range(8)))
```

```python
import functools
import math

import jax
import jax.numpy as jnp
from jax import lax
from jax.experimental import pallas as pl
from jax.experimental.pallas import tpu as pltpu

F32 = jnp.float32
BF16 = jnp.bfloat16
NEG_INF = float("-inf")

EPS = 1e-6
N_META = 16
PAST_LEN = 16384
PAGE_SIZE = 128
GROUP_W = 512
HEAD_D = 128
N_HEADS = 4
S5_STATE_W = 2048
LRU_C = 8.0
CONV_W = 4
PEER_HEADS = 8
PEER_NKEYS = 128
PEER_TOPK = 16
LANES = 128
SUBLANES = 8
MXU_W = 256
LOG2E = 1.4426950408889634
MIB = 1024 * 1024

COL_S5, COL_FQ, COL_FK, COL_FV, COL_RQ, COL_RK, COL_RV, COL_RG, COL_LX, COL_LG = range(10)


def _cparams(sem, vmem_mib, **kw):
    return pltpu.CompilerParams(dimension_semantics=sem, vmem_limit_bytes=vmem_mib * MIB, **kw)


def _dot(a, b):
    return jnp.dot(a, b, preferred_element_type=F32)


def _dot_nt(a, b):
    return lax.dot_general(a, b, (((1,), (1,)), ((), ())), preferred_element_type=F32)


def _dot_tn(a, b):
    return lax.dot_general(a, b, (((0,), (0,)), ((), ())), preferred_element_type=F32)


def _dot_f32(a, b):
    return jnp.dot(a, b, preferred_element_type=F32, precision=lax.Precision.HIGHEST)


def _gelu(x):
    return 0.5 * x * (1.0 + jnp.tanh(0.7978845608028654 * (x + 0.044715 * (x * x * x))))


def _row_tile(n, target):
    if n <= target:
        return n
    best = None
    for t in range(SUBLANES, target + 1, SUBLANES):
        if n % t == 0:
            best = t
    assert best is not None, n
    return best


def _in_proj_kernel(x_ref, g_ref, w_ref, wff_ref, o_ref, off_ref, xn_ref):
    @pl.when(pl.program_id(1) == 0)
    def _():
        x = x_ref[...]
        ms = jnp.mean(x * x, axis=-1, keepdims=True)
        xn = (x * lax.rsqrt(ms + EPS) * g_ref[...]).astype(BF16)
        xn_ref[...] = xn
        off_ref[...] = _dot(xn, wff_ref[...])
    o_ref[...] = _dot(xn_ref[...], w_ref[...])


def _repack_w_in_kernel(w_ref, main_ref, ff_ref):
    g0 = 4 * GROUP_W
    w = w_ref[...]
    main_ref[:, :g0] = w[:, :g0].astype(BF16)
    main_ref[:, g0:] = w[:, g0 + N_HEADS:].astype(BF16)
    lane = lax.broadcasted_iota(jnp.int32, (w.shape[0], LANES), 1)
    ff_ref[...] = jnp.where(lane < N_HEADS, w[:, g0:g0 + LANES], 0.0).astype(BF16)


def _repack_w_in(w_in):
    depth, d, n = w_in.shape
    rows = 256
    return pl.pallas_call(
        _repack_w_in_kernel,
        grid=(depth, d // rows),
        in_specs=[pl.BlockSpec((None, rows, n), lambda l, r: (l, r, 0))],
        out_specs=[pl.BlockSpec((None, rows, n - N_HEADS), lambda l, r: (l, r, 0)),
                   pl.BlockSpec((None, rows, LANES), lambda l, r: (l, r, 0))],
        out_shape=[jax.ShapeDtypeStruct((depth, d, n - N_HEADS), BF16), jax.ShapeDtypeStruct((depth, d, LANES), BF16)],
        compiler_params=_cparams(("arbitrary", "arbitrary"), 32),
        name="repack_w_in",
    )(w_in)


def _in_proj(x, g, w_main, w_ff, layer):
    t, d = x.shape
    n = w_main.shape[2]
    tm, tn = _row_tile(t, 1032), 512
    return pl.pallas_call(
        _in_proj_kernel,
        grid=(t // tm, n // tn),
        in_specs=[pl.BlockSpec((tm, d), lambda i, j: (i, 0)),
                  pl.BlockSpec((1, d), lambda i, j: (0, 0)),
                  pl.BlockSpec((None, d, tn), lambda i, j: (layer, 0, j)),
                  pl.BlockSpec((None, d, LANES), lambda i, j: (layer, 0, 0))],
        out_specs=[pl.BlockSpec((tm, tn), lambda i, j: (i, j)),
                   pl.BlockSpec((tm, LANES), lambda i, j: (i, 0))],
        out_shape=[jax.ShapeDtypeStruct((t, n), F32), jax.ShapeDtypeStruct((t, LANES), F32)],
        scratch_shapes=[pltpu.VMEM((tm, d), BF16)],
        compiler_params=_cparams(("arbitrary", "arbitrary"), 48),
        name="in_proj",
    )(x, g.reshape(1, d), w_main, w_ff)


def _q_proj_kernel(x_ref, g_ref, w_ref, o_ref, xn_ref):
    @pl.when(pl.program_id(1) == 0)
    def _():
        x = x_ref[...]
        ms = jnp.mean(x * x, axis=-1, keepdims=True)
        xn_ref[...] = (x * lax.rsqrt(ms + EPS) * g_ref[...]).astype(BF16)
    o_ref[...] = _dot(xn_ref[...], w_ref[...])


def _q_proj(x, g, w):
    t, d = x.shape
    n = w.shape[1]
    tm, tn = _row_tile(t, 1032), 512
    return pl.pallas_call(
        _q_proj_kernel,
        grid=(t // tm, n // tn),
        in_specs=[pl.BlockSpec((tm, d), lambda i, j: (i, 0)),
                  pl.BlockSpec((1, d), lambda i, j: (0, 0)),
                  pl.BlockSpec((d, tn), lambda i, j: (0, j))],
        out_specs=[pl.BlockSpec((tm, tn), lambda i, j: (i, j)),
                   pl.BlockSpec((tm, d), lambda i, j: (i, 0))],
        out_shape=[jax.ShapeDtypeStruct((t, n), F32), jax.ShapeDtypeStruct((t, d), BF16)],
        compiler_params=_cparams(("arbitrary", "arbitrary"), 48),
        name="q_proj",
    )(x, g.reshape(1, d), w)


def _out_proj_kernel(h_ref, y0_ref, y1_ref, y2_ref, y3_ref, w_ref, o_ref):
    acc = h_ref[...]
    for gi, y_ref in enumerate((y0_ref, y1_ref, y2_ref, y3_ref)):
        acc = acc + _dot(y_ref[...].astype(BF16), w_ref[gi])
    o_ref[...] = acc


def _out_proj(h, ys, w):
    t, d = h.shape
    tm, tn = _row_tile(t, 1032), 512
    yspec = pl.BlockSpec((tm, GROUP_W), lambda i, j: (i, 0))
    return pl.pallas_call(
        _out_proj_kernel,
        grid=(t // tm, d // tn),
        in_specs=[pl.BlockSpec((tm, tn), lambda i, j: (i, j)), yspec, yspec, yspec, yspec,
                  pl.BlockSpec((4, GROUP_W, tn), lambda i, j: (0, 0, j))],
        out_specs=pl.BlockSpec((tm, tn), lambda i, j: (i, j)),
        out_shape=jax.ShapeDtypeStruct((t, d), F32),
        compiler_params=_cparams(("arbitrary", "arbitrary"), 48),
        name="out_proj",
    )(h, *ys, w)


def _rmsnorm_kernel(x_ref, g_ref, o_ref):
    x = x_ref[...]
    ms = jnp.mean(x * x, axis=-1, keepdims=True)
    o_ref[...] = x * lax.rsqrt(ms + EPS) * g_ref[...]


def _rmsnorm(x, g):
    t, d = x.shape
    tm = _row_tile(t, 1032)
    return pl.pallas_call(
        _rmsnorm_kernel,
        grid=(t // tm,),
        in_specs=[pl.BlockSpec((tm, d), lambda i: (i, 0)), pl.BlockSpec((1, d), lambda i: (0, 0))],
        out_specs=pl.BlockSpec((tm, d), lambda i: (i, 0)),
        out_shape=jax.ShapeDtypeStruct((t, d), F32),
        compiler_params=_cparams(("arbitrary",), 48),
        name="final_norm",
    )(x, g.reshape(1, d))


def _s5_kernel(u_ref, h0re_ref, h0im_ref, lre_ref, lim_ref, bw_ref, cw_ref, d_ref, gw_ref, gb_ref,
               y_ref, hre_ref, him_ref, sre_ref, sim_ref, *, nb, tc):
    single = tc == 1

    @pl.when(pl.program_id(0) == 0)
    def _():
        hre_ref[...] = h0re_ref[...]
        him_ref[...] = h0im_ref[...]

    def get_u(b):
        return u_ref[...] if single else u_ref[b]

    seqs = (0,) if single else tuple(range(nb))
    nrow = nb if single else tc
    for b in seqs:
        ub = get_u(b).astype(BF16)
        for r in range(4):
            bu = _dot(ub[:, r * LANES:(r + 1) * LANES], bw_ref[r])
            for q in range(4):
                sre_ref[4 * r + q, b * nrow:(b + 1) * nrow, :] = bu[:, q * LANES:(q + 1) * LANES]
                sim_ref[4 * r + q, b * nrow:(b + 1) * nrow, :] = bu[:, GROUP_W + q * LANES:GROUP_W + (q + 1) * LANES]

    for r in range(4):
        tiles = tuple(range(4 * r, 4 * r + 4))
        lanes = [slice(lt * LANES, (lt + 1) * LANES) for lt in tiles]
        lr = [jnp.broadcast_to(lre_ref[:, ln], (nb, LANES)) for ln in lanes]
        li = [jnp.broadcast_to(lim_ref[:, ln], (nb, LANES)) for ln in lanes]

        def body(t, carry, tiles=tiles, lr=lr, li=li):
            idx = pl.ds(t, nb, stride=tc)
            new = []
            for q, lt in enumerate(tiles):
                hr, hi = carry[q]
                nr = lr[q] * hr - li[q] * hi + sre_ref[lt, idx, :]
                ni = lr[q] * hi + li[q] * hr + sim_ref[lt, idx, :]
                sre_ref[lt, idx, :] = nr
                sim_ref[lt, idx, :] = ni
                new.append((nr, ni))
            return tuple(new)

        fin = lax.fori_loop(0, tc, body, tuple((hre_ref[:, ln], him_ref[:, ln]) for ln in lanes))
        for q, ln in enumerate(lanes):
            hre_ref[:, ln] = fin[q][0]
            him_ref[:, ln] = fin[q][1]

    for b in seqs:
        rows = slice(b * nrow, (b + 1) * nrow)
        ys = []
        for r in range(4):
            hre = jnp.concatenate([sre_ref[4 * r + q, rows, :] for q in range(4)], axis=1).astype(BF16)
            him = jnp.concatenate([sim_ref[4 * r + q, rows, :] for q in range(4)], axis=1).astype(BF16)
            ys.append(_dot(hre, cw_ref[r, :GROUP_W, :]) + _dot(him, cw_ref[r, GROUP_W:, :]))
        y = jnp.concatenate(ys, axis=1) + d_ref[...] * get_u(b)
        y = _gelu(y)
        z = _dot(y.astype(BF16), gw_ref[...]) + gb_ref[...]
        out = y * jax.nn.sigmoid(z)
        if single:
            y_ref[...] = out
        else:
            y_ref[b] = out


def _s5_weights(lam_re, lam_im, log_dt, b_re, b_im, c_re, c_im):
    lr, li = jnp.minimum(lam_re, -1e-4), lam_im
    dt = jnp.exp(log_dt)[:, None]
    mag = jnp.exp(lr * dt)
    bar_re, bar_im = mag * jnp.cos(li * dt), mag * jnp.sin(li * dt)
    den = lr * lr + li * li
    f_re = ((bar_re - 1.0) * lr + bar_im * li) / den
    f_im = (bar_im * lr - (bar_re - 1.0) * li) / den
    bb_re = f_re[:, :, None] * b_re - f_im[:, :, None] * b_im
    bb_im = f_re[:, :, None] * b_im + f_im[:, :, None] * b_re
    eye = jnp.eye(8, dtype=F32)

    def pack_b(m):
        m = m.reshape(4, 8, 64, 16)
        return jnp.einsum("rgpc,gh->rgchp", m, eye).reshape(4, LANES, GROUP_W)

    def pack_c(m):
        m = m.reshape(4, 8, 16, 64)
        return jnp.einsum("rgcp,hg->rhpgc", m, eye).reshape(4, GROUP_W, LANES)

    bw = jnp.concatenate([pack_b(bb_re), pack_b(bb_im)], axis=2).astype(BF16)
    cw = jnp.concatenate([pack_c(c_re), -pack_c(c_im)], axis=1).astype(BF16)
    return (bar_re.reshape(1, S5_STATE_W), bar_im.reshape(1, S5_STATE_W), bw, cw)


def _s5(proj, h0re, h0im, wts, d, glu_w, glu_b, *, nb, seq):
    lre, lim, bw, cw = wts
    single = seq == 1
    tc = 1 if single else _row_tile(seq, 344)
    nchunk = seq // tc
    full = lambda shape: pl.BlockSpec(shape, lambda c: tuple(0 for _ in shape))
    if single:
        u_in = proj
        u_spec = pl.BlockSpec((nb, GROUP_W), lambda c: (0, COL_S5))
        y_spec = pl.BlockSpec((nb, GROUP_W), lambda c: (0, 0))
        y_shape = jax.ShapeDtypeStruct((nb, GROUP_W), F32)
    else:
        u_in = proj.reshape(nb, seq, proj.shape[1])
        u_spec = pl.BlockSpec((nb, tc, GROUP_W), lambda c: (0, c, COL_S5))
        y_spec = pl.BlockSpec((nb, tc, GROUP_W), lambda c: (0, c, 0))
        y_shape = jax.ShapeDtypeStruct((nb, seq, GROUP_W), F32)
    y, hre, him = pl.pallas_call(
        functools.partial(_s5_kernel, nb=nb, tc=tc),
        grid=(nchunk,),
        in_specs=[u_spec, full((nb, S5_STATE_W)), full((nb, S5_STATE_W)), full((1, S5_STATE_W)),
                  full((1, S5_STATE_W)), full((4, LANES, 2 * GROUP_W)), full((4, 2 * GROUP_W, LANES)),
                  full((1, GROUP_W)), full((GROUP_W, GROUP_W)), full((1, GROUP_W))],
        out_specs=[y_spec, full((nb, S5_STATE_W)), full((nb, S5_STATE_W))],
        out_shape=[y_shape, jax.ShapeDtypeStruct((nb, S5_STATE_W), F32),
                   jax.ShapeDtypeStruct((nb, S5_STATE_W), F32)],
        scratch_shapes=[pltpu.VMEM((S5_STATE_W // LANES, nb * tc, LANES), F32)] * 2,
        compiler_params=_cparams(("arbitrary",), 56),
        name="s5",
    )(u_in, h0re, h0im, lre, lim, bw, cw, d.reshape(1, GROUP_W), glu_w, glu_b.reshape(1, GROUP_W))
    return y.reshape(nb * seq, GROUP_W), hre, him


def _softplus(x):
    return jnp.maximum(x, 0.0) + jnp.log1p(jnp.exp(-jnp.abs(x)))


def _lru_gates(conv, wa_ref, wx_ref, ba_ref, bx_ref, lam_ref):
    cb = conv.astype(BF16)
    r = jax.nn.sigmoid(_dot(cb, wa_ref[...]) + ba_ref[...])
    i = jax.nn.sigmoid(_dot(cb, wx_ref[...]) + bx_ref[...])
    log_a = -LRU_C * r * _softplus(-lam_ref[...])
    a = jnp.exp(log_a)
    return a, jnp.sqrt(-jnp.tanh(log_a) * (a * a + 1.0)) * (i * conv)


def _lru_kernel(x_ref, g_ref, h0_ref, buf0_ref, cw_ref, cb_ref, wa_ref, ba_ref, wx_ref, bx_ref, lam_ref,
                y_ref, h_ref, buf_ref, xe_ref, sa_ref, sx_ref, *, nb, tc):
    nq = GROUP_W // LANES

    @pl.when(pl.program_id(0) == 0)
    def _():
        h_ref[...] = h0_ref[...]
        for b in range(nb):
            xe_ref[b, 5:8, :] = buf0_ref[b]

    for b in range(nb):
        xe_ref[b, 8:8 + tc, :] = x_ref[b]
    for b in range(nb):
        conv = cb_ref[...] + sum(xe_ref[b, 5 + j:5 + j + tc, :] * cw_ref[j:j + 1, :] for j in range(CONV_W))
        a, xin = _lru_gates(conv, wa_ref, wx_ref, ba_ref, bx_ref, lam_ref)
        for q in range(nq):
            sa_ref[q, b * tc:(b + 1) * tc, :] = a[:, q * LANES:(q + 1) * LANES]
            sx_ref[q, b * tc:(b + 1) * tc, :] = xin[:, q * LANES:(q + 1) * LANES]

    def body(t, hs):
        idx = pl.ds(t, nb, stride=tc)
        new = []
        for q in range(nq):
            h = sa_ref[q, idx, :] * hs[q] + sx_ref[q, idx, :]
            sx_ref[q, idx, :] = h
            new.append(h)
        return tuple(new)

    fin = lax.fori_loop(0, tc, body, tuple(h_ref[:, q * LANES:(q + 1) * LANES] for q in range(nq)))
    for q in range(nq):
        h_ref[:, q * LANES:(q + 1) * LANES] = fin[q]
    for b in range(nb):
        hseq = jnp.concatenate([sx_ref[q, b * tc:(b + 1) * tc, :] for q in range(nq)], axis=1)
        y_ref[b] = hseq * _gelu(g_ref[b])
        tail = xe_ref[b, tc + 5:tc + 8, :]
        xe_ref[b, 5:8, :] = tail
        buf_ref[b] = tail


def _lru_step_kernel(x_ref, g_ref, h0_ref, buf0_ref, cw_ref, cb_ref, wa_ref, ba_ref, wx_ref, bx_ref, lam_ref,
                     y_ref, h_ref, buf_ref):
    x = x_ref[...]
    conv = cb_ref[...] + x * cw_ref[3:4, :] + sum(buf0_ref[j] * cw_ref[j:j + 1, :] for j in range(CONV_W - 1))
    a, xin = _lru_gates(conv, wa_ref, wx_ref, ba_ref, bx_ref, lam_ref)
    h = a * h0_ref[...] + xin
    h_ref[...] = h
    y_ref[...] = h * _gelu(g_ref[...])
    buf_ref[0] = buf0_ref[1]
    buf_ref[1] = buf0_ref[2]
    buf_ref[2] = x


def _block_diag(w):
    return jnp.einsum("hij,hg->higj", w, jnp.eye(8, dtype=w.dtype)).reshape(GROUP_W, GROUP_W)


def _lru(proj, h0, buf0, conv_w, conv_b, wa, ba, wx, bx, lam, *, nb, seq):
    single = seq == 1
    row = lambda v: v.reshape(1, GROUP_W)
    wts = (conv_w, row(conv_b), _block_diag(wa).astype(BF16), row(ba), _block_diag(wx).astype(BF16), row(bx), row(lam))
    full = lambda shape: pl.BlockSpec(shape, lambda c: tuple(0 for _ in shape))
    wspecs = [full((CONV_W, GROUP_W)), full((1, GROUP_W)), full((GROUP_W, GROUP_W)), full((1, GROUP_W)),
              full((GROUP_W, GROUP_W)), full((1, GROUP_W)), full((1, GROUP_W))]
    if single:
        y, h, buf = pl.pallas_call(
            _lru_step_kernel,
            grid=(1,),
            in_specs=[pl.BlockSpec((nb, GROUP_W), lambda c: (0, COL_LX)), pl.BlockSpec((nb, GROUP_W), lambda c: (0, COL_LG)),
                      full((nb, GROUP_W)), full((CONV_W - 1, nb, GROUP_W))] + wspecs,
            out_specs=[full((nb, GROUP_W)), full((nb, GROUP_W)), full((CONV_W - 1, nb, GROUP_W))],
            out_shape=[jax.ShapeDtypeStruct((nb, GROUP_W), F32), jax.ShapeDtypeStruct((nb, GROUP_W), F32),
                       jax.ShapeDtypeStruct((CONV_W - 1, nb, GROUP_W), F32)],
            compiler_params=_cparams(("arbitrary",), 32),
            name="lru_step",
        )(proj, proj, h0, jnp.transpose(buf0, (1, 0, 2)), *wts)
        return y, h, jnp.transpose(buf, (1, 0, 2))
    tc = _row_tile(seq, 344)
    p3 = proj.reshape(nb, seq, proj.shape[1])
    y, h, buf = pl.pallas_call(
        functools.partial(_lru_kernel, nb=nb, tc=tc),
        grid=(seq // tc,),
        in_specs=[pl.BlockSpec((nb, tc, GROUP_W), lambda c: (0, c, COL_LX)),
                  pl.BlockSpec((nb, tc, GROUP_W), lambda c: (0, c, COL_LG)),
                  full((nb, GROUP_W)), full((nb, CONV_W - 1, GROUP_W))] + wspecs,
        out_specs=[pl.BlockSpec((nb, tc, GROUP_W), lambda c: (0, c, 0)), full((nb, GROUP_W)),
                   full((nb, CONV_W - 1, GROUP_W))],
        out_shape=[jax.ShapeDtypeStruct((nb, seq, GROUP_W), F32), jax.ShapeDtypeStruct((nb, GROUP_W), F32),
                   jax.ShapeDtypeStruct((nb, CONV_W - 1, GROUP_W), F32)],
        scratch_shapes=[pltpu.VMEM((nb, tc + 8, GROUP_W), F32), pltpu.VMEM((GROUP_W // LANES, nb * tc, LANES), F32),
                        pltpu.VMEM((GROUP_W // LANES, nb * tc, LANES), F32)],
        compiler_params=_cparams(("arbitrary",), 48),
        name="lru",
    )(p3, p3, h0, buf0, *wts)
    return y.reshape(nb * seq, GROUP_W), h, buf


def _ret_log_gamma(h):
    return math.log(1.0 - 2.0 ** (-5.0 - h))


def _rotary_tables(pos):
    half = HEAD_D // 2
    inv = 1.0 / (10000.0 ** (jnp.arange(half, dtype=F32) / half))
    ang = pos.astype(F32)[:, None] * inv[None, :]
    cos, sin = jnp.cos(ang), jnp.sin(ang)
    return jnp.concatenate([cos, cos], axis=1), jnp.concatenate([-sin, sin], axis=1)


def _rotate(x, cos2, sin2):
    return x * cos2 + pltpu.roll(x, HEAD_D // 2, 1) * sin2


def _ret_kernel(q_ref, k_ref, v_ref, g_ref, cos_ref, sin_ref, ng_ref, y_ref, s_ref, *, seq):
    c = pl.program_id(1)

    @pl.when(c == 0)
    def _():
        s_ref[...] = jnp.zeros_like(s_ref)

    nv = jnp.minimum(LANES, seq - c * LANES)
    nvf = nv.astype(F32)
    row = lax.broadcasted_iota(jnp.int32, (LANES, 1), 0)
    rowf = row.astype(F32)
    valid = row < nv
    diff = (lax.broadcasted_iota(jnp.int32, (LANES, LANES), 0)
            - lax.broadcasted_iota(jnp.int32, (LANES, LANES), 1)).astype(F32)
    cos2, sin2 = cos_ref[...], sin_ref[...]
    for h in range(N_HEADS):
        lg = _ret_log_gamma(h)
        cols = slice(h * HEAD_D, (h + 1) * HEAD_D)
        decay = jnp.exp(jnp.where(diff >= 0, diff * lg, NEG_INF))
        q_dec = jnp.exp((rowf + 1.0) * lg)
        k_dec = jnp.exp((nvf - 1.0 - rowf) * lg)
        c_dec = jnp.exp(jnp.full((1, 1), lg, F32) * nvf)
        q = jnp.where(valid, q_ref[:, cols], 0.0)
        k = jnp.where(valid, k_ref[:, cols], 0.0)
        v = jnp.where(valid, v_ref[:, cols], 0.0)
        qr = _rotate(q, cos2, sin2)
        kr = _rotate(k, cos2, sin2) * (HEAD_D ** -0.5)
        vb = v.astype(BF16)
        att = _dot_nt(qr.astype(BF16), kr.astype(BF16)) * decay
        s_old = s_ref[h]
        o = _dot(att.astype(BF16), vb) + _dot((qr * q_dec).astype(BF16), s_old.astype(BF16))
        s_ref[h] = s_old * c_dec + _dot_tn((kr * k_dec).astype(BF16), vb)
        o = o * lax.rsqrt(jnp.mean(o * o, axis=-1, keepdims=True) + EPS) * ng_ref[:, cols]
        g = jnp.where(valid, g_ref[:, cols], 0.0)
        y_ref[:, cols] = o * (g * jax.nn.sigmoid(g))


def _ret_prompt(proj, norm_g, *, nb, seq):
    p3 = proj.reshape(nb, seq, proj.shape[1])
    nchunk = pl.cdiv(seq, LANES)
    cos2, sin2 = _rotary_tables(jnp.arange(nchunk * LANES))
    blk = lambda col: pl.BlockSpec((None, LANES, GROUP_W), lambda b, c: (b, c, col))
    tab = pl.BlockSpec((LANES, HEAD_D), lambda b, c: (c, 0))
    y, s = pl.pallas_call(
        functools.partial(_ret_kernel, seq=seq),
        grid=(nb, nchunk),
        in_specs=[blk(COL_RQ), blk(COL_RK), blk(COL_RV), blk(COL_RG), tab, tab,
                  pl.BlockSpec((1, GROUP_W), lambda b, c: (0, 0))],
        out_specs=[pl.BlockSpec((None, LANES, GROUP_W), lambda b, c: (b, c, 0)),
                   pl.BlockSpec((None, N_HEADS, HEAD_D, HEAD_D), lambda b, c: (b, 0, 0, 0))],
        out_shape=[jax.ShapeDtypeStruct((nb, seq, GROUP_W), F32),
                   jax.ShapeDtypeStruct((nb, N_HEADS, HEAD_D, HEAD_D), F32)],
        compiler_params=_cparams(("arbitrary", "arbitrary"), 32),
        name="retention",
    )(p3, p3, p3, p3, cos2, sin2, norm_g.reshape(1, GROUP_W))
    return y.reshape(nb * seq, GROUP_W), s


def _ret_step_kernel(q_ref, k_ref, v_ref, g_ref, cos_ref, sin_ref, ng_ref, s0_ref, y_ref, s_ref):
    cos2, sin2 = cos_ref[...], sin_ref[...]
    eye = (lax.broadcasted_iota(jnp.int32, (HEAD_D, HEAD_D), 0)
           == lax.broadcasted_iota(jnp.int32, (HEAD_D, HEAD_D), 1))
    for h in range(N_HEADS):
        gamma = 1.0 - 2.0 ** (-5.0 - h)
        cols = slice(h * HEAD_D, (h + 1) * HEAD_D)
        qr = _rotate(q_ref[:, cols], cos2, sin2)
        kr = _rotate(k_ref[:, cols], cos2, sin2) * (HEAD_D ** -0.5)
        v = v_ref[:, cols]
        s0 = s0_ref[h]
        qs = _dot_f32(jnp.broadcast_to(qr, (SUBLANES, HEAD_D)), s0)[0:1]
        o = jnp.sum(qr * kr, axis=-1, keepdims=True) * v + gamma * qs
        kcol = jnp.sum(jnp.where(eye, jnp.broadcast_to(kr, (HEAD_D, HEAD_D)), 0.0), axis=1, keepdims=True)
        s_ref[h] = gamma * s0 + kcol * v
        o = o * lax.rsqrt(jnp.mean(o * o, axis=-1, keepdims=True) + EPS) * ng_ref[:, cols]
        g = g_ref[:, cols]
        y_ref[:, cols] = o * (g * jax.nn.sigmoid(g))


def _ret_step(proj, norm_g, s0, *, nb):
    p3 = proj.reshape(nb, 1, proj.shape[1])
    cos2, sin2 = _rotary_tables(jnp.full((1,), PAST_LEN))
    blk = lambda col: pl.BlockSpec((None, 1, GROUP_W), lambda b: (b, 0, col))
    one = lambda n: pl.BlockSpec((1, n), lambda b: (0, 0))
    st = pl.BlockSpec((None, N_HEADS, HEAD_D, HEAD_D), lambda b: (b, 0, 0, 0))
    y, s = pl.pallas_call(
        _ret_step_kernel,
        grid=(nb,),
        in_specs=[blk(COL_RQ), blk(COL_RK), blk(COL_RV), blk(COL_RG), one(HEAD_D), one(HEAD_D), one(GROUP_W), st],
        out_specs=[pl.BlockSpec((None, 1, GROUP_W), lambda b: (b, 0, 0)), st],
        out_shape=[jax.ShapeDtypeStruct((nb, 1, GROUP_W), F32),
                   jax.ShapeDtypeStruct((nb, N_HEADS, HEAD_D, HEAD_D), F32)],
        compiler_params=_cparams(("arbitrary",), 32),
        name="retention_step",
    )(p3, p3, p3, p3, cos2, sin2, norm_g.reshape(1, GROUP_W), s0)
    return y.reshape(nb, GROUP_W), s


def _log_sigmoid(x):
    return jnp.minimum(x, 0.0) - jnp.log1p(jnp.exp(-jnp.abs(x)))


def _fox_prep_kernel(ff_ref, bf_ref, logf_ref, c_ref):
    logf = _log_sigmoid(ff_ref[...] + bf_ref[...])
    logf_ref[...] = logf
    tri = (lax.broadcasted_iota(jnp.int32, (LANES, LANES), 0)
           <= lax.broadcasted_iota(jnp.int32, (LANES, LANES), 1)).astype(F32)
    carry = jnp.zeros((SUBLANES, 1), F32)
    for j in range(logf.shape[1] // LANES):
        blk = logf[:, j * LANES:(j + 1) * LANES]
        c_ref[:, j * LANES:(j + 1) * LANES] = carry + _dot_f32(blk, tri)
        carry = carry + jnp.sum(blk, axis=1, keepdims=True)


FOX_TK = 512
FOX_TQ = 344


def _fox_prep(ff, bf, *, nb, seq):
    nblk = pl.cdiv(seq, FOX_TK)
    lp = nblk * FOX_TK
    fft = jnp.transpose(ff[:, :N_HEADS].reshape(nb, seq, N_HEADS), (0, 2, 1))
    fft = jnp.pad(fft, ((0, 0), (0, SUBLANES - N_HEADS), (0, lp - seq)))
    bfc = jnp.pad(bf, (0, SUBLANES - N_HEADS)).reshape(SUBLANES, 1)
    spec = pl.BlockSpec((None, SUBLANES, lp), lambda b: (b, 0, 0))
    logf_t, c_t = pl.pallas_call(
        _fox_prep_kernel,
        grid=(nb,),
        in_specs=[spec, pl.BlockSpec((SUBLANES, 1), lambda b: (0, 0))],
        out_specs=[spec, spec],
        out_shape=[jax.ShapeDtypeStruct((nb, SUBLANES, lp), F32)] * 2,
        compiler_params=_cparams(("arbitrary",), 56),
        name="fox_prep",
    )(fft, bfc)
    logf = jnp.transpose(logf_t[:, :N_HEADS, :seq], (0, 2, 1))
    c_blk = jnp.transpose(c_t.reshape(nb, SUBLANES, nblk, FOX_TK), (0, 2, 1, 3))
    return logf, c_blk


def _softmax_update(carry, s, vb):
    m, l, acc = carry
    mn = jnp.maximum(m, jnp.max(s, axis=-1, keepdims=True))
    p = jnp.exp(s - mn)
    al = jnp.exp(m - mn)
    return mn, al * l + jnp.sum(p, axis=-1, keepdims=True), al * acc + _dot(p.astype(BF16), vb)


def _fox_attn_kernel(q_ref, k_ref, v_ref, c_ref, o_ref, kb_ref, vb_ref, m_ref, l_ref, acc_ref, *, seq):
    i = pl.program_id(1)
    tk = FOX_TK
    nfull = seq // tk
    rem = seq - nfull * tk
    scale = HEAD_D ** -0.5

    @pl.when(i == 0)
    def _():
        kb_ref[...] = k_ref[...].astype(BF16)
        vb_ref[...] = v_ref[...].astype(BF16)

    m_ref[...] = jnp.full_like(m_ref, NEG_INF)
    l_ref[...] = jnp.zeros_like(l_ref)
    acc_ref[...] = jnp.zeros_like(acc_ref)
    tq = o_ref.shape[0]
    qpos = i * tq + lax.broadcasted_iota(jnp.int32, (tq, 1), 0)
    head_cols = [slice(h * HEAD_D, (h + 1) * HEAD_D) for h in range(N_HEADS)]
    live = qpos < seq
    qbs = [jnp.where(live, q_ref[:, cols], 0.0).astype(BF16) for cols in head_cols]

    def tile(rows, cj, kpos):
        mask = kpos <= qpos
        heads = range(N_HEADS)
        ss = [jnp.where(mask, _dot_nt(qbs[h], kb_ref[rows, head_cols[h]]) * scale - cj[h:h + 1, :], NEG_INF)
              for h in heads]
        ms = [m_ref[h] for h in heads]
        mns = [jnp.maximum(ms[h], jnp.max(ss[h], axis=-1, keepdims=True)) for h in heads]
        ps = [jnp.exp(ss[h] - mns[h]) for h in heads]
        als = [jnp.exp(ms[h] - mns[h]) for h in heads]
        pvs = [_dot(ps[h].astype(BF16), vb_ref[rows, head_cols[h]]) for h in heads]
        for h in heads:
            m_ref[h] = mns[h]
            l_ref[h] = als[h] * l_ref[h] + jnp.sum(ps[h], axis=-1, keepdims=True)
            acc_ref[h] = als[h] * acc_ref[h] + pvs[h]

    def body(j, carry):
        r0 = pl.multiple_of(j * tk, tk)
        tile(pl.ds(r0, tk), c_ref[j], j * tk + lax.broadcasted_iota(jnp.int32, (1, tk), 1))
        return carry

    lax.fori_loop(0, jnp.minimum((i * tq + tq + tk - 1) // tk, nfull), body, 0)
    if rem:
        @pl.when(i * tq + tq > nfull * tk)
        def _():
            tile(slice(nfull * tk, seq), c_ref[nfull][:, 0:rem],
                 nfull * tk + lax.broadcasted_iota(jnp.int32, (1, rem), 1))
    for h, cols in enumerate(head_cols):
        o_ref[:, cols] = acc_ref[h] / l_ref[h]


def _fox_prompt(proj, c_blk, *, nb, seq):
    p3 = proj.reshape(nb, seq, proj.shape[1])
    nkt = c_blk.shape[1]
    tq = _row_tile(seq, FOX_TQ)
    kv = lambda col: pl.BlockSpec((None, seq, GROUP_W), lambda b, i: (b, 0, col))
    y = pl.pallas_call(
        functools.partial(_fox_attn_kernel, seq=seq),
        grid=(nb, seq // tq),
        in_specs=[pl.BlockSpec((None, tq, GROUP_W), lambda b, i: (b, i, COL_FQ)), kv(COL_FK), kv(COL_FV),
                  pl.BlockSpec((None, nkt, SUBLANES, FOX_TK), lambda b, i: (b, 0, 0, 0))],
        out_specs=pl.BlockSpec((None, tq, GROUP_W), lambda b, i: (b, i, 0)),
        out_shape=jax.ShapeDtypeStruct((nb, seq, GROUP_W), F32),
        scratch_shapes=[pltpu.VMEM((seq, GROUP_W), BF16), pltpu.VMEM((seq, GROUP_W), BF16),
                        pltpu.VMEM((N_HEADS, tq, 1), F32), pltpu.VMEM((N_HEADS, tq, 1), F32),
                        pltpu.VMEM((N_HEADS, tq, HEAD_D), F32)],
        compiler_params=_cparams(("arbitrary", "arbitrary"), 48),
        name="fox_attn",
    )(p3, p3, p3, c_blk)
    return y.reshape(nb * seq, GROUP_W)


def _fox_step_kernel(pt_ref, q_ref, kn_ref, vn_ref, ff_ref, bf_ref, anchor_ref, *rest, pps):
    del anchor_ref
    k_refs, v_refs, lf_refs = rest[:pps], rest[pps:2 * pps], rest[2 * pps:3 * pps]
    o_ref, logf_ref, qbd_ref, m_ref, l_ref, acc_ref, carry_ref = rest[3 * pps:]
    j = pl.program_id(1)
    scale = HEAD_D ** -0.5
    head_of_col = lax.broadcasted_iota(jnp.int32, (SUBLANES, GROUP_W), 1) // HEAD_D
    bd = head_of_col == lax.broadcasted_iota(jnp.int32, (SUBLANES, GROUP_W), 0)

    @pl.when(j == 0)
    def _():
        qbd_ref[...] = jnp.where(bd, jnp.broadcast_to(q_ref[...], (SUBLANES, GROUP_W)), 0.0).astype(BF16)
        m_ref[...] = jnp.full_like(m_ref, NEG_INF)
        l_ref[...] = jnp.zeros_like(l_ref)
        acc_ref[...] = jnp.zeros_like(acc_ref)
        carry_ref[...] = jnp.zeros_like(carry_ref)

    later = (lax.broadcasted_iota(jnp.int32, (PAGE_SIZE, PAGE_SIZE), 0)
             > lax.broadcasted_iota(jnp.int32, (PAGE_SIZE, PAGE_SIZE), 1)).astype(F32)
    def page(ref):
        flat = ref.reshape(PAGE_SIZE * N_HEADS, HEAD_D)
        return jnp.concatenate([flat[pl.ds(h, PAGE_SIZE, stride=N_HEADS), :] for h in range(N_HEADS)],
                               axis=1).astype(BF16)

    qbd = qbd_ref[...]
    run = carry_ref[...]
    lps = [lf_refs[r][...] for r in range(pps)]
    suffix = _dot_f32(jnp.concatenate(lps, axis=0), later)
    scores = []
    for r in range(pps):
        bias = run + suffix[r * SUBLANES:(r + 1) * SUBLANES]
        run = run + jnp.sum(lps[r], axis=1, keepdims=True)
        scores.append(_dot_nt(qbd, page(k_refs[r])) * scale + bias)
    carry_ref[...] = run
    m = m_ref[...]
    mn = m
    for s in scores:
        mn = jnp.maximum(mn, jnp.max(s, axis=-1, keepdims=True))
    al = jnp.exp(m - mn)
    l = al * l_ref[...]
    acc = al * acc_ref[...]
    for r, s in enumerate(scores):
        p = jnp.exp(s - mn)
        l = l + jnp.sum(p, axis=-1, keepdims=True)
        acc = acc + _dot(p.astype(BF16), page(v_refs[r]))
    m_ref[...], l_ref[...], acc_ref[...] = mn, l, acc

    @pl.when(j == pl.num_programs(1) - 1)
    def _():
        logf = _log_sigmoid(ff_ref[...] + bf_ref[...])
        logf_ref[...] = jnp.broadcast_to(logf, (SUBLANES, LANES))
        qf = qbd_ref[...].astype(F32)
        kn = kn_ref[...].astype(BF16).astype(F32)
        s_new = jnp.sum(qf * kn, axis=1, keepdims=True) * scale - logf
        m, l, acc = m_ref[...], l_ref[...], acc_ref[...]
        mn = jnp.maximum(m, s_new)
        p = jnp.exp(s_new - mn)
        al = jnp.exp(m - mn)
        out = (al * acc + p * vn_ref[...]) / (al * l + p)
        o_ref[...] = jnp.sum(jnp.where(bd, out, 0.0), axis=0, keepdims=True)


def _fox_caches(cache_k, cache_v, cache_lf):
    ck, cv = cache_k, cache_v
    clf = jnp.pad(jnp.transpose(cache_lf, (0, 1, 3, 2)), ((0, 0), (0, 0), (0, SUBLANES - N_HEADS), (0, 0)))
    return ck, cv, clf


def _fox_step(proj, ff, bf, caches, layer, page_table, anchor, *, nb, pps=8):
    ck, cv, clf = caches
    npages = page_table.shape[1]
    p3 = proj.reshape(nb, 1, proj.shape[1])
    ffc = jnp.pad(ff[:, :N_HEADS], ((0, 0), (0, SUBLANES - N_HEADS))).reshape(nb, SUBLANES, 1)
    bfc = jnp.pad(bf, (0, SUBLANES - N_HEADS)).reshape(SUBLANES, 1)
    row = lambda col: pl.BlockSpec((None, 1, GROUP_W), lambda b, j, pt: (b, 0, col))

    def page(shape, r):
        return pl.BlockSpec((None, None) + shape,
                            lambda b, j, pt: (layer, pt[b, npages - 1 - (j * pps + r)]) + (0,) * len(shape))

    in_specs = ([row(COL_FQ), row(COL_FK), row(COL_FV),
                 pl.BlockSpec((None, SUBLANES, 1), lambda b, j, pt: (b, 0, 0)),
                 pl.BlockSpec((SUBLANES, 1), lambda b, j, pt: (0, 0)),
                 pl.BlockSpec((SUBLANES, LANES), lambda b, j, pt: (0, 0))]
                + [page((PAGE_SIZE, N_HEADS, HEAD_D), r) for r in range(pps)]
                + [page((PAGE_SIZE, N_HEADS, HEAD_D), r) for r in range(pps)]
                + [page((SUBLANES, PAGE_SIZE), r) for r in range(pps)])
    y, logf = pl.pallas_call(
        functools.partial(_fox_step_kernel, pps=pps),
        grid_spec=pltpu.PrefetchScalarGridSpec(
            num_scalar_prefetch=1,
            grid=(nb, npages // pps),
            in_specs=in_specs,
            out_specs=[pl.BlockSpec((None, 1, GROUP_W), lambda b, j, pt: (b, 0, 0)),
                       pl.BlockSpec((None, SUBLANES, LANES), lambda b, j, pt: (b, 0, 0))],
            scratch_shapes=[pltpu.VMEM((SUBLANES, GROUP_W), BF16), pltpu.VMEM((SUBLANES, 1), F32),
                            pltpu.VMEM((SUBLANES, 1), F32), pltpu.VMEM((SUBLANES, GROUP_W), F32),
                            pltpu.VMEM((SUBLANES, 1), F32)]),
        out_shape=[jax.ShapeDtypeStruct((nb, 1, GROUP_W), F32), jax.ShapeDtypeStruct((nb, SUBLANES, LANES), F32)],
        compiler_params=_cparams(("arbitrary", "arbitrary"), 32),
        name="fox_step",
    )(page_table, p3, p3, p3, ffc, bfc, anchor, *([ck] * pps), *([cv] * pps), *([clf] * pps))
    return y.reshape(nb, GROUP_W), logf[:, :N_HEADS, 0]


N_TOP = PEER_TOPK + 1


def _sort_desc(v):
    v, n = list(v), len(v)
    k = 2
    while k <= n:
        j = k // 2
        while j >= 1:
            for i in range(n):
                l = i ^ j
                if l > i:
                    hi, lo = jnp.maximum(v[i], v[l]), jnp.minimum(v[i], v[l])
                    v[i], v[l] = (hi, lo) if (i & k) == 0 else (lo, hi)
            j //= 2
        k *= 2
    return v


def _top_values(s):
    depth = s.shape[0] // SUBLANES
    slabs = [s[r * SUBLANES:(r + 1) * SUBLANES, :] for r in range(depth)]
    slabs += [jnp.full_like(slabs[0], NEG_INF)] * (pl.next_power_of_2(depth) - depth)
    cols = _sort_desc(slabs)
    rows = lax.broadcasted_iota(jnp.int32, (24, 1), 0)
    top = jnp.full((24, s.shape[1]), NEG_INF, F32)
    for r in range(N_TOP):
        m = jnp.max(cols[0], axis=0, keepdims=True)
        top = jnp.where(rows == r, m, top)
        hit = cols[0] == m
        keep = min(len(cols), N_TOP - 1 - r)
        cols = [jnp.where(hit, cols[d + 1] if d + 1 < len(cols) else NEG_INF, cols[d]) for d in range(keep)]
    return top


def _peer_route_kernel(q_ref, sk_ref, s1_ref, s2_ref, tau_ref, *, tmr, t_total):
    live = lax.broadcasted_iota(jnp.int32, (tmr, 1), 0) < t_total - pl.program_id(0) * tmr
    for h in range(PEER_HEADS):
        q = jnp.where(live, q_ref[:, h * 2 * PEER_NKEYS:(h + 1) * 2 * PEER_NKEYS], 0.0).astype(BF16)
        _peer_route_head(q, sk_ref.at[h], s1_ref.at[h], s2_ref.at[h], tau_ref.at[h], tmr)


def _peer_route_head(q, sk_ref, s1_ref, s2_ref, tau_ref, tmr):
    s1 = _dot_nt(sk_ref[0], q[:, :PEER_NKEYS])
    s2 = _dot_nt(sk_ref[1], q[:, PEER_NKEYS:])
    a, b = _top_values(s1), _top_values(s2)
    r24 = lax.broadcasted_iota(jnp.int32, (24, 1), 0)
    r8 = lax.broadcasted_iota(jnp.int32, (8, 1), 0)
    a8, b8 = a[0:8], b[0:8]
    slabs = [a[0:1] + b, jnp.where(r24 >= 1, a + b[0:1], NEG_INF)]
    for i in range(1, 5):
        slabs.append(jnp.where((r8 >= 1) & (r8 < N_TOP // (i + 1)), a[i:i + 1] + b8, NEG_INF))
    slabs.append(jnp.where(r8 >= 5, a8 + b[1:2], NEG_INF))
    cand = jnp.concatenate(slabs, axis=0)

    ctop = _top_values(cand)
    t16, t17 = ctop[N_TOP - 2:N_TOP - 1], ctop[N_TOP - 1:N_TOP]
    m0 = a[0:1] + b[0:1]
    z = jnp.sum(jnp.where(cand >= t16, jnp.exp(cand - m0), 0.0), axis=0, keepdims=True)
    log2z = jnp.log2(z)
    s1n = (s1 - a[0:1]) * LOG2E
    s2n = (s2 - b[0:1]) * LOG2E - log2z
    taun = (0.5 * (t16 + t17) - m0) * LOG2E - log2z
    for q in range(tmr // LANES):
        lanes = slice(q * LANES, (q + 1) * LANES)
        s1_ref[q] = s1n[:, lanes]
        s2_ref[q] = s2n[:, lanes]
        tau_ref[q] = jnp.broadcast_to(taun[:, lanes], (SUBLANES, LANES))


def _peer_route(qp, subkeys):
    t = qp.shape[0]
    tmr = 2 * LANES if t > LANES else LANES
    ntile = pl.cdiv(t, tmr)
    k = tmr // LANES
    sblk = pl.BlockSpec((PEER_HEADS, k, PEER_NKEYS, LANES), lambda i: (0, i, 0, 0))
    return pl.pallas_call(
        functools.partial(_peer_route_kernel, tmr=tmr, t_total=t),
        grid=(ntile,),
        in_specs=[pl.BlockSpec((tmr, PEER_HEADS * 2 * PEER_NKEYS), lambda i: (i, 0)),
                  pl.BlockSpec((PEER_HEADS, 2, PEER_NKEYS, PEER_NKEYS), lambda i: (0, 0, 0, 0))],
        out_specs=[sblk, sblk, pl.BlockSpec((PEER_HEADS, k, SUBLANES, LANES), lambda i: (0, i, 0, 0))],
        out_shape=[jax.ShapeDtypeStruct((PEER_HEADS, ntile * k, PEER_NKEYS, LANES), F32)] * 2
        + [jax.ShapeDtypeStruct((PEER_HEADS, ntile * k, SUBLANES, LANES), F32)],
        compiler_params=_cparams(("arbitrary",), 32),
        name="peer_route",
    )(qp, subkeys)


def _peer_dense_kernel(x_ref, h_ref, u_ref, v_ref, s1_ref, s2_ref, tau_ref, o_ref, xs_ref, xu_ref, ht_ref,
                       *, tm, et, t_total):
    j = pl.program_id(1)

    @pl.when(j == 0)
    def _():
        live = lax.broadcasted_iota(jnp.int32, (tm, 1), 0) < t_total - pl.program_id(0) * tm
        o_ref[...] = jnp.where(live, h_ref[...], 0.0)
        xs_ref[...] = jnp.where(live, x_ref[...], jnp.zeros((), BF16))

    xu_ref[...] = _dot_nt(u_ref[...], xs_ref[...])
    na = et // PEER_NKEYS
    for ap in range(na):
        a_glob = j * na + ap
        rows = slice(ap * PEER_NKEYS, (ap + 1) * PEER_NKEYS)
        for lt in range(tm // LANES):
            lanes = slice(lt * LANES, (lt + 1) * LANES)
            gate = jnp.zeros((PEER_NKEYS, LANES), F32)
            for h in range(PEER_HEADS):
                v2 = s2_ref[h, lt] + s1_ref[h, lt, pl.ds(a_glob, 1), :]
                gate = gate + jnp.exp2(jnp.where(v2 >= tau_ref[h, lt, 0:1, :], v2, NEG_INF))
            ht_ref[rows, lanes] = (_gelu(xu_ref[rows, lanes]) * gate).astype(BF16)
    o_ref[...] += _dot_tn(ht_ref[...], v_ref[...])


def _peer_dense(xn, h, u_tab, v_tab, layer, s1n, s2n, tau):
    t, d = h.shape
    ne = u_tab.shape[1]
    tm = 5 * LANES if t > 5 * LANES else LANES
    et = 4 * PEER_NKEYS
    k = tm // LANES
    rblk = lambda rows: pl.BlockSpec((PEER_HEADS, k, rows, LANES), lambda i, j: (0, i, 0, 0))
    tab = pl.BlockSpec((None, et, d), lambda i, j: (layer, j, 0))
    return pl.pallas_call(
        functools.partial(_peer_dense_kernel, tm=tm, et=et, t_total=t),
        grid=(pl.cdiv(t, tm), ne // et),
        in_specs=[pl.BlockSpec((tm, d), lambda i, j: (i, 0)), pl.BlockSpec((tm, d), lambda i, j: (i, 0)),
                  tab, tab, rblk(PEER_NKEYS), rblk(PEER_NKEYS), rblk(SUBLANES)],
        out_specs=pl.BlockSpec((tm, d), lambda i, j: (i, 0)),
        out_shape=jax.ShapeDtypeStruct((t, d), F32),
        scratch_shapes=[pltpu.VMEM((tm, d), BF16), pltpu.VMEM((et, tm), F32), pltpu.VMEM((et, tm), BF16)],
        compiler_params=_cparams(("arbitrary", "arbitrary"), 56),
        name="peer_dense",
    )(xn, h, u_tab, v_tab, s1n, s2n, tau)


def _peer(h, norm_g, wq, subkeys, u_tab, v_tab, layer):
    t = h.shape[0]
    if t < LANES:
        h = jnp.pad(h, ((0, LANES - t), (0, 0)))
    qp, xn = _q_proj(h, norm_g, wq)
    s1n, s2n, tau = _peer_route(qp, subkeys)
    out = _peer_dense(xn, h, u_tab, v_tab, layer, s1n, s2n, tau)
    return out[:t]


def kernel(x_prompt, x_sample, cache_fox_k, cache_fox_v, cache_fox_logf, page_table, state_s5_re, state_s5_im, state_ret, state_lru, state_conv, meta_tokens, norm_mix_g, w_in, w_out, s5_lam_re, s5_lam_im, s5_log_dt, s5_b_re, s5_b_im, s5_c_re, s5_c_im, s5_d, s5_glu_w, s5_glu_b, fox_bf, ret_norm_g, lru_conv_w, lru_conv_b, lru_wa, lru_ba, lru_wx, lru_bx, lru_lam, norm_ffn_g, peer_wq, peer_subkeys, peer_u, peer_v, norm_final_g):
    bp, seq_x, d = x_prompt.shape
    bs = x_sample.shape[0]
    depth = w_in.shape[0]
    seq = seq_x + N_META
    meta = jnp.broadcast_to(meta_tokens[None], (bp, N_META, d))
    h_p = jnp.concatenate([meta, x_prompt], axis=1).reshape(bp * seq, d)
    h_s = x_sample.reshape(bs, d)
    zeros = lambda *shape: jnp.zeros(shape, F32)
    u_tab, v_tab = peer_u.astype(BF16), peer_v.astype(BF16)
    caches = _fox_caches(cache_fox_k, cache_fox_v, cache_fox_logf)
    w_main, w_ff = _repack_w_in(w_in)
    outs_p, outs_s = [], []
    for l in range(depth):
        w_o = w_out[l].astype(BF16).reshape(4, GROUP_W, d)
        s5w = _s5_weights(s5_lam_re[l], s5_lam_im[l], s5_log_dt[l], s5_b_re[l], s5_b_im[l], s5_c_re[l], s5_c_im[l])
        glu_w = s5_glu_w[l].astype(BF16)
        lru_w = (lru_conv_w[l], lru_conv_b[l], lru_wa[l], lru_ba[l], lru_wx[l], lru_bx[l], lru_lam[l])
        wq = peer_wq[l].astype(BF16)
        subkeys = peer_subkeys[l].astype(BF16)

        proj, ff = _in_proj(h_p, norm_mix_g[l], w_main, w_ff, l)
        y_s5, s5re, s5im = _s5(proj, zeros(bp, S5_STATE_W), zeros(bp, S5_STATE_W), s5w, s5_d[l], glu_w, s5_glu_b[l],
                               nb=bp, seq=seq)
        logf, c_blk = _fox_prep(ff, fox_bf[l], nb=bp, seq=seq)
        y_fox = _fox_prompt(proj, c_blk, nb=bp, seq=seq)
        y_ret, ret_s = _ret_prompt(proj, ret_norm_g[l], nb=bp, seq=seq)
        y_lru, lru_h, conv_buf = _lru(proj, zeros(bp, GROUP_W), zeros(bp, CONV_W - 1, GROUP_W), *lru_w, nb=bp, seq=seq)
        h_p = _out_proj(h_p, (y_s5, y_fox, y_ret, y_lru), w_o)
        h_p = _peer(h_p, norm_ffn_g[l], wq, subkeys, u_tab, v_tab, l)
        p3 = proj.reshape(bp, seq, -1)
        outs_p.append((p3[:, :, COL_FK * GROUP_W:(COL_FK + 1) * GROUP_W].reshape(bp, seq, N_HEADS, HEAD_D),
                       p3[:, :, COL_FV * GROUP_W:(COL_FV + 1) * GROUP_W].reshape(bp, seq, N_HEADS, HEAD_D),
                       logf, s5re.reshape(bp, -1, 64), s5im.reshape(bp, -1, 64), ret_s, lru_h, conv_buf))

        proj, ff = _in_proj(h_s, norm_mix_g[l], w_main, w_ff, l)
        y_s5, s5re, s5im = _s5(proj, state_s5_re[l].reshape(bs, S5_STATE_W), state_s5_im[l].reshape(bs, S5_STATE_W),
                               s5w, s5_d[l], glu_w, s5_glu_b[l], nb=bs, seq=1)
        y_fox, logf = _fox_step(proj, ff, fox_bf[l], caches, l, page_table, h_p, nb=bs)
        y_ret, ret_s = _ret_step(proj, ret_norm_g[l], state_ret[l], nb=bs)
        y_lru, lru_h, conv_buf = _lru(proj, state_lru[l], state_conv[l], *lru_w, nb=bs, seq=1)
        h_s = _out_proj(h_s, (y_s5, y_fox, y_ret, y_lru), w_o)
        h_s = _peer(h_s, norm_ffn_g[l], wq, subkeys, u_tab, v_tab, l)
        outs_s.append((proj[:, COL_FK * GROUP_W:(COL_FK + 1) * GROUP_W].reshape(bs, 1, N_HEADS, HEAD_D),
                       proj[:, COL_FV * GROUP_W:(COL_FV + 1) * GROUP_W].reshape(bs, 1, N_HEADS, HEAD_D),
                       logf.reshape(bs, 1, N_HEADS), s5re.reshape(bs, -1, 64), s5im.reshape(bs, -1, 64),
                       ret_s, lru_h, conv_buf))

    y_prompt = _rmsnorm(h_p, norm_final_g).reshape(bp, seq, d)[:, N_META:]
    y_sample = _rmsnorm(h_s, norm_final_g).reshape(bs, 1, d)
    stk = lambda outs, j: jnp.stack([o[j] for o in outs], axis=0)
    return ((y_prompt, y_sample) + tuple(stk(outs_p, j) for j in range(8)) + tuple(stk(outs_s, j) for j in range(8)))
```

```python
import functools
import math

import jax
import jax.numpy as jnp
from jax import lax
from jax.experimental import pallas as pl
from jax.experimental.pallas import tpu as pltpu

F32 = jnp.float32
BF16 = jnp.bfloat16
NEG_INF = float("-inf")

EPS = 1e-6
N_META = 16
PAST_LEN = 16384
PAGE_SIZE = 128
GROUP_W = 512
HEAD_D = 128
N_HEADS = 4
S5_STATE_W = 2048
LRU_C = 8.0
CONV_W = 4
PEER_HEADS = 8
PEER_NKEYS = 128
PEER_TOPK = 16
LANES = 128
SUBLANES = 8
MXU_W = 256
SCAN_UNROLL = 8
LOG2E = 1.4426950408889634
MIB = 1024 * 1024

COL_S5, COL_FQ, COL_FK, COL_FV, COL_RQ, COL_RK, COL_RV, COL_RG, COL_LX, COL_LG = range(10)


def _cparams(sem, vmem_mib, **kw):
    return pltpu.CompilerParams(dimension_semantics=sem, vmem_limit_bytes=vmem_mib * MIB, **kw)


def _dot(a, b):
    return jnp.dot(a, b, preferred_element_type=F32)


def _dot_nt(a, b):
    return lax.dot_general(a, b, (((1,), (1,)), ((), ())), preferred_element_type=F32)


def _dot_tn(a, b):
    return lax.dot_general(a, b, (((0,), (0,)), ((), ())), preferred_element_type=F32)


def _dot_f32(a, b):
    return jnp.dot(a, b, preferred_element_type=F32, precision=lax.Precision.HIGHEST)


def _gelu(x):
    return 0.5 * x * (1.0 + jnp.tanh(0.7978845608028654 * (x + 0.044715 * (x * x * x))))


def _row_tile(n, target):
    if n <= target:
        return n
    best = None
    for t in range(SUBLANES, target + 1, SUBLANES):
        if n % t == 0:
            best = t
    assert best is not None, n
    return best


def _in_proj_kernel(x_ref, g_ref, w_ref, wff_ref, o_ref, off_ref, xn_ref):
    @pl.when(pl.program_id(1) == 0)
    def _():
        x = x_ref[...]
        ms = jnp.mean(x * x, axis=-1, keepdims=True)
        xn = (x * lax.rsqrt(ms + EPS) * g_ref[...]).astype(BF16)
        xn_ref[...] = xn
        off_ref[...] = _dot(xn, wff_ref[...])
    o_ref[...] = _dot(xn_ref[...], w_ref[...])


def _repack_w_in_kernel(w_ref, main_ref, ff_ref):
    g0 = 4 * GROUP_W
    w = w_ref[...]
    main_ref[:, :g0] = w[:, :g0].astype(BF16)
    main_ref[:, g0:] = w[:, g0 + N_HEADS:].astype(BF16)
    lane = lax.broadcasted_iota(jnp.int32, (w.shape[0], LANES), 1)
    ff_ref[...] = jnp.where(lane < N_HEADS, w[:, g0:g0 + LANES], 0.0).astype(BF16)


def _repack_w_in(w_in):
    depth, d, n = w_in.shape
    rows = 256
    return pl.pallas_call(
        _repack_w_in_kernel,
        grid=(depth, d // rows),
        in_specs=[pl.BlockSpec((None, rows, n), lambda l, r: (l, r, 0))],
        out_specs=[pl.BlockSpec((None, rows, n - N_HEADS), lambda l, r: (l, r, 0)),
                   pl.BlockSpec((None, rows, LANES), lambda l, r: (l, r, 0))],
        out_shape=[jax.ShapeDtypeStruct((depth, d, n - N_HEADS), BF16), jax.ShapeDtypeStruct((depth, d, LANES), BF16)],
        compiler_params=_cparams(("arbitrary", "arbitrary"), 32),
        name="repack_w_in",
    )(w_in)


def _in_proj(x, g, w_main, w_ff, layer):
    t, d = x.shape
    n = w_main.shape[2]
    tm, tn = _row_tile(t, 1032), 512
    return pl.pallas_call(
        _in_proj_kernel,
        grid=(t // tm, n // tn),
        in_specs=[pl.BlockSpec((tm, d), lambda i, j: (i, 0)),
                  pl.BlockSpec((1, d), lambda i, j: (0, 0)),
                  pl.BlockSpec((None, d, tn), lambda i, j: (layer, 0, j)),
                  pl.BlockSpec((None, d, LANES), lambda i, j: (layer, 0, 0))],
        out_specs=[pl.BlockSpec((tm, tn), lambda i, j: (i, j)),
                   pl.BlockSpec((tm, LANES), lambda i, j: (i, 0))],
        out_shape=[jax.ShapeDtypeStruct((t, n), F32), jax.ShapeDtypeStruct((t, LANES), F32)],
        scratch_shapes=[pltpu.VMEM((tm, d), BF16)],
        compiler_params=_cparams(("arbitrary", "arbitrary"), 48),
        name="in_proj",
    )(x, g.reshape(1, d), w_main, w_ff)


def _q_proj_kernel(x_ref, g_ref, w_ref, o_ref, xn_ref):
    @pl.when(pl.program_id(1) == 0)
    def _():
        x = x_ref[...]
        ms = jnp.mean(x * x, axis=-1, keepdims=True)
        xn_ref[...] = (x * lax.rsqrt(ms + EPS) * g_ref[...]).astype(BF16)
    o_ref[...] = _dot(xn_ref[...], w_ref[...])


def _q_proj(x, g, w):
    t, d = x.shape
    n = w.shape[1]
    tm, tn = _row_tile(t, 1032), 512
    return pl.pallas_call(
        _q_proj_kernel,
        grid=(t // tm, n // tn),
        in_specs=[pl.BlockSpec((tm, d), lambda i, j: (i, 0)),
                  pl.BlockSpec((1, d), lambda i, j: (0, 0)),
                  pl.BlockSpec((d, tn), lambda i, j: (0, j))],
        out_specs=[pl.BlockSpec((tm, tn), lambda i, j: (i, j)),
                   pl.BlockSpec((tm, d), lambda i, j: (i, 0))],
        out_shape=[jax.ShapeDtypeStruct((t, n), F32), jax.ShapeDtypeStruct((t, d), BF16)],
        compiler_params=_cparams(("arbitrary", "arbitrary"), 48),
        name="q_proj",
    )(x, g.reshape(1, d), w)


def _out_proj_kernel(h_ref, y0_ref, y1_ref, y2_ref, y3_ref, w_ref, o_ref):
    acc = h_ref[...]
    for gi, y_ref in enumerate((y0_ref, y1_ref, y2_ref, y3_ref)):
        acc = acc + _dot(y_ref[...].astype(BF16), w_ref[gi])
    o_ref[...] = acc


def _out_proj(h, ys, w):
    t, d = h.shape
    tm, tn = _row_tile(t, 1032), 512
    yspec = pl.BlockSpec((tm, GROUP_W), lambda i, j: (i, 0))
    return pl.pallas_call(
        _out_proj_kernel,
        grid=(t // tm, d // tn),
        in_specs=[pl.BlockSpec((tm, tn), lambda i, j: (i, j)), yspec, yspec, yspec, yspec,
                  pl.BlockSpec((4, GROUP_W, tn), lambda i, j: (0, 0, j))],
        out_specs=pl.BlockSpec((tm, tn), lambda i, j: (i, j)),
        out_shape=jax.ShapeDtypeStruct((t, d), F32),
        compiler_params=_cparams(("arbitrary", "arbitrary"), 48),
        name="out_proj",
    )(h, *ys, w)


def _rmsnorm_kernel(x_ref, g_ref, o_ref):
    x = x_ref[...]
    ms = jnp.mean(x * x, axis=-1, keepdims=True)
    o_ref[...] = x * lax.rsqrt(ms + EPS) * g_ref[...]


def _rmsnorm(x, g):
    t, d = x.shape
    tm = _row_tile(t, 1032)
    return pl.pallas_call(
        _rmsnorm_kernel,
        grid=(t // tm,),
        in_specs=[pl.BlockSpec((tm, d), lambda i: (i, 0)), pl.BlockSpec((1, d), lambda i: (0, 0))],
        out_specs=pl.BlockSpec((tm, d), lambda i: (i, 0)),
        out_shape=jax.ShapeDtypeStruct((t, d), F32),
        compiler_params=_cparams(("arbitrary",), 48),
        name="final_norm",
    )(x, g.reshape(1, d))


def _s5_kernel(u_ref, h0re_ref, h0im_ref, lre_ref, lim_ref, bw_ref, cw_ref, d_ref, gw_ref, gb_ref,
               y_ref, hre_ref, him_ref, sre_ref, sim_ref, *, nb, tc):
    single = tc == 1

    @pl.when(pl.program_id(0) == 0)
    def _():
        hre_ref[...] = h0re_ref[...]
        him_ref[...] = h0im_ref[...]

    def get_u(b):
        return u_ref[...] if single else u_ref[b]

    seqs = (0,) if single else tuple(range(nb))
    nrow = nb if single else tc
    for b in seqs:
        ub = get_u(b).astype(BF16)
        for r in range(4):
            bu = _dot(ub[:, r * LANES:(r + 1) * LANES], bw_ref[r])
            for q in range(4):
                sre_ref[4 * r + q, b * nrow:(b + 1) * nrow, :] = bu[:, q * LANES:(q + 1) * LANES]
                sim_ref[4 * r + q, b * nrow:(b + 1) * nrow, :] = bu[:, GROUP_W + q * LANES:GROUP_W + (q + 1) * LANES]

    for r in range(4):
        tiles = tuple(range(4 * r, 4 * r + 4))
        lanes = [slice(lt * LANES, (lt + 1) * LANES) for lt in tiles]
        lr = [jnp.broadcast_to(lre_ref[:, ln], (nb, LANES)) for ln in lanes]
        li = [jnp.broadcast_to(lim_ref[:, ln], (nb, LANES)) for ln in lanes]

        def body(t, carry, tiles=tiles, lr=lr, li=li):
            idx = pl.ds(t, nb, stride=tc)
            new = []
            for q, lt in enumerate(tiles):
                hr, hi = carry[q]
                nr = lr[q] * hr - li[q] * hi + sre_ref[lt, idx, :]
                ni = lr[q] * hi + li[q] * hr + sim_ref[lt, idx, :]
                sre_ref[lt, idx, :] = nr
                sim_ref[lt, idx, :] = ni
                new.append((nr, ni))
            return tuple(new)

        fin = lax.fori_loop(0, tc, body, tuple((hre_ref[:, ln], him_ref[:, ln]) for ln in lanes),
                            unroll=min(tc, SCAN_UNROLL))
        for q, ln in enumerate(lanes):
            hre_ref[:, ln] = fin[q][0]
            him_ref[:, ln] = fin[q][1]

    for b in seqs:
        rows = slice(b * nrow, (b + 1) * nrow)
        ys = []
        for r in range(4):
            hre = jnp.concatenate([sre_ref[4 * r + q, rows, :] for q in range(4)], axis=1).astype(BF16)
            him = jnp.concatenate([sim_ref[4 * r + q, rows, :] for q in range(4)], axis=1).astype(BF16)
            ys.append(_dot(hre, cw_ref[r, :GROUP_W, :]) + _dot(him, cw_ref[r, GROUP_W:, :]))
        y = jnp.concatenate(ys, axis=1) + d_ref[...] * get_u(b)
        y = _gelu(y)
        z = _dot(y.astype(BF16), gw_ref[...]) + gb_ref[...]
        out = y * jax.nn.sigmoid(z)
        if single:
            y_ref[...] = out
        else:
            y_ref[b] = out


def _s5_weights(lam_re, lam_im, log_dt, b_re, b_im, c_re, c_im):
    lr, li = jnp.minimum(lam_re, -1e-4), lam_im
    dt = jnp.exp(log_dt)[:, None]
    mag = jnp.exp(lr * dt)
    bar_re, bar_im = mag * jnp.cos(li * dt), mag * jnp.sin(li * dt)
    den = lr * lr + li * li
    f_re = ((bar_re - 1.0) * lr + bar_im * li) / den
    f_im = (bar_im * lr - (bar_re - 1.0) * li) / den
    bb_re = f_re[:, :, None] * b_re - f_im[:, :, None] * b_im
    bb_im = f_re[:, :, None] * b_im + f_im[:, :, None] * b_re
    eye = jnp.eye(8, dtype=F32)

    def pack_b(m):
        m = m.reshape(4, 8, 64, 16)
        return jnp.einsum("rgpc,gh->rgchp", m, eye).reshape(4, LANES, GROUP_W)

    def pack_c(m):
        m = m.reshape(4, 8, 16, 64)
        return jnp.einsum("rgcp,hg->rhpgc", m, eye).reshape(4, GROUP_W, LANES)

    bw = jnp.concatenate([pack_b(bb_re), pack_b(bb_im)], axis=2).astype(BF16)
    cw = jnp.concatenate([pack_c(c_re), -pack_c(c_im)], axis=1).astype(BF16)
    return (bar_re.reshape(1, S5_STATE_W), bar_im.reshape(1, S5_STATE_W), bw, cw)


def _s5(proj, h0re, h0im, wts, d, glu_w, glu_b, *, nb, seq):
    lre, lim, bw, cw = wts
    single = seq == 1
    tc = 1 if single else _row_tile(seq, 344)
    nchunk = seq // tc
    full = lambda shape: pl.BlockSpec(shape, lambda c: tuple(0 for _ in shape))
    if single:
        u_in = proj
        u_spec = pl.BlockSpec((nb, GROUP_W), lambda c: (0, COL_S5))
        y_spec = pl.BlockSpec((nb, GROUP_W), lambda c: (0, 0))
        y_shape = jax.ShapeDtypeStruct((nb, GROUP_W), F32)
    else:
        u_in = proj.reshape(nb, seq, proj.shape[1])
        u_spec = pl.BlockSpec((nb, tc, GROUP_W), lambda c: (0, c, COL_S5))
        y_spec = pl.BlockSpec((nb, tc, GROUP_W), lambda c: (0, c, 0))
        y_shape = jax.ShapeDtypeStruct((nb, seq, GROUP_W), F32)
    y, hre, him = pl.pallas_call(
        functools.partial(_s5_kernel, nb=nb, tc=tc),
        grid=(nchunk,),
        in_specs=[u_spec, full((nb, S5_STATE_W)), full((nb, S5_STATE_W)), full((1, S5_STATE_W)),
                  full((1, S5_STATE_W)), full((4, LANES, 2 * GROUP_W)), full((4, 2 * GROUP_W, LANES)),
                  full((1, GROUP_W)), full((GROUP_W, GROUP_W)), full((1, GROUP_W))],
        out_specs=[y_spec, full((nb, S5_STATE_W)), full((nb, S5_STATE_W))],
        out_shape=[y_shape, jax.ShapeDtypeStruct((nb, S5_STATE_W), F32),
                   jax.ShapeDtypeStruct((nb, S5_STATE_W), F32)],
        scratch_shapes=[pltpu.VMEM((S5_STATE_W // LANES, nb * tc, LANES), F32)] * 2,
        compiler_params=_cparams(("arbitrary",), 56),
        name="s5",
    )(u_in, h0re, h0im, lre, lim, bw, cw, d.reshape(1, GROUP_W), glu_w, glu_b.reshape(1, GROUP_W))
    return y.reshape(nb * seq, GROUP_W), hre, him


def _softplus(x):
    return jnp.maximum(x, 0.0) + jnp.log1p(jnp.exp(-jnp.abs(x)))


def _lru_gates(conv, wa_ref, wx_ref, ba_ref, bx_ref, lam_ref):
    cb = conv.astype(BF16)
    r = jax.nn.sigmoid(_dot(cb, wa_ref[...]) + ba_ref[...])
    i = jax.nn.sigmoid(_dot(cb, wx_ref[...]) + bx_ref[...])
    log_a = -LRU_C * r * _softplus(-lam_ref[...])
    a = jnp.exp(log_a)
    return a, jnp.sqrt(-jnp.tanh(log_a) * (a * a + 1.0)) * (i * conv)


def _lru_kernel(x_ref, g_ref, h0_ref, buf0_ref, cw_ref, cb_ref, wa_ref, ba_ref, wx_ref, bx_ref, lam_ref,
                y_ref, h_ref, buf_ref, xe_ref, sa_ref, sx_ref, *, nb, tc):
    nq = GROUP_W // LANES

    @pl.when(pl.program_id(0) == 0)
    def _():
        h_ref[...] = h0_ref[...]
        for b in range(nb):
            xe_ref[b, 5:8, :] = buf0_ref[b]

    for b in range(nb):
        xe_ref[b, 8:8 + tc, :] = x_ref[b]
    for b in range(nb):
        conv = cb_ref[...] + sum(xe_ref[b, 5 + j:5 + j + tc, :] * cw_ref[j:j + 1, :] for j in range(CONV_W))
        a, xin = _lru_gates(conv, wa_ref, wx_ref, ba_ref, bx_ref, lam_ref)
        for q in range(nq):
            sa_ref[q, b * tc:(b + 1) * tc, :] = a[:, q * LANES:(q + 1) * LANES]
            sx_ref[q, b * tc:(b + 1) * tc, :] = xin[:, q * LANES:(q + 1) * LANES]

    def body(t, hs):
        idx = pl.ds(t, nb, stride=tc)
        new = []
        for q in range(nq):
            h = sa_ref[q, idx, :] * hs[q] + sx_ref[q, idx, :]
            sx_ref[q, idx, :] = h
            new.append(h)
        return tuple(new)

    fin = lax.fori_loop(0, tc, body, tuple(h_ref[:, q * LANES:(q + 1) * LANES] for q in range(nq)),
                        unroll=min(tc, SCAN_UNROLL))
    for q in range(nq):
        h_ref[:, q * LANES:(q + 1) * LANES] = fin[q]
    for b in range(nb):
        hseq = jnp.concatenate([sx_ref[q, b * tc:(b + 1) * tc, :] for q in range(nq)], axis=1)
        y_ref[b] = hseq * _gelu(g_ref[b])
        tail = xe_ref[b, tc + 5:tc + 8, :]
        xe_ref[b, 5:8, :] = tail
        buf_ref[b] = tail


def _lru_step_kernel(x_ref, g_ref, h0_ref, buf0_ref, cw_ref, cb_ref, wa_ref, ba_ref, wx_ref, bx_ref, lam_ref,
                     y_ref, h_ref, buf_ref):
    x = x_ref[...]
    conv = cb_ref[...] + x * cw_ref[3:4, :] + sum(buf0_ref[j] * cw_ref[j:j + 1, :] for j in range(CONV_W - 1))
    a, xin = _lru_gates(conv, wa_ref, wx_ref, ba_ref, bx_ref, lam_ref)
    h = a * h0_ref[...] + xin
    h_ref[...] = h
    y_ref[...] = h * _gelu(g_ref[...])
    buf_ref[0] = buf0_ref[1]
    buf_ref[1] = buf0_ref[2]
    buf_ref[2] = x


def _block_diag(w):
    return jnp.einsum("hij,hg->higj", w, jnp.eye(8, dtype=w.dtype)).reshape(GROUP_W, GROUP_W)


def _lru(proj, h0, buf0, conv_w, conv_b, wa, ba, wx, bx, lam, *, nb, seq):
    single = seq == 1
    row = lambda v: v.reshape(1, GROUP_W)
    wts = (conv_w, row(conv_b), _block_diag(wa).astype(BF16), row(ba), _block_diag(wx).astype(BF16), row(bx), row(lam))
    full = lambda shape: pl.BlockSpec(shape, lambda c: tuple(0 for _ in shape))
    wspecs = [full((CONV_W, GROUP_W)), full((1, GROUP_W)), full((GROUP_W, GROUP_W)), full((1, GROUP_W)),
              full((GROUP_W, GROUP_W)), full((1, GROUP_W)), full((1, GROUP_W))]
    if single:
        y, h, buf = pl.pallas_call(
            _lru_step_kernel,
            grid=(1,),
            in_specs=[pl.BlockSpec((nb, GROUP_W), lambda c: (0, COL_LX)), pl.BlockSpec((nb, GROUP_W), lambda c: (0, COL_LG)),
                      full((nb, GROUP_W)), full((CONV_W - 1, nb, GROUP_W))] + wspecs,
            out_specs=[full((nb, GROUP_W)), full((nb, GROUP_W)), full((CONV_W - 1, nb, GROUP_W))],
            out_shape=[jax.ShapeDtypeStruct((nb, GROUP_W), F32), jax.ShapeDtypeStruct((nb, GROUP_W), F32),
                       jax.ShapeDtypeStruct((CONV_W - 1, nb, GROUP_W), F32)],
            compiler_params=_cparams(("arbitrary",), 32),
            name="lru_step",
        )(proj, proj, h0, jnp.transpose(buf0, (1, 0, 2)), *wts)
        return y, h, jnp.transpose(buf, (1, 0, 2))
    tc = _row_tile(seq, 344)
    p3 = proj.reshape(nb, seq, proj.shape[1])
    y, h, buf = pl.pallas_call(
        functools.partial(_lru_kernel, nb=nb, tc=tc),
        grid=(seq // tc,),
        in_specs=[pl.BlockSpec((nb, tc, GROUP_W), lambda c: (0, c, COL_LX)),
                  pl.BlockSpec((nb, tc, GROUP_W), lambda c: (0, c, COL_LG)),
                  full((nb, GROUP_W)), full((nb, CONV_W - 1, GROUP_W))] + wspecs,
        out_specs=[pl.BlockSpec((nb, tc, GROUP_W), lambda c: (0, c, 0)), full((nb, GROUP_W)),
                   full((nb, CONV_W - 1, GROUP_W))],
        out_shape=[jax.ShapeDtypeStruct((nb, seq, GROUP_W), F32), jax.ShapeDtypeStruct((nb, GROUP_W), F32),
                   jax.ShapeDtypeStruct((nb, CONV_W - 1, GROUP_W), F32)],
        scratch_shapes=[pltpu.VMEM((nb, tc + 8, GROUP_W), F32), pltpu.VMEM((GROUP_W // LANES, nb * tc, LANES), F32),
                        pltpu.VMEM((GROUP_W // LANES, nb * tc, LANES), F32)],
        compiler_params=_cparams(("arbitrary",), 48),
        name="lru",
    )(p3, p3, h0, buf0, *wts)
    return y.reshape(nb * seq, GROUP_W), h, buf


def _ret_log_gamma(h):
    return math.log(1.0 - 2.0 ** (-5.0 - h))


def _rotary_tables(pos):
    half = HEAD_D // 2
    inv = 1.0 / (10000.0 ** (jnp.arange(half, dtype=F32) / half))
    ang = pos.astype(F32)[:, None] * inv[None, :]
    cos, sin = jnp.cos(ang), jnp.sin(ang)
    return jnp.concatenate([cos, cos], axis=1), jnp.concatenate([-sin, sin], axis=1)


def _rotate(x, cos2, sin2):
    return x * cos2 + pltpu.roll(x, HEAD_D // 2, 1) * sin2


def _ret_kernel(q_ref, k_ref, v_ref, g_ref, cos_ref, sin_ref, ng_ref, y_ref, s_ref, *, seq):
    c = pl.program_id(1)

    @pl.when(c == 0)
    def _():
        s_ref[...] = jnp.zeros_like(s_ref)

    nv = jnp.minimum(LANES, seq - c * LANES)
    nvf = nv.astype(F32)
    row = lax.broadcasted_iota(jnp.int32, (LANES, 1), 0)
    rowf = row.astype(F32)
    valid = row < nv
    diff = (lax.broadcasted_iota(jnp.int32, (LANES, LANES), 0)
            - lax.broadcasted_iota(jnp.int32, (LANES, LANES), 1)).astype(F32)
    cos2, sin2 = cos_ref[...], sin_ref[...]
    for h in range(N_HEADS):
        lg = _ret_log_gamma(h)
        cols = slice(h * HEAD_D, (h + 1) * HEAD_D)
        decay = jnp.exp(jnp.where(diff >= 0, diff * lg, NEG_INF))
        q_dec = jnp.exp((rowf + 1.0) * lg)
        k_dec = jnp.exp((nvf - 1.0 - rowf) * lg)
        c_dec = jnp.exp(jnp.full((1, 1), lg, F32) * nvf)
        q = jnp.where(valid, q_ref[:, cols], 0.0)
        k = jnp.where(valid, k_ref[:, cols], 0.0)
        v = jnp.where(valid, v_ref[:, cols], 0.0)
        qr = _rotate(q, cos2, sin2)
        kr = _rotate(k, cos2, sin2) * (HEAD_D ** -0.5)
        vb = v.astype(BF16)
        att = _dot_nt(qr.astype(BF16), kr.astype(BF16)) * decay
        s_old = s_ref[h]
        o = _dot(att.astype(BF16), vb) + _dot((qr * q_dec).astype(BF16), s_old.astype(BF16))
        s_ref[h] = s_old * c_dec + _dot_tn((kr * k_dec).astype(BF16), vb)
        o = o * lax.rsqrt(jnp.mean(o * o, axis=-1, keepdims=True) + EPS) * ng_ref[:, cols]
        g = jnp.where(valid, g_ref[:, cols], 0.0)
        y_ref[:, cols] = o * (g * jax.nn.sigmoid(g))


def _ret_prompt(proj, norm_g, *, nb, seq):
    p3 = proj.reshape(nb, seq, proj.shape[1])
    nchunk = pl.cdiv(seq, LANES)
    cos2, sin2 = _rotary_tables(jnp.arange(nchunk * LANES))
    blk = lambda col: pl.BlockSpec((None, LANES, GROUP_W), lambda b, c: (b, c, col))
    tab = pl.BlockSpec((LANES, HEAD_D), lambda b, c: (c, 0))
    y, s = pl.pallas_call(
        functools.partial(_ret_kernel, seq=seq),
        grid=(nb, nchunk),
        in_specs=[blk(COL_RQ), blk(COL_RK), blk(COL_RV), blk(COL_RG), tab, tab,
                  pl.BlockSpec((1, GROUP_W), lambda b, c: (0, 0))],
        out_specs=[pl.BlockSpec((None, LANES, GROUP_W), lambda b, c: (b, c, 0)),
                   pl.BlockSpec((None, N_HEADS, HEAD_D, HEAD_D), lambda b, c: (b, 0, 0, 0))],
        out_shape=[jax.ShapeDtypeStruct((nb, seq, GROUP_W), F32),
                   jax.ShapeDtypeStruct((nb, N_HEADS, HEAD_D, HEAD_D), F32)],
        compiler_params=_cparams(("arbitrary", "arbitrary"), 32),
        name="retention",
    )(p3, p3, p3, p3, cos2, sin2, norm_g.reshape(1, GROUP_W))
    return y.reshape(nb * seq, GROUP_W), s


def _ret_step_kernel(q_ref, k_ref, v_ref, g_ref, cos_ref, sin_ref, ng_ref, s0_ref, y_ref, s_ref):
    cos2, sin2 = cos_ref[...], sin_ref[...]
    eye = (lax.broadcasted_iota(jnp.int32, (HEAD_D, HEAD_D), 0)
           == lax.broadcasted_iota(jnp.int32, (HEAD_D, HEAD_D), 1))
    for h in range(N_HEADS):
        gamma = 1.0 - 2.0 ** (-5.0 - h)
        cols = slice(h * HEAD_D, (h + 1) * HEAD_D)
        qr = _rotate(q_ref[:, cols], cos2, sin2)
        kr = _rotate(k_ref[:, cols], cos2, sin2) * (HEAD_D ** -0.5)
        v = v_ref[:, cols]
        s0 = s0_ref[h]
        qs = _dot_f32(jnp.broadcast_to(qr, (SUBLANES, HEAD_D)), s0)[0:1]
        o = jnp.sum(qr * kr, axis=-1, keepdims=True) * v + gamma * qs
        kcol = jnp.sum(jnp.where(eye, jnp.broadcast_to(kr, (HEAD_D, HEAD_D)), 0.0), axis=1, keepdims=True)
        s_ref[h] = gamma * s0 + kcol * v
        o = o * lax.rsqrt(jnp.mean(o * o, axis=-1, keepdims=True) + EPS) * ng_ref[:, cols]
        g = g_ref[:, cols]
        y_ref[:, cols] = o * (g * jax.nn.sigmoid(g))


def _ret_step(proj, norm_g, s0, *, nb):
    p3 = proj.reshape(nb, 1, proj.shape[1])
    cos2, sin2 = _rotary_tables(jnp.full((1,), PAST_LEN))
    blk = lambda col: pl.BlockSpec((None, 1, GROUP_W), lambda b: (b, 0, col))
    one = lambda n: pl.BlockSpec((1, n), lambda b: (0, 0))
    st = pl.BlockSpec((None, N_HEADS, HEAD_D, HEAD_D), lambda b: (b, 0, 0, 0))
    y, s = pl.pallas_call(
        _ret_step_kernel,
        grid=(nb,),
        in_specs=[blk(COL_RQ), blk(COL_RK), blk(COL_RV), blk(COL_RG), one(HEAD_D), one(HEAD_D), one(GROUP_W), st],
        out_specs=[pl.BlockSpec((None, 1, GROUP_W), lambda b: (b, 0, 0)), st],
        out_shape=[jax.ShapeDtypeStruct((nb, 1, GROUP_W), F32),
                   jax.ShapeDtypeStruct((nb, N_HEADS, HEAD_D, HEAD_D), F32)],
        compiler_params=_cparams(("arbitrary",), 32),
        name="retention_step",
    )(p3, p3, p3, p3, cos2, sin2, norm_g.reshape(1, GROUP_W), s0)
    return y.reshape(nb, GROUP_W), s


def _log_sigmoid(x):
    return jnp.minimum(x, 0.0) - jnp.log1p(jnp.exp(-jnp.abs(x)))


def _fox_prep_kernel(ff_ref, bf_ref, logf_ref, c_ref):
    logf = _log_sigmoid(ff_ref[...] + bf_ref[...])
    logf_ref[...] = logf
    tri = (lax.broadcasted_iota(jnp.int32, (LANES, LANES), 0)
           <= lax.broadcasted_iota(jnp.int32, (LANES, LANES), 1)).astype(F32)
    carry = jnp.zeros((SUBLANES, 1), F32)
    for j in range(logf.shape[1] // LANES):
        blk = logf[:, j * LANES:(j + 1) * LANES]
        c_ref[:, j * LANES:(j + 1) * LANES] = carry + _dot_f32(blk, tri)
        carry = carry + jnp.sum(blk, axis=1, keepdims=True)


FOX_TK = 512
FOX_TQ = 344


def _fox_prep(ff, bf, *, nb, seq):
    nblk = pl.cdiv(seq, FOX_TK)
    lp = nblk * FOX_TK
    fft = jnp.transpose(ff[:, :N_HEADS].reshape(nb, seq, N_HEADS), (0, 2, 1))
    fft = jnp.pad(fft, ((0, 0), (0, SUBLANES - N_HEADS), (0, lp - seq)))
    bfc = jnp.pad(bf, (0, SUBLANES - N_HEADS)).reshape(SUBLANES, 1)
    spec = pl.BlockSpec((None, SUBLANES, lp), lambda b: (b, 0, 0))
    logf_t, c_t = pl.pallas_call(
        _fox_prep_kernel,
        grid=(nb,),
        in_specs=[spec, pl.BlockSpec((SUBLANES, 1), lambda b: (0, 0))],
        out_specs=[spec, spec],
        out_shape=[jax.ShapeDtypeStruct((nb, SUBLANES, lp), F32)] * 2,
        compiler_params=_cparams(("arbitrary",), 56),
        name="fox_prep",
    )(fft, bfc)
    logf = jnp.transpose(logf_t[:, :N_HEADS, :seq], (0, 2, 1))
    c_blk = jnp.transpose(c_t.reshape(nb, SUBLANES, nblk, FOX_TK), (0, 2, 1, 3))
    return logf, c_blk


def _softmax_update(carry, s, vb):
    m, l, acc = carry
    mn = jnp.maximum(m, jnp.max(s, axis=-1, keepdims=True))
    p = jnp.exp(s - mn)
    al = jnp.exp(m - mn)
    return mn, al * l + jnp.sum(p, axis=-1, keepdims=True), al * acc + _dot(p.astype(BF16), vb)


def _fox_attn_kernel(q_ref, k_ref, v_ref, c_ref, o_ref, kb_ref, vb_ref, m_ref, l_ref, acc_ref, *, seq):
    i = pl.program_id(1)
    tk = FOX_TK
    nfull = seq // tk
    rem = seq - nfull * tk
    scale = HEAD_D ** -0.5

    @pl.when(i == 0)
    def _():
        kb_ref[...] = k_ref[...].astype(BF16)
        vb_ref[...] = v_ref[...].astype(BF16)

    m_ref[...] = jnp.full_like(m_ref, NEG_INF)
    l_ref[...] = jnp.zeros_like(l_ref)
    acc_ref[...] = jnp.zeros_like(acc_ref)
    tq = o_ref.shape[0]
    qpos = i * tq + lax.broadcasted_iota(jnp.int32, (tq, 1), 0)
    head_cols = [slice(h * HEAD_D, (h + 1) * HEAD_D) for h in range(N_HEADS)]
    live = qpos < seq
    qbs = [jnp.where(live, q_ref[:, cols], 0.0).astype(BF16) for cols in head_cols]

    def tile(rows, cj, kpos):
        mask = kpos <= qpos
        heads = range(N_HEADS)
        ss = [jnp.where(mask, _dot_nt(qbs[h], kb_ref[rows, head_cols[h]]) * scale - cj[h:h + 1, :], NEG_INF)
              for h in heads]
        ms = [m_ref[h] for h in heads]
        mns = [jnp.maximum(ms[h], jnp.max(ss[h], axis=-1, keepdims=True)) for h in heads]
        ps = [jnp.exp(ss[h] - mns[h]) for h in heads]
        als = [jnp.exp(ms[h] - mns[h]) for h in heads]
        pvs = [_dot(ps[h].astype(BF16), vb_ref[rows, head_cols[h]]) for h in heads]
        for h in heads:
            m_ref[h] = mns[h]
            l_ref[h] = als[h] * l_ref[h] + jnp.sum(ps[h], axis=-1, keepdims=True)
            acc_ref[h] = als[h] * acc_ref[h] + pvs[h]

    def body(j, carry):
        r0 = pl.multiple_of(j * tk, tk)
        tile(pl.ds(r0, tk), c_ref[j], j * tk + lax.broadcasted_iota(jnp.int32, (1, tk), 1))
        return carry

    lax.fori_loop(0, jnp.minimum((i * tq + tq + tk - 1) // tk, nfull), body, 0)
    if rem:
        @pl.when(i * tq + tq > nfull * tk)
        def _():
            tile(slice(nfull * tk, seq), c_ref[nfull][:, 0:rem],
                 nfull * tk + lax.broadcasted_iota(jnp.int32, (1, rem), 1))
    for h, cols in enumerate(head_cols):
        o_ref[:, cols] = acc_ref[h] / l_ref[h]


def _fox_prompt(proj, c_blk, *, nb, seq):
    p3 = proj.reshape(nb, seq, proj.shape[1])
    nkt = c_blk.shape[1]
    tq = _row_tile(seq, FOX_TQ)
    kv = lambda col: pl.BlockSpec((None, seq, GROUP_W), lambda b, i: (b, 0, col))
    y = pl.pallas_call(
        functools.partial(_fox_attn_kernel, seq=seq),
        grid=(nb, seq // tq),
        in_specs=[pl.BlockSpec((None, tq, GROUP_W), lambda b, i: (b, i, COL_FQ)), kv(COL_FK), kv(COL_FV),
                  pl.BlockSpec((None, nkt, SUBLANES, FOX_TK), lambda b, i: (b, 0, 0, 0))],
        out_specs=pl.BlockSpec((None, tq, GROUP_W), lambda b, i: (b, i, 0)),
        out_shape=jax.ShapeDtypeStruct((nb, seq, GROUP_W), F32),
        scratch_shapes=[pltpu.VMEM((seq, GROUP_W), BF16), pltpu.VMEM((seq, GROUP_W), BF16),
                        pltpu.VMEM((N_HEADS, tq, 1), F32), pltpu.VMEM((N_HEADS, tq, 1), F32),
                        pltpu.VMEM((N_HEADS, tq, HEAD_D), F32)],
        compiler_params=_cparams(("arbitrary", "arbitrary"), 48),
        name="fox_attn",
    )(p3, p3, p3, c_blk)
    return y.reshape(nb * seq, GROUP_W)


def _fox_step_kernel(pt_ref, q_ref, kn_ref, vn_ref, ff_ref, bf_ref, anchor_ref, *rest, pps):
    del anchor_ref
    k_refs, v_refs, lf_refs = rest[:pps], rest[pps:2 * pps], rest[2 * pps:3 * pps]
    o_ref, logf_ref, qbd_ref, m_ref, l_ref, acc_ref, carry_ref = rest[3 * pps:]
    j = pl.program_id(1)
    scale = HEAD_D ** -0.5
    head_of_col = lax.broadcasted_iota(jnp.int32, (SUBLANES, GROUP_W), 1) // HEAD_D
    bd = head_of_col == lax.broadcasted_iota(jnp.int32, (SUBLANES, GROUP_W), 0)

    @pl.when(j == 0)
    def _():
        qbd_ref[...] = jnp.where(bd, jnp.broadcast_to(q_ref[...], (SUBLANES, GROUP_W)), 0.0).astype(BF16)
        m_ref[...] = jnp.full_like(m_ref, NEG_INF)
        l_ref[...] = jnp.zeros_like(l_ref)
        acc_ref[...] = jnp.zeros_like(acc_ref)
        carry_ref[...] = jnp.zeros_like(carry_ref)

    later = (lax.broadcasted_iota(jnp.int32, (PAGE_SIZE, PAGE_SIZE), 0)
             > lax.broadcasted_iota(jnp.int32, (PAGE_SIZE, PAGE_SIZE), 1)).astype(F32)
    def page(ref):
        flat = ref.reshape(PAGE_SIZE * N_HEADS, HEAD_D)
        return jnp.concatenate([flat[pl.ds(h, PAGE_SIZE, stride=N_HEADS), :] for h in range(N_HEADS)],
                               axis=1).astype(BF16)

    qbd = qbd_ref[...]
    run = carry_ref[...]
    lps = [lf_refs[r][...] for r in range(pps)]
    suffix = _dot_f32(jnp.concatenate(lps, axis=0), later)
    scores = []
    for r in range(pps):
        bias = run + suffix[r * SUBLANES:(r + 1) * SUBLANES]
        run = run + jnp.sum(lps[r], axis=1, keepdims=True)
        scores.append(_dot_nt(qbd, page(k_refs[r])) * scale + bias)
    carry_ref[...] = run
    m = m_ref[...]
    mn = m
    for s in scores:
        mn = jnp.maximum(mn, jnp.max(s, axis=-1, keepdims=True))
    al = jnp.exp(m - mn)
    l = al * l_ref[...]
    acc = al * acc_ref[...]
    for r, s in enumerate(scores):
        p = jnp.exp(s - mn)
        l = l + jnp.sum(p, axis=-1, keepdims=True)
        acc = acc + _dot(p.astype(BF16), page(v_refs[r]))
    m_ref[...], l_ref[...], acc_ref[...] = mn, l, acc

    @pl.when(j == pl.num_programs(1) - 1)
    def _():
        logf = _log_sigmoid(ff_ref[...] + bf_ref[...])
        logf_ref[...] = jnp.broadcast_to(logf, (SUBLANES, LANES))
        qf = qbd_ref[...].astype(F32)
        kn = kn_ref[...].astype(BF16).astype(F32)
        s_new = jnp.sum(qf * kn, axis=1, keepdims=True) * scale - logf
        m, l, acc = m_ref[...], l_ref[...], acc_ref[...]
        mn = jnp.maximum(m, s_new)
        p = jnp.exp(s_new - mn)
        al = jnp.exp(m - mn)
        out = (al * acc + p * vn_ref[...]) / (al * l + p)
        o_ref[...] = jnp.sum(jnp.where(bd, out, 0.0), axis=0, keepdims=True)


def _fox_caches(cache_k, cache_v, cache_lf):
    ck, cv = cache_k, cache_v
    clf = jnp.pad(jnp.transpose(cache_lf, (0, 1, 3, 2)), ((0, 0), (0, 0), (0, SUBLANES - N_HEADS), (0, 0)))
    return ck, cv, clf


def _fox_step(proj, ff, bf, caches, layer, page_table, anchor, *, nb, pps=16):
    ck, cv, clf = caches
    npages = page_table.shape[1]
    p3 = proj.reshape(nb, 1, proj.shape[1])
    ffc = jnp.pad(ff[:, :N_HEADS], ((0, 0), (0, SUBLANES - N_HEADS))).reshape(nb, SUBLANES, 1)
    bfc = jnp.pad(bf, (0, SUBLANES - N_HEADS)).reshape(SUBLANES, 1)
    row = lambda col: pl.BlockSpec((None, 1, GROUP_W), lambda b, j, pt: (b, 0, col))

    def page(shape, r):
        return pl.BlockSpec((None, None) + shape,
                            lambda b, j, pt: (layer, pt[b, npages - 1 - (j * pps + r)]) + (0,) * len(shape))

    in_specs = ([row(COL_FQ), row(COL_FK), row(COL_FV),
                 pl.BlockSpec((None, SUBLANES, 1), lambda b, j, pt: (b, 0, 0)),
                 pl.BlockSpec((SUBLANES, 1), lambda b, j, pt: (0, 0)),
                 pl.BlockSpec((SUBLANES, LANES), lambda b, j, pt: (0, 0))]
                + [page((PAGE_SIZE, N_HEADS, HEAD_D), r) for r in range(pps)]
                + [page((PAGE_SIZE, N_HEADS, HEAD_D), r) for r in range(pps)]
                + [page((SUBLANES, PAGE_SIZE), r) for r in range(pps)])
    y, logf = pl.pallas_call(
        functools.partial(_fox_step_kernel, pps=pps),
        grid_spec=pltpu.PrefetchScalarGridSpec(
            num_scalar_prefetch=1,
            grid=(nb, npages // pps),
            in_specs=in_specs,
            out_specs=[pl.BlockSpec((None, 1, GROUP_W), lambda b, j, pt: (b, 0, 0)),
                       pl.BlockSpec((None, SUBLANES, LANES), lambda b, j, pt: (b, 0, 0))],
            scratch_shapes=[pltpu.VMEM((SUBLANES, GROUP_W), BF16), pltpu.VMEM((SUBLANES, 1), F32),
                            pltpu.VMEM((SUBLANES, 1), F32), pltpu.VMEM((SUBLANES, GROUP_W), F32),
                            pltpu.VMEM((SUBLANES, 1), F32)]),
        out_shape=[jax.ShapeDtypeStruct((nb, 1, GROUP_W), F32), jax.ShapeDtypeStruct((nb, SUBLANES, LANES), F32)],
        compiler_params=_cparams(("arbitrary", "arbitrary"), 32),
        name="fox_step",
    )(page_table, p3, p3, p3, ffc, bfc, anchor, *([ck] * pps), *([cv] * pps), *([clf] * pps))
    return y.reshape(nb, GROUP_W), logf[:, :N_HEADS, 0]


N_TOP = PEER_TOPK + 1


def _sort_desc(v):
    v, n = list(v), len(v)
    k = 2
    while k <= n:
        j = k // 2
        while j >= 1:
            for i in range(n):
                l = i ^ j
                if l > i:
                    hi, lo = jnp.maximum(v[i], v[l]), jnp.minimum(v[i], v[l])
                    v[i], v[l] = (hi, lo) if (i & k) == 0 else (lo, hi)
            j //= 2
        k *= 2
    return v


def _top_values(s):
    depth = s.shape[0] // SUBLANES
    slabs = [s[r * SUBLANES:(r + 1) * SUBLANES, :] for r in range(depth)]
    slabs += [jnp.full_like(slabs[0], NEG_INF)] * (pl.next_power_of_2(depth) - depth)
    cols = _sort_desc(slabs)
    rows = lax.broadcasted_iota(jnp.int32, (24, 1), 0)
    top = jnp.full((24, s.shape[1]), NEG_INF, F32)
    for r in range(N_TOP):
        m = jnp.max(cols[0], axis=0, keepdims=True)
        top = jnp.where(rows == r, m, top)
        hit = cols[0] == m
        keep = min(len(cols), N_TOP - 1 - r)
        cols = [jnp.where(hit, cols[d + 1] if d + 1 < len(cols) else NEG_INF, cols[d]) for d in range(keep)]
    return top


def _peer_route_kernel(q_ref, sk_ref, s1_ref, s2_ref, tau_ref, *, tmr, t_total):
    live = lax.broadcasted_iota(jnp.int32, (tmr, 1), 0) < t_total - pl.program_id(0) * tmr
    for h in range(PEER_HEADS):
        q = jnp.where(live, q_ref[:, h * 2 * PEER_NKEYS:(h + 1) * 2 * PEER_NKEYS], 0.0).astype(BF16)
        _peer_route_head(q, sk_ref.at[h], s1_ref.at[h], s2_ref.at[h], tau_ref.at[h], tmr)


def _peer_route_head(q, sk_ref, s1_ref, s2_ref, tau_ref, tmr):
    s1 = _dot_nt(sk_ref[0], q[:, :PEER_NKEYS])
    s2 = _dot_nt(sk_ref[1], q[:, PEER_NKEYS:])
    a, b = _top_values(s1), _top_values(s2)
    r24 = lax.broadcasted_iota(jnp.int32, (24, 1), 0)
    r8 = lax.broadcasted_iota(jnp.int32, (8, 1), 0)
    a8, b8 = a[0:8], b[0:8]
    slabs = [a[0:1] + b, jnp.where(r24 >= 1, a + b[0:1], NEG_INF)]
    for i in range(1, 5):
        slabs.append(jnp.where((r8 >= 1) & (r8 < N_TOP // (i + 1)), a[i:i + 1] + b8, NEG_INF))
    slabs.append(jnp.where(r8 >= 5, a8 + b[1:2], NEG_INF))
    cand = jnp.concatenate(slabs, axis=0)

    ctop = _top_values(cand)
    t16, t17 = ctop[N_TOP - 2:N_TOP - 1], ctop[N_TOP - 1:N_TOP]
    m0 = a[0:1] + b[0:1]
    z = jnp.sum(jnp.where(cand >= t16, jnp.exp(cand - m0), 0.0), axis=0, keepdims=True)
    log2z = jnp.log2(z)
    s1n = (s1 - a[0:1]) * LOG2E
    s2n = (s2 - b[0:1]) * LOG2E - log2z
    taun = (0.5 * (t16 + t17) - m0) * LOG2E - log2z
    for q in range(tmr // LANES):
        lanes = slice(q * LANES, (q + 1) * LANES)
        s1_ref[q] = s1n[:, lanes]
        s2_ref[q] = s2n[:, lanes]
        tau_ref[q] = jnp.broadcast_to(taun[:, lanes], (SUBLANES, LANES))


def _peer_route(qp, subkeys):
    t = qp.shape[0]
    tmr = 2 * LANES if t > LANES else LANES
    ntile = pl.cdiv(t, tmr)
    k = tmr // LANES
    sblk = pl.BlockSpec((PEER_HEADS, k, PEER_NKEYS, LANES), lambda i: (0, i, 0, 0))
    return pl.pallas_call(
        functools.partial(_peer_route_kernel, tmr=tmr, t_total=t),
        grid=(ntile,),
        in_specs=[pl.BlockSpec((tmr, PEER_HEADS * 2 * PEER_NKEYS), lambda i: (i, 0)),
                  pl.BlockSpec((PEER_HEADS, 2, PEER_NKEYS, PEER_NKEYS), lambda i: (0, 0, 0, 0))],
        out_specs=[sblk, sblk, pl.BlockSpec((PEER_HEADS, k, SUBLANES, LANES), lambda i: (0, i, 0, 0))],
        out_shape=[jax.ShapeDtypeStruct((PEER_HEADS, ntile * k, PEER_NKEYS, LANES), F32)] * 2
        + [jax.ShapeDtypeStruct((PEER_HEADS, ntile * k, SUBLANES, LANES), F32)],
        compiler_params=_cparams(("arbitrary",), 32),
        name="peer_route",
    )(qp, subkeys)


def _peer_dense_kernel(x_ref, h_ref, u_ref, v_ref, s1_ref, s2_ref, tau_ref, o_ref, xs_ref, xu_ref, ht_ref,
                       *, tm, et, t_total):
    j = pl.program_id(1)

    @pl.when(j == 0)
    def _():
        live = lax.broadcasted_iota(jnp.int32, (tm, 1), 0) < t_total - pl.program_id(0) * tm
        o_ref[...] = jnp.where(live, h_ref[...], 0.0)
        xs_ref[...] = jnp.where(live, x_ref[...], jnp.zeros((), BF16))

    xu_ref[...] = _dot_nt(u_ref[...], xs_ref[...])
    na = et // PEER_NKEYS
    for ap in range(na):
        a_glob = j * na + ap
        rows = slice(ap * PEER_NKEYS, (ap + 1) * PEER_NKEYS)
        for lt in range(tm // LANES):
            lanes = slice(lt * LANES, (lt + 1) * LANES)
            gate = jnp.zeros((PEER_NKEYS, LANES), F32)
            for h in range(PEER_HEADS):
                v2 = s2_ref[h, lt] + s1_ref[h, lt, pl.ds(a_glob, 1), :]
                gate = gate + jnp.exp2(jnp.where(v2 >= tau_ref[h, lt, 0:1, :], v2, NEG_INF))
            ht_ref[rows, lanes] = (_gelu(xu_ref[rows, lanes]) * gate).astype(BF16)
    o_ref[...] += _dot_tn(ht_ref[...], v_ref[...])


def _peer_dense(xn, h, u_tab, v_tab, layer, s1n, s2n, tau):
    t, d = h.shape
    ne = u_tab.shape[1]
    tm = 5 * LANES if t > 5 * LANES else LANES
    et = 4 * PEER_NKEYS
    k = tm // LANES
    rblk = lambda rows: pl.BlockSpec((PEER_HEADS, k, rows, LANES), lambda i, j: (0, i, 0, 0))
    tab = pl.BlockSpec((None, et, d), lambda i, j: (layer, j, 0))
    return pl.pallas_call(
        functools.partial(_peer_dense_kernel, tm=tm, et=et, t_total=t),
        grid=(pl.cdiv(t, tm), ne // et),
        in_specs=[pl.BlockSpec((tm, d), lambda i, j: (i, 0)), pl.BlockSpec((tm, d), lambda i, j: (i, 0)),
                  tab, tab, rblk(PEER_NKEYS), rblk(PEER_NKEYS), rblk(SUBLANES)],
        out_specs=pl.BlockSpec((tm, d), lambda i, j: (i, 0)),
        out_shape=jax.ShapeDtypeStruct((t, d), F32),
        scratch_shapes=[pltpu.VMEM((tm, d), BF16), pltpu.VMEM((et, tm), F32), pltpu.VMEM((et, tm), BF16)],
        compiler_params=_cparams(("arbitrary", "arbitrary"), 56),
        name="peer_dense",
    )(xn, h, u_tab, v_tab, s1n, s2n, tau)


def _peer(h, norm_g, wq, subkeys, u_tab, v_tab, layer):
    t = h.shape[0]
    if t < LANES:
        h = jnp.pad(h, ((0, LANES - t), (0, 0)))
    qp, xn = _q_proj(h, norm_g, wq)
    s1n, s2n, tau = _peer_route(qp, subkeys)
    out = _peer_dense(xn, h, u_tab, v_tab, layer, s1n, s2n, tau)
    return out[:t]


def kernel(x_prompt, x_sample, cache_fox_k, cache_fox_v, cache_fox_logf, page_table, state_s5_re, state_s5_im, state_ret, state_lru, state_conv, meta_tokens, norm_mix_g, w_in, w_out, s5_lam_re, s5_lam_im, s5_log_dt, s5_b_re, s5_b_im, s5_c_re, s5_c_im, s5_d, s5_glu_w, s5_glu_b, fox_bf, ret_norm_g, lru_conv_w, lru_conv_b, lru_wa, lru_ba, lru_wx, lru_bx, lru_lam, norm_ffn_g, peer_wq, peer_subkeys, peer_u, peer_v, norm_final_g):
    bp, seq_x, d = x_prompt.shape
    bs = x_sample.shape[0]
    depth = w_in.shape[0]
    seq = seq_x + N_META
    meta = jnp.broadcast_to(meta_tokens[None], (bp, N_META, d))
    h_p = jnp.concatenate([meta, x_prompt], axis=1).reshape(bp * seq, d)
    h_s = x_sample.reshape(bs, d)
    zeros = lambda *shape: jnp.zeros(shape, F32)
    u_tab, v_tab = peer_u.astype(BF16), peer_v.astype(BF16)
    caches = _fox_caches(cache_fox_k, cache_fox_v, cache_fox_logf)
    w_main, w_ff = _repack_w_in(w_in)
    outs_p, outs_s = [], []
    for l in range(depth):
        w_o = w_out[l].astype(BF16).reshape(4, GROUP_W, d)
        s5w = _s5_weights(s5_lam_re[l], s5_lam_im[l], s5_log_dt[l], s5_b_re[l], s5_b_im[l], s5_c_re[l], s5_c_im[l])
        glu_w = s5_glu_w[l].astype(BF16)
        lru_w = (lru_conv_w[l], lru_conv_b[l], lru_wa[l], lru_ba[l], lru_wx[l], lru_bx[l], lru_lam[l])
        wq = peer_wq[l].astype(BF16)
        subkeys = peer_subkeys[l].astype(BF16)

        proj, ff = _in_proj(h_p, norm_mix_g[l], w_main, w_ff, l)
        y_s5, s5re, s5im = _s5(proj, zeros(bp, S5_STATE_W), zeros(bp, S5_STATE_W), s5w, s5_d[l], glu_w, s5_glu_b[l],
                               nb=bp, seq=seq)
        logf, c_blk = _fox_prep(ff, fox_bf[l], nb=bp, seq=seq)
        y_fox = _fox_prompt(proj, c_blk, nb=bp, seq=seq)
        y_ret, ret_s = _ret_prompt(proj, ret_norm_g[l], nb=bp, seq=seq)
        y_lru, lru_h, conv_buf = _lru(proj, zeros(bp, GROUP_W), zeros(bp, CONV_W - 1, GROUP_W), *lru_w, nb=bp, seq=seq)
        h_p = _out_proj(h_p, (y_s5, y_fox, y_ret, y_lru), w_o)
        h_p = _peer(h_p, norm_ffn_g[l], wq, subkeys, u_tab, v_tab, l)
        p3 = proj.reshape(bp, seq, -1)
        outs_p.append((p3[:, :, COL_FK * GROUP_W:(COL_FK + 1) * GROUP_W].reshape(bp, seq, N_HEADS, HEAD_D),
                       p3[:, :, COL_FV * GROUP_W:(COL_FV + 1) * GROUP_W].reshape(bp, seq, N_HEADS, HEAD_D),
                       logf, s5re.reshape(bp, -1, 64), s5im.reshape(bp, -1, 64), ret_s, lru_h, conv_buf))

        proj, ff = _in_proj(h_s, norm_mix_g[l], w_main, w_ff, l)
        y_s5, s5re, s5im = _s5(proj, state_s5_re[l].reshape(bs, S5_STATE_W), state_s5_im[l].reshape(bs, S5_STATE_W),
                               s5w, s5_d[l], glu_w, s5_glu_b[l], nb=bs, seq=1)
        y_fox, logf = _fox_step(proj, ff, fox_bf[l], caches, l, page_table, h_p, nb=bs)
        y_ret, ret_s = _ret_step(proj, ret_norm_g[l], state_ret[l], nb=bs)
        y_lru, lru_h, conv_buf = _lru(proj, state_lru[l], state_conv[l], *lru_w, nb=bs, seq=1)
        h_s = _out_proj(h_s, (y_s5, y_fox, y_ret, y_lru), w_o)
        h_s = _peer(h_s, norm_ffn_g[l], wq, subkeys, u_tab, v_tab, l)
        outs_s.append((proj[:, COL_FK * GROUP_W:(COL_FK + 1) * GROUP_W].reshape(bs, 1, N_HEADS, HEAD_D),
                       proj[:, COL_FV * GROUP_W:(COL_FV + 1) * GROUP_W].reshape(bs, 1, N_HEADS, HEAD_D),
                       logf.reshape(bs, 1, N_HEADS), s5re.reshape(bs, -1, 64), s5im.reshape(bs, -1, 64),
                       ret_s, lru_h, conv_buf))

    y_prompt = _rmsnorm(h_p, norm_final_g).reshape(bp, seq, d)[:, N_META:]
    y_sample = _rmsnorm(h_s, norm_final_g).reshape(bs, 1, d)
    stk = lambda outs, j: jnp.stack([o[j] for o in outs], axis=0)
    return ((y_prompt, y_sample) + tuple(stk(outs_p, j) for j in range(8)) + tuple(stk(outs_s, j) for j in range(8)))
```
